```python
import math
import jax
import jax.numpy as jnp
from jax import lax
import numpy as np

D_MODEL = 2048
BATCH = 4
SEQ = 4096
DEPTH = 4

N_EVEN = (DEPTH + 1) // 2
N_ODD = DEPTH // 2
NORM_EPS = 1e-6

SSM_INNER = D_MODEL
SSM_HEADDIM = 64
SSM_HEADS = SSM_INNER // SSM_HEADDIM
SSM_GROUPS = 4
SSM_HEADS_PER_GROUP = SSM_HEADS // SSM_GROUPS
SSM_STATE = 128
SSM_CONV_WIDTH = 4
SSM_CHUNK = 128
SSM_CONV_DIM = SSM_INNER + 2 * SSM_GROUPS * SSM_STATE
CONF_WIDTH = D_MODEL // 2
CONF_CONV_WIDTH = 31
AB_IN = SSM_INNER + SSM_CONV_DIM + SSM_HEADS + 2 * CONF_WIDTH
AB_OUT = SSM_INNER + CONF_WIDTH
SC_WIDTH = D_MODEL // 2
SC_CONV_WIDTH = 3
NSA_HEAD_DIM = 128
NSA_HEADS = D_MODEL // NSA_HEAD_DIM
NSA_KV_HEADS = 4
NSA_GROUP = NSA_HEADS // NSA_KV_HEADS
CMP_BLOCK = 32
CMP_STRIDE = 16
SLC_BLOCK = 64
N_SELECT = 16
N_LOCAL = 2
WINDOW = 512
NSA_QBLOCK = 32
ROPE_THETA = 10000.0
NSA_Q = NSA_HEADS * NSA_HEAD_DIM
NSA_KV = NSA_KV_HEADS * NSA_HEAD_DIM
CD_IN = 3 * SC_WIDTH + NSA_Q + 6 * NSA_KV + 3 * NSA_HEADS
CD_OUT = SC_WIDTH + NSA_Q
D_FF = 256 * ((8 * D_MODEL // 3 + 255) // 256)
FFN_CONV_WIDTH = 3
NEG_BIG = -1e30
FORCE = 1e9

kernel_name = "hybrid_ssd_conformer_shortconv_nsa_trunk"


def split_cols(x, sizes):
    return jnp.split(x, [int(c) for c in np.cumsum(sizes)[:-1]], axis=-1)


def rmsnorm(x, g):
    xf = x.astype(jnp.float32)
    y = xf * lax.rsqrt(jnp.mean(xf * xf, axis=-1, keepdims=True) + NORM_EPS)
    return (y * g.astype(jnp.float32)).astype(x.dtype)


def layernorm(x, g, b):
    xf = x.astype(jnp.float32)
    mu = jnp.mean(xf, axis=-1, keepdims=True)
    var = jnp.mean(jnp.square(xf - mu), axis=-1, keepdims=True)
    y = (xf - mu) * lax.rsqrt(var + NORM_EPS) * g.astype(jnp.float32) + b.astype(jnp.float32)
    return y.astype(x.dtype)


def causal_dwconv(x, w, b=None):
    width, ch = w.shape
    y = lax.conv_general_dilated(
        x, w[:, None, :].astype(x.dtype), window_strides=(1,), padding=[(width - 1, 0)],
        dimension_numbers=("NWC", "WIO", "NWC"), feature_group_count=ch)
    return y if b is None else y + b


def rope(x, pos):
    half = x.shape[-1] // 2
    inv = ROPE_THETA ** (-jnp.arange(half, dtype=jnp.float32) / half)
    ang = pos.astype(jnp.float32)[..., None] * inv
    cos = jnp.cos(ang)[:, :, None, :]
    sin = jnp.sin(ang)[:, :, None, :]
    xf = x.astype(jnp.float32)
    x1, x2 = xf[..., :half], xf[..., half:]
    return jnp.concatenate([x1 * cos - x2 * sin, x2 * cos + x1 * sin], axis=-1).astype(x.dtype)


def masked_softmax(s, mask):
    s = jnp.where(mask, s.astype(jnp.float32), NEG_BIG)
    return jax.nn.softmax(s, axis=-1) * mask


def ssd_chunked(xh, dt, a, bmat, cmat):
    bsz, seq = xh.shape[:2]
    nc = seq // SSM_CHUNK
    x = (xh * dt[..., None]).reshape(bsz, nc, SSM_CHUNK, SSM_GROUPS, SSM_HEADS_PER_GROUP, SSM_HEADDIM)
    la = jnp.moveaxis((dt * a).reshape(bsz, nc, SSM_CHUNK, SSM_GROUPS, SSM_HEADS_PER_GROUP), 2, -1)
    cum = jnp.cumsum(la, axis=-1)
    bm = bmat.reshape(bsz, nc, SSM_CHUNK, SSM_GROUPS, SSM_STATE)
    cm = cmat.reshape(bsz, nc, SSM_CHUNK, SSM_GROUPS, SSM_STATE)
    causal = jnp.tril(jnp.ones((SSM_CHUNK, SSM_CHUNK), dtype=bool))
    decay = jnp.exp(jnp.where(causal, cum[..., :, None] - cum[..., None, :], -jnp.inf))
    cb = jnp.einsum("bclgn,bcsgn->bcgls", cm, bm)
    y_diag = jnp.einsum("bcgls,bcgels,bcsgep->bclgep", cb, decay, x)
    decay_to_end = jnp.exp(cum[..., -1:] - cum)
    states = jnp.einsum("bcsgn,bcges,bcsgep->bcgepn", bm, decay_to_end, x)
    chunk_decay = jnp.exp(cum[..., -1])

    def step(h, inp):
        d_c, st_c = inp
        return h * d_c[..., None, None] + st_c, h

    h0 = jnp.zeros((bsz, SSM_GROUPS, SSM_HEADS_PER_GROUP, SSM_HEADDIM, SSM_STATE), jnp.float32)
    _, h_in = lax.scan(step, h0, (jnp.moveaxis(chunk_decay, 1, 0), jnp.moveaxis(states, 1, 0)))
    h_in = jnp.moveaxis(h_in, 0, 1)
    y_off = jnp.einsum("bclgn,bcgepn,bcgel->bclgep", cm, h_in, jnp.exp(cum))
    return (y_diag + y_off).reshape(bsz, seq, SSM_GROUPS, SSM_HEADS_PER_GROUP, SSM_HEADDIM)


def even_mixer(h, w_in, conv_w, conv_b, dt_bias, a_log, d_skip, ssm_norm,
               conf_dw_w, conf_dw_b, conf_ln_g, conf_ln_b, w_out):
    bsz, seq, _ = h.shape
    f32 = jnp.float32
    proj = h @ w_in
    z, xbc, dt_raw, glu = split_cols(proj, [SSM_INNER, SSM_CONV_DIM, SSM_HEADS, 2 * CONF_WIDTH])
    xbc = jax.nn.silu(causal_dwconv(xbc, conv_w, conv_b))
    xs, bm, cm = split_cols(xbc, [SSM_INNER, SSM_GROUPS * SSM_STATE, SSM_GROUPS * SSM_STATE])
    dt = jax.nn.softplus(dt_raw.astype(f32) + dt_bias.astype(f32))
    dt = dt.reshape(bsz, seq, SSM_GROUPS, SSM_HEADS_PER_GROUP)
    a = -jnp.exp(a_log.astype(f32)).reshape(SSM_GROUPS, SSM_HEADS_PER_GROUP)
    xh = xs.astype(f32).reshape(bsz, seq, SSM_GROUPS, SSM_HEADS_PER_GROUP, SSM_HEADDIM)
    y = ssd_chunked(xh, dt, a,
                    bm.astype(f32).reshape(bsz, seq, SSM_GROUPS, SSM_STATE),
                    cm.astype(f32).reshape(bsz, seq, SSM_GROUPS, SSM_STATE))
    y = y + d_skip.astype(f32).reshape(SSM_GROUPS, SSM_HEADS_PER_GROUP)[..., None] * xh
    y = y.reshape(bsz, seq, SSM_INNER).astype(h.dtype) * jax.nn.silu(z)
    y = rmsnorm(y.reshape(bsz, seq, SSM_GROUPS, -1), ssm_norm.reshape(SSM_GROUPS, -1)).reshape(bsz, seq, SSM_INNER)
    ua, ug = jnp.split(glu, 2, axis=-1)
    u = ua * jax.nn.sigmoid(ug)
    u = causal_dwconv(u, conf_dw_w, conf_dw_b)
    u = jax.nn.silu(layernorm(u, conf_ln_g, conf_ln_b))
    return jnp.concatenate([y, u], axis=-1) @ w_out


def nsa_attention(q, k_cmp_tok, v_cmp_tok, k_slc, v_slc, k_win, v_win, gates, positions,
                  cmp_pe, cmp_w1, cmp_w2):
    bsz, seq = q.shape[:2]
    dt = q.dtype
    scale = NSA_HEAD_DIM ** -0.5
    q = rope(q, positions)
    k_slc = rope(k_slc, positions)
    k_win = rope(k_win, positions)
    qg = q.reshape(bsz, seq, NSA_KV_HEADS, NSA_GROUP, NSA_HEAD_DIM)
    t = jnp.arange(seq)

    n_cmp = (seq - CMP_BLOCK) // CMP_STRIDE + 1
    starts = jnp.arange(n_cmp) * CMP_STRIDE
    ends = starts + CMP_BLOCK - 1
    idx = starts[:, None] + jnp.arange(CMP_BLOCK)[None, :]

    def compress(tok, pe, w1, w2):
        blk = tok[:, idx] + pe[:, None, :]
        blk = jnp.moveaxis(blk, 3, 2).reshape(bsz, n_cmp, NSA_KV_HEADS, CMP_BLOCK * NSA_HEAD_DIM)
        return jax.nn.silu(blk @ w1) @ w2

    kc = rope(compress(k_cmp_tok, cmp_pe[0], cmp_w1[0], cmp_w2[0]), positions[:, ends])
    vc = compress(v_cmp_tok, cmp_pe[1], cmp_w1[1], cmp_w2[1])
    s_cmp = jnp.einsum("bsged,bngd->bsgen", qg, kc) * scale
    cmp_mask = (ends[None, :] <= t[:, None])[None, :, None, None, :]
    p_cmp = masked_softmax(s_cmp, cmp_mask)
    o_cmp = jnp.einsum("bsgen,bngd->bsged", p_cmp.astype(dt), vc)

    n_slc = seq // SLC_BLOCK
    n_sel = min(N_SELECT, n_slc)
    j = jnp.arange(n_slc)
    overlap = ((starts[:, None] < (j[None, :] + 1) * SLC_BLOCK)
               & (ends[:, None] >= j[None, :] * SLC_BLOCK)).astype(jnp.float32)
    imp = jnp.einsum("bsgen,nj->bsgj", p_cmp, overlap)
    cur = (t // SLC_BLOCK)[:, None]
    forced = (j[None, :] == 0) | ((j[None, :] <= cur) & (j[None, :] > cur - N_LOCAL))
    valid = j[None, :] * SLC_BLOCK <= t[:, None]
    imp = jnp.where(forced[None, :, None, :], FORCE, jnp.where(valid[None, :, None, :], imp, -FORCE))
    _, sel = lax.top_k(imp, n_sel)

    kb = jnp.moveaxis(k_slc.reshape(bsz, n_slc, SLC_BLOCK, NSA_KV_HEADS, NSA_HEAD_DIM), 3, 1)
    vb = jnp.moveaxis(v_slc.reshape(bsz, n_slc, SLC_BLOCK, NSA_KV_HEADS, NSA_HEAD_DIM), 3, 1)
    kpad = jnp.pad(k_win, ((0, 0), (WINDOW, 0), (0, 0), (0, 0)))
    vpad = jnp.pad(v_win, ((0, 0), (WINDOW, 0), (0, 0), (0, 0)))
    bi = jnp.arange(bsz)[:, None, None, None]
    gi = jnp.arange(NSA_KV_HEADS)[None, None, :, None]
    n_qb = seq // NSA_QBLOCK
    band = WINDOW + NSA_QBLOCK

    def query_block(args):
        qb, selb, i = args
        tq = i * NSA_QBLOCK + jnp.arange(NSA_QBLOCK)
        ks = kb[bi, gi, selb]
        vs = vb[bi, gi, selb]
        s = jnp.einsum("bqged,bqgnld->bqgenl", qb, ks) * scale
        kpos = selb[..., None] * SLC_BLOCK + jnp.arange(SLC_BLOCK)
        m = (kpos <= tq[None, :, None, None, None])[:, :, :, None]
        p = masked_softmax(s.reshape(s.shape[:4] + (-1,)), m.reshape(m.shape[:4] + (-1,)))
        o_s = jnp.einsum("bqgek,bqgkd->bqged", p.astype(dt),
                         vs.reshape(bsz, NSA_QBLOCK, NSA_KV_HEADS, -1, NSA_HEAD_DIM))
        kw = lax.dynamic_slice_in_dim(kpad, i * NSA_QBLOCK, band, axis=1)
        vw = lax.dynamic_slice_in_dim(vpad, i * NSA_QBLOCK, band, axis=1)
        s = jnp.einsum("bqged,bkgd->bqgek", qb, kw) * scale
        kp = i * NSA_QBLOCK - WINDOW + jnp.arange(band)
        mw = (kp[None, :] <= tq[:, None]) & (kp[None, :] > tq[:, None] - WINDOW) & (kp[None, :] >= 0)
        p = masked_softmax(s, mw[None, :, None, None, :])
        o_w = jnp.einsum("bqgek,bkgd->bqged", p.astype(dt), vw)
        return o_s, o_w

    qblk = jnp.moveaxis(qg.reshape(bsz, n_qb, NSA_QBLOCK, NSA_KV_HEADS, NSA_GROUP, NSA_HEAD_DIM), 1, 0)
    sblk = jnp.moveaxis(sel.reshape(bsz, n_qb, NSA_QBLOCK, NSA_KV_HEADS, n_sel), 1, 0)
    o_slc, o_win = lax.map(query_block, (qblk, sblk, jnp.arange(n_qb)))
    shp = (bsz, seq, NSA_KV_HEADS, NSA_GROUP, NSA_HEAD_DIM)
    o_slc = jnp.moveaxis(o_slc, 0, 1).reshape(shp)
    o_win = jnp.moveaxis(o_win, 0, 1).reshape(shp)
    g = jax.nn.sigmoid(gates.astype(jnp.float32)).astype(dt).reshape(bsz, seq, NSA_KV_HEADS, NSA_GROUP, 3)
    o = g[..., 0:1] * o_cmp + g[..., 1:2] * o_slc + g[..., 2:3] * o_win
    return o.reshape(bsz, seq, NSA_Q)


def odd_mixer(h, positions, w_in, sc_conv_w, cmp_pe, cmp_w1, cmp_w2, w_out):
    bsz, seq, _ = h.shape
    proj = h @ w_in
    sc_b, sc_c, sc_h, q, kc, vc, ks, vs, kw, vw, gates = split_cols(
        proj, [SC_WIDTH] * 3 + [NSA_Q] + [NSA_KV] * 6 + [3 * NSA_HEADS])
    y_c = sc_b * causal_dwconv(sc_c * sc_h, sc_conv_w)
    kv = lambda a: a.reshape(bsz, seq, NSA_KV_HEADS, NSA_HEAD_DIM)
    y_d = nsa_attention(q.reshape(bsz, seq, NSA_HEADS, NSA_HEAD_DIM), kv(kc), kv(vc), kv(ks), kv(vs),
                        kv(kw), kv(vw), gates, positions, cmp_pe, cmp_w1, cmp_w2)
    return jnp.concatenate([y_c, y_d], axis=-1) @ w_out


def conv_ffn(h, w_up, conv_w, conv_b, w_down):
    u = causal_dwconv(h @ w_up, conv_w, conv_b)
    g, v = jnp.split(u, 2, axis=-1)
    return (jax.nn.silu(g) * v) @ w_down


def setup_inputs(seed: int = 0) -> dict:
    key = jax.random.key(seed)
    keys = iter(jax.random.split(key, 40))

    def normal(shape, scale):
        return scale * jax.random.normal(next(keys), shape, jnp.float32)

    def gain(shape):
        return 1.0 + normal(shape, 0.02)

    out_scale = (2.0 * DEPTH) ** -0.5
    x = jax.random.normal(next(keys), (BATCH, SEQ, D_MODEL), jnp.float32)
    offset = jax.random.randint(next(keys), (BATCH, 1), 0, 1024, dtype=jnp.int32)
    positions = offset + jnp.arange(SEQ, dtype=jnp.int32)[None, :]
    dt0 = jnp.exp(jax.random.uniform(next(keys), (N_EVEN, SSM_HEADS), jnp.float32,
                                     math.log(1e-3), math.log(1e-1)))
    ssm_dt_bias = dt0 + jnp.log(-jnp.expm1(-dt0))
    ssm_a_log = jnp.log(jax.random.uniform(next(keys), (N_EVEN, SSM_HEADS), jnp.float32, 1.0, 16.0))
    return {
        "x": x,
        "positions": positions,
        "mix_norm": gain((DEPTH, D_MODEL)),
        "ffn_norm": gain((DEPTH, D_MODEL)),
        "final_norm": gain((D_MODEL,)),
        "ab_w_in": normal((N_EVEN, D_MODEL, AB_IN), D_MODEL ** -0.5),
        "ssm_conv_w": normal((N_EVEN, SSM_CONV_WIDTH, SSM_CONV_DIM), SSM_CONV_WIDTH ** -0.5),
        "ssm_conv_b": normal((N_EVEN, SSM_CONV_DIM), 0.01),
        "ssm_dt_bias": ssm_dt_bias,
        "ssm_a_log": ssm_a_log,
        "ssm_d": gain((N_EVEN, SSM_HEADS)),
        "ssm_norm": gain((N_EVEN, SSM_INNER)),
        "conf_dw_w": normal((N_EVEN, CONF_CONV_WIDTH, CONF_WIDTH), CONF_CONV_WIDTH ** -0.5),
        "conf_dw_b": normal((N_EVEN, CONF_WIDTH), 0.01),
        "conf_ln_g": gain((N_EVEN, CONF_WIDTH)),
        "conf_ln_b": normal((N_EVEN, CONF_WIDTH), 0.01),
        "ab_w_out": normal((N_EVEN, AB_OUT, D_MODEL), AB_OUT ** -0.5 * out_scale),
        "cd_w_in": normal((N_ODD, D_MODEL, CD_IN), D_MODEL ** -0.5),
        "sc_conv_w": normal((N_ODD, SC_CONV_WIDTH, SC_WIDTH), SC_CONV_WIDTH ** -0.5),
        "nsa_cmp_pe": normal((N_ODD, 2, CMP_BLOCK, NSA_HEAD_DIM), 0.1),
        "nsa_cmp_w1": normal((N_ODD, 2, CMP_BLOCK * NSA_HEAD_DIM, NSA_HEAD_DIM), (CMP_BLOCK * NSA_HEAD_DIM) ** -0.5),
        "nsa_cmp_w2": normal((N_ODD, 2, NSA_HEAD_DIM, NSA_HEAD_DIM), NSA_HEAD_DIM ** -0.5),
        "cd_w_out": normal((N_ODD, CD_OUT, D_MODEL), CD_OUT ** -0.5 * out_scale),
        "ffn_w_up": normal((DEPTH, D_MODEL, 2 * D_FF), D_MODEL ** -0.5),
        "ffn_conv_w": normal((DEPTH, FFN_CONV_WIDTH, 2 * D_FF), FFN_CONV_WIDTH ** -0.5),
        "ffn_conv_b": normal((DEPTH, 2 * D_FF), 0.01),
        "ffn_w_down": normal((DEPTH, D_FF, D_MODEL), D_FF ** -0.5 * out_scale),
    }


def reference(x, positions, mix_norm, ffn_norm, final_norm, ab_w_in, ssm_conv_w, ssm_conv_b,
              ssm_dt_bias, ssm_a_log, ssm_d, ssm_norm, conf_dw_w, conf_dw_b, conf_ln_g, conf_ln_b,
              ab_w_out, cd_w_in, sc_conv_w, nsa_cmp_pe, nsa_cmp_w1, nsa_cmp_w2, cd_w_out,
              ffn_w_up, ffn_conv_w, ffn_conv_b, ffn_w_down):
    h = x
    for layer in range(DEPTH):
        i = layer // 2
        u = rmsnorm(h, mix_norm[layer])
        if layer % 2 == 0:
            h = h + even_mixer(u, ab_w_in[i], ssm_conv_w[i], ssm_conv_b[i], ssm_dt_bias[i], ssm_a_log[i],
                               ssm_d[i], ssm_norm[i], conf_dw_w[i], conf_dw_b[i], conf_ln_g[i],
                               conf_ln_b[i], ab_w_out[i])
        else:
            h = h + odd_mixer(u, positions, cd_w_in[i], sc_conv_w[i], nsa_cmp_pe[i], nsa_cmp_w1[i],
                              nsa_cmp_w2[i], cd_w_out[i])
        u = rmsnorm(h, ffn_norm[layer])
        h = h + conv_ffn(u, ffn_w_up[layer], ffn_conv_w[layer], ffn_conv_b[layer], ffn_w_down[layer])
    return rmsnorm(h, final_norm)
```

```python
import functools

import jax
import jax.numpy as jnp
from jax import lax
from jax.experimental import pallas as pl
from jax.experimental.pallas import tpu as pltpu

F32 = jnp.float32
BF16 = jnp.bfloat16

D_MODEL = 2048
NORM_EPS = 1e-6
SSM_INNER = 2048
SSM_HEADDIM = 64
SSM_HEADS = 32
SSM_GROUPS = 4
SSM_STATE = 128
SSM_CONV_WIDTH = 4
SSM_CHUNK = 128
CONF_WIDTH = 1024
CONF_CONV_WIDTH = 31
SC_WIDTH = 1024
SC_CONV_WIDTH = 3
NSA_HEAD_DIM = 128
NSA_HEADS = 16
NSA_KV_HEADS = 4
NSA_GROUP = 4
CMP_BLOCK = 32
CMP_STRIDE = 16
SLC_BLOCK = 64
N_SELECT = 16
N_LOCAL = 2
WINDOW = 512
ROPE_THETA = 10000.0
NSA_Q = 2048
NSA_KV = 512
D_FF = 5632
FFN_CONV_WIDTH = 3
NEG_BIG = -1e30
FORCE = 1e9

V7X_VMEM_BYTES = 64 * 1024 * 1024
VMEM_LIMIT = V7X_VMEM_BYTES - 8 * 1024 * 1024
LANES = 128
SUBLANES = 8

TM_PROJ = 1024
TN_PROJ = 1024
TM_OUT = 512
TN_OUT = 1024
TS_FFN = 512
TC_FFN = 512
TM_CONF = 256
CONF_HALO = 32
TM_SC = 512
TM_ROPE = 1024
TQ = 128
TK = 256


def _cparams(*sem):
    return pltpu.CompilerParams(dimension_semantics=sem, vmem_limit_bytes=VMEM_LIMIT)


def _silu(x):
    return x * jax.nn.sigmoid(x)


def _softplus(x):
    return jnp.maximum(x, 0.0) + jnp.log1p(jnp.exp(-jnp.abs(x)))


def _split3(x):
    hi = x.astype(BF16)
    r1 = x - hi.astype(F32)
    mid = r1.astype(BF16)
    lo = (r1 - mid.astype(F32)).astype(BF16)
    return hi, mid, lo


def _dot(a, b):
    return jnp.dot(a, b, preferred_element_type=F32)


def _dot_nt(a, b):
    return lax.dot_general(a, b, (((1,), (1,)), ((), ())), preferred_element_type=F32)


def _dot_f32_rhs01(x, e01):
    hi, mid, lo = _split3(x)
    return _dot(hi, e01) + _dot(mid, e01) + _dot(lo, e01)


def _dot_f32_lhs01(e01, x):
    hi, mid, lo = _split3(x)
    return _dot(e01, hi) + _dot(e01, mid) + _dot(e01, lo)


def _dwconv(ext_ref, x, halo, first, w_ref, width):
    rows = x.shape[0]
    hrows = halo.shape[0]
    ext_ref[0:hrows, :] = jnp.where(first, 0.0, halo)
    ext_ref[hrows:hrows + rows, :] = x
    acc = None
    for k in range(width):
        off = hrows - (width - 1) + k
        term = w_ref[k:k + 1, :] * ext_ref[off:off + rows, :]
        acc = term if acc is None else acc + term
    return acc


def _norm_matmul_kernel(h_ref, g_ref, w_ref, o_ref, xn_ref):
    @pl.when(pl.program_id(1) == 0)
    def _():
        x = h_ref[...]
        ms = jnp.mean(x * x, axis=-1, keepdims=True)
        xn_ref[...] = (x * lax.rsqrt(ms + NORM_EPS) * g_ref[...]).astype(BF16)

    o_ref[...] = _dot(xn_ref[...], w_ref[...]).astype(o_ref.dtype)


def norm_matmul(h, gain, w, *, tm=TM_PROJ, tn=TN_PROJ):
    t, k = h.shape
    n = w.shape[1]
    tn = min(tn, n)
    return pl.pallas_call(
        _norm_matmul_kernel,
        grid=(t // tm, n // tn),
        in_specs=[pl.BlockSpec((tm, k), lambda i, j: (i, 0)),
                  pl.BlockSpec((1, k), lambda i, j: (0, 0)),
                  pl.BlockSpec((k, tn), lambda i, j: (0, j))],
        out_specs=pl.BlockSpec((tm, tn), lambda i, j: (i, j)),
        out_shape=jax.ShapeDtypeStruct((t, n), F32),
        scratch_shapes=[pltpu.VMEM((tm, k), BF16)],
        compiler_params=_cparams("parallel", "arbitrary"),
    )(h, gain.reshape(1, k), w)


def _matmul_res_kernel(*refs, n_in):
    a_refs, w_refs = refs[:n_in], refs[n_in:2 * n_in]
    r_ref, o_ref = refs[2 * n_in], refs[2 * n_in + 1]
    acc = _dot(a_refs[0][...], w_refs[0][...])
    for a_ref, w_ref in zip(a_refs[1:], w_refs[1:]):
        acc = acc + _dot(a_ref[...], w_ref[...])
    o_ref[...] = r_ref[...] + acc


def matmul_res(a_list, w_list, res, *, tm=TM_OUT, tn=TN_OUT):
    t, n = res.shape
    n_in = len(a_list)
    in_specs = ([pl.BlockSpec((tm, a.shape[1]), lambda i, j: (i, 0)) for a in a_list]
                + [pl.BlockSpec((w.shape[0], tn), lambda i, j: (0, j)) for w in w_list]
                + [pl.BlockSpec((tm, tn), lambda i, j: (i, j))])
    return pl.pallas_call(
        functools.partial(_matmul_res_kernel, n_in=n_in),
        grid=(t // tm, n // tn),
        in_specs=in_specs,
        out_specs=pl.BlockSpec((tm, tn), lambda i, j: (i, j)),
        out_shape=jax.ShapeDtypeStruct((t, n), F32),
        compiler_params=_cparams("parallel", "parallel"),
    )(*a_list, *w_list, res)


def _rmsnorm_kernel(h_ref, g_ref, o_ref):
    x = h_ref[...]
    ms = jnp.mean(x * x, axis=-1, keepdims=True)
    o_ref[...] = x * lax.rsqrt(ms + NORM_EPS) * g_ref[...]


def rmsnorm_rows(h, gain, *, tm=512):
    t, k = h.shape
    return pl.pallas_call(
        _rmsnorm_kernel,
        grid=(t // tm,),
        in_specs=[pl.BlockSpec((tm, k), lambda i: (i, 0)), pl.BlockSpec((1, k), lambda i: (0, 0))],
        out_specs=pl.BlockSpec((tm, k), lambda i: (i, 0)),
        out_shape=jax.ShapeDtypeStruct((t, k), F32),
        compiler_params=_cparams("parallel"),
    )(h, gain.reshape(1, k))


def _ffn_gate_kernel(g_ref, gh_ref, v_ref, vh_ref, wg_ref, wv_ref, bg_ref, bv_ref, o_ref,
                     extg_ref, extv_ref, *, tiles_per_seq):
    first = (pl.program_id(0) % tiles_per_seq) == 0
    gc = _dwconv(extg_ref, g_ref[...], gh_ref[...], first, wg_ref, FFN_CONV_WIDTH) + bg_ref[...]
    vc = _dwconv(extv_ref, v_ref[...], vh_ref[...], first, wv_ref, FFN_CONV_WIDTH) + bv_ref[...]
    o_ref[...] = (_silu(gc) * vc).astype(o_ref.dtype)


def ffn_gate(u, conv_w, conv_b, seq, *, ts=TS_FFN, tc=TC_FFN):
    t = u.shape[0]
    nf = D_FF // tc
    hb = ts // SUBLANES
    halo = lambda i, j: (jnp.maximum(i * hb - 1, 0), j)
    halo_v = lambda i, j: (jnp.maximum(i * hb - 1, 0), j + nf)
    return pl.pallas_call(
        functools.partial(_ffn_gate_kernel, tiles_per_seq=seq // ts),
        grid=(t // ts, nf),
        in_specs=[pl.BlockSpec((ts, tc), lambda i, j: (i, j)),
                  pl.BlockSpec((SUBLANES, tc), halo),
                  pl.BlockSpec((ts, tc), lambda i, j: (i, j + nf)),
                  pl.BlockSpec((SUBLANES, tc), halo_v),
                  pl.BlockSpec((FFN_CONV_WIDTH, tc), lambda i, j: (0, j)),
                  pl.BlockSpec((FFN_CONV_WIDTH, tc), lambda i, j: (0, j + nf)),
                  pl.BlockSpec((1, tc), lambda i, j: (0, j)),
                  pl.BlockSpec((1, tc), lambda i, j: (0, j + nf))],
        out_specs=pl.BlockSpec((ts, tc), lambda i, j: (i, j)),
        out_shape=jax.ShapeDtypeStruct((t, D_FF), BF16),
        scratch_shapes=[pltpu.VMEM((ts + SUBLANES, tc), F32), pltpu.VMEM((ts + SUBLANES, tc), F32)],
        compiler_params=_cparams("parallel", "parallel"),
    )(u, u, u, u, conv_w, conv_w, conv_b.reshape(1, -1), conv_b.reshape(1, -1))


def _ssd_kernel(z_ref, xs_ref, xsh_ref, bc_ref, bch_ref, dt_ref, cwx_ref, cbx_ref, cwb_ref, cbb_ref,
                dtb_ref, alog_ref, dsk_ref, gn_ref, y_ref, extx_ref, extb_ref, state_ref):
    L = SSM_CHUNK
    GW = SSM_INNER // SSM_GROUPS
    first = pl.program_id(1) == 0

    @pl.when(first)
    def _():
        state_ref[...] = jnp.zeros_like(state_ref)

    xs = _silu(_dwconv(extx_ref, xs_ref[...], xsh_ref[...], first, cwx_ref, SSM_CONV_WIDTH) + cbx_ref[...])
    bc = _silu(_dwconv(extb_ref, bc_ref[...], bch_ref[...], first, cwb_ref, SSM_CONV_WIDTH) + cbb_ref[...])

    dt = _softplus(dt_ref[...] + dtb_ref[...])
    a = -jnp.exp(alog_ref[...])
    la = dt * a
    row = lax.broadcasted_iota(jnp.int32, (L, L), 0)
    col = lax.broadcasted_iota(jnp.int32, (L, L), 1)
    causal = col <= row
    cum = _dot_f32_lhs01(causal.astype(BF16), la)
    cum_t = cum.T
    clast = cum[L - 1:L, :]

    e_head = (lax.broadcasted_iota(jnp.int32, (LANES, SSM_INNER), 1) // SSM_HEADDIM
              == lax.broadcasted_iota(jnp.int32, (LANES, SSM_INNER), 0)).astype(BF16)
    x = xs * _dot_f32_rhs01(dt, e_head)
    ecum_x = _dot_f32_rhs01(jnp.exp(cum), e_head)
    dte_x = _dot_f32_rhs01(jnp.exp(clast - cum), e_head)
    cdec_x = _dot_f32_rhs01(jnp.broadcast_to(jnp.exp(clast), (SUBLANES, LANES)), e_head)[0:1, :]
    xb = x.astype(BF16)
    xdte = (x * dte_x).astype(BF16)
    lo_half = lax.broadcasted_iota(jnp.int32, (L, LANES), 1) < SSM_HEADDIM

    for g in range(SSM_GROUPS):
        bg = bc[:, g * SSM_STATE:(g + 1) * SSM_STATE]
        cg = bc[:, (SSM_GROUPS + g) * SSM_STATE:(SSM_GROUPS + g + 1) * SSM_STATE].astype(BF16)
        cb = _dot_nt(cg, bg.astype(BF16))
        hg = state_ref[g]
        y_off = _dot(cg, hg.astype(BF16)) * ecum_x[:, g * GW:(g + 1) * GW]
        pieces = []
        for pr in range(GW // LANES):
            h0 = g * (GW // SSM_HEADDIM) + 2 * pr
            xp = xb[:, h0 * SSM_HEADDIM:h0 * SSM_HEADDIM + LANES]
            ypair = None
            for s in range(2):
                h = h0 + s
                diff = cum[:, h:h + 1] - cum_t[h:h + 1, :]
                decay = jnp.exp(jnp.where(causal, diff, -jnp.inf))
                m = (cb * decay).astype(BF16)
                keep = lo_half if s == 0 else jnp.logical_not(lo_half)
                yh = _dot(m, jnp.where(keep, xp, jnp.zeros_like(xp)))
                ypair = yh if ypair is None else ypair + yh
            pieces.append(ypair)
        y_diag = jnp.concatenate(pieces, axis=-1)
        st = _dot(bg.T.astype(BF16), xdte[:, g * GW:(g + 1) * GW])
        state_ref[g] = hg * cdec_x[:, g * GW:(g + 1) * GW] + st

        yg = y_diag + y_off + dsk_ref[:, g * GW:(g + 1) * GW] * xs[:, g * GW:(g + 1) * GW]
        yg = yg * _silu(z_ref[:, g * GW:(g + 1) * GW])
        ms = jnp.mean(yg * yg, axis=-1, keepdims=True)
        yg = yg * lax.rsqrt(ms + NORM_EPS) * gn_ref[:, g * GW:(g + 1) * GW]
        y_ref[:, g * GW:(g + 1) * GW] = yg.astype(y_ref.dtype)


def ssd_mixer(proj, dt_raw, conv_w, conv_b, dt_bias, a_log, d_skip, ssm_norm, bsz, seq):
    L = SSM_CHUNK
    nc = seq // L
    hb = L // SUBLANES
    pad = lambda v: jnp.pad(v.reshape(1, -1), ((0, 0), (0, LANES - v.shape[-1])))
    rowblk = lambda b, c: b * nc + c
    halo_row = lambda b, c: jnp.maximum((b * nc + c) * hb - 1, 0)
    const = lambda b, c: (0, 0)
    bcw = 2 * SSM_GROUPS * SSM_STATE
    return pl.pallas_call(
        _ssd_kernel,
        grid=(bsz, nc),
        in_specs=[pl.BlockSpec((L, SSM_INNER), lambda b, c: (rowblk(b, c), 0)),
                  pl.BlockSpec((L, SSM_INNER), lambda b, c: (rowblk(b, c), 1)),
                  pl.BlockSpec((SUBLANES, SSM_INNER), lambda b, c: (halo_row(b, c), 1)),
                  pl.BlockSpec((L, bcw), lambda b, c: (rowblk(b, c), 4)),
                  pl.BlockSpec((SUBLANES, bcw), lambda b, c: (halo_row(b, c), 4)),
                  pl.BlockSpec((L, LANES), lambda b, c: (rowblk(b, c), 0)),
                  pl.BlockSpec((SSM_CONV_WIDTH, SSM_INNER), const),
                  pl.BlockSpec((1, SSM_INNER), const),
                  pl.BlockSpec((SSM_CONV_WIDTH, bcw), const),
                  pl.BlockSpec((1, bcw), const),
                  pl.BlockSpec((1, LANES), const),
                  pl.BlockSpec((1, LANES), const),
                  pl.BlockSpec((1, SSM_INNER), const),
                  pl.BlockSpec((1, SSM_INNER), const)],
        out_specs=pl.BlockSpec((L, SSM_INNER), lambda b, c: (rowblk(b, c), 0)),
        out_shape=jax.ShapeDtypeStruct((bsz * seq, SSM_INNER), BF16),
        scratch_shapes=[pltpu.VMEM((L + SUBLANES, SSM_INNER), F32),
                        pltpu.VMEM((L + SUBLANES, bcw), F32),
                        pltpu.VMEM((SSM_GROUPS, SSM_STATE, SSM_INNER // SSM_GROUPS), F32)],
        compiler_params=_cparams("parallel", "arbitrary"),
    )(proj, proj, proj, proj, proj, dt_raw,
      conv_w[:, :SSM_INNER], conv_b[:SSM_INNER].reshape(1, -1),
      conv_w[:, SSM_INNER:], conv_b[SSM_INNER:].reshape(1, -1),
      pad(dt_bias), pad(a_log), jnp.repeat(d_skip, SSM_HEADDIM).reshape(1, -1), ssm_norm.reshape(1, -1))


def _conformer_kernel(ua_ref, ug_ref, uah_ref, ugh_ref, w_ref, b_ref, lg_ref, lb_ref, o_ref, ext_ref,
                      *, tiles_per_seq):
    first = (pl.program_id(0) % tiles_per_seq) == 0
    u = ua_ref[...] * jax.nn.sigmoid(ug_ref[...])
    uh = uah_ref[...] * jax.nn.sigmoid(ugh_ref[...])
    c = _dwconv(ext_ref, u, uh, first, w_ref, CONF_CONV_WIDTH) + b_ref[...]
    mu = jnp.mean(c, axis=-1, keepdims=True)
    d = c - mu
    var = jnp.mean(d * d, axis=-1, keepdims=True)
    y = d * lax.rsqrt(var + NORM_EPS) * lg_ref[...] + lb_ref[...]
    o_ref[...] = _silu(y).astype(o_ref.dtype)


def conformer_mixer(proj, dw_w, dw_b, ln_g, ln_b, seq, *, tm=TM_CONF):
    t = proj.shape[0]
    hb = tm // CONF_HALO
    halo_row = lambda i: jnp.maximum(i * hb - 1, 0)
    const = lambda i: (0, 0)
    return pl.pallas_call(
        functools.partial(_conformer_kernel, tiles_per_seq=seq // tm),
        grid=(t // tm,),
        in_specs=[pl.BlockSpec((tm, CONF_WIDTH), lambda i: (i, 5)),
                  pl.BlockSpec((tm, CONF_WIDTH), lambda i: (i, 6)),
                  pl.BlockSpec((CONF_HALO, CONF_WIDTH), lambda i: (halo_row(i), 5)),
                  pl.BlockSpec((CONF_HALO, CONF_WIDTH), lambda i: (halo_row(i), 6)),
                  pl.BlockSpec((CONF_CONV_WIDTH, CONF_WIDTH), const),
                  pl.BlockSpec((1, CONF_WIDTH), const),
                  pl.BlockSpec((1, CONF_WIDTH), const),
                  pl.BlockSpec((1, CONF_WIDTH), const)],
        out_specs=pl.BlockSpec((tm, CONF_WIDTH), lambda i: (i, 0)),
        out_shape=jax.ShapeDtypeStruct((t, CONF_WIDTH), BF16),
        scratch_shapes=[pltpu.VMEM((tm + CONF_HALO, CONF_WIDTH), F32)],
        compiler_params=_cparams("parallel"),
    )(proj, proj, proj, proj, dw_w, dw_b.reshape(1, -1), ln_g.reshape(1, -1), ln_b.reshape(1, -1))


def _shortconv_kernel(b_ref, c_ref, h_ref, ch_ref, hh_ref, w_ref, o_ref, ext_ref, *, tiles_per_seq):
    first = (pl.program_id(0) % tiles_per_seq) == 0
    conv = _dwconv(ext_ref, c_ref[...] * h_ref[...], ch_ref[...] * hh_ref[...], first, w_ref, SC_CONV_WIDTH)
    o_ref[...] = (b_ref[...] * conv).astype(o_ref.dtype)


def shortconv_mixer(proj, conv_w, seq, *, tm=TM_SC):
    t = proj.shape[0]
    hb = tm // SUBLANES
    halo_row = lambda i: jnp.maximum(i * hb - 1, 0)
    return pl.pallas_call(
        functools.partial(_shortconv_kernel, tiles_per_seq=seq // tm),
        grid=(t // tm,),
        in_specs=[pl.BlockSpec((tm, SC_WIDTH), lambda i: (i, 0)),
                  pl.BlockSpec((tm, SC_WIDTH), lambda i: (i, 1)),
                  pl.BlockSpec((tm, SC_WIDTH), lambda i: (i, 2)),
                  pl.BlockSpec((SUBLANES, SC_WIDTH), lambda i: (halo_row(i), 1)),
                  pl.BlockSpec((SUBLANES, SC_WIDTH), lambda i: (halo_row(i), 2)),
                  pl.BlockSpec((SC_CONV_WIDTH, SC_WIDTH), lambda i: (0, 0))],
        out_specs=pl.BlockSpec((tm, SC_WIDTH), lambda i: (i, 0)),
        out_shape=jax.ShapeDtypeStruct((t, SC_WIDTH), BF16),
        scratch_shapes=[pltpu.VMEM((tm + SUBLANES, SC_WIDTH), F32)],
        compiler_params=_cparams("parallel"),
    )(proj, proj, proj, proj, proj, conv_w)


def _rope_table_kernel(pos_ref, inv_ref, cos_ref, sin_ref):
    ang = pos_ref[...].astype(F32) * inv_ref[...]
    sign = jnp.where(lax.broadcasted_iota(jnp.int32, ang.shape, 1) < NSA_HEAD_DIM // 2, -1.0, 1.0)
    cos_ref[...] = jnp.cos(ang)
    sin_ref[...] = jnp.sin(ang) * sign


def rope_tables(positions, *, tm=TM_ROPE):
    t = positions.size
    half = NSA_HEAD_DIM // 2
    inv = ROPE_THETA ** (-jnp.arange(half, dtype=F32) / half)
    inv = jnp.concatenate([inv, inv]).reshape(1, NSA_HEAD_DIM)
    tab = jax.ShapeDtypeStruct((t, NSA_HEAD_DIM), F32)
    return pl.pallas_call(
        _rope_table_kernel,
        grid=(t // tm,),
        in_specs=[pl.BlockSpec((tm, 1), lambda i: (i, 0)), pl.BlockSpec((1, NSA_HEAD_DIM), lambda i: (0, 0))],
        out_specs=[pl.BlockSpec((tm, NSA_HEAD_DIM), lambda i: (i, 0))] * 2,
        out_shape=[tab, tab],
        compiler_params=_cparams("parallel"),
    )(positions.reshape(t, 1), inv)


def _rope(x, cos, sin_signed):
    return x * cos + pltpu.roll(x, NSA_HEAD_DIM // 2, axis=1) * sin_signed


def _kv_prep_kernel(ks_ref, vs_ref, kw_ref, vw_ref, cos_ref, sin_ref, kso_ref, vso_ref, kwo_ref, vwo_ref):
    cos, sin = cos_ref[...], sin_ref[...]
    for g in range(NSA_KV_HEADS):
        sl = slice(g * NSA_HEAD_DIM, (g + 1) * NSA_HEAD_DIM)
        kso_ref[:, sl] = _rope(ks_ref[:, sl], cos, sin).astype(BF16)
        kwo_ref[:, sl] = _rope(kw_ref[:, sl], cos, sin).astype(BF16)
    vso_ref[...] = vs_ref[...].astype(BF16)
    vwo_ref[...] = vw_ref[...].astype(BF16)


def kv_prep(proj, cos, sin, *, tm=512):
    t = proj.shape[0]
    kv = jax.ShapeDtypeStruct((t, NSA_KV), BF16)
    return pl.pallas_call(
        _kv_prep_kernel,
        grid=(t // tm,),
        in_specs=[pl.BlockSpec((tm, NSA_KV), lambda i: (i, 12)),
                  pl.BlockSpec((tm, NSA_KV), lambda i: (i, 13)),
                  pl.BlockSpec((tm, NSA_KV), lambda i: (i, 14)),
                  pl.BlockSpec((tm, NSA_KV), lambda i: (i, 15)),
                  pl.BlockSpec((tm, NSA_HEAD_DIM), lambda i: (i, 0)),
                  pl.BlockSpec((tm, NSA_HEAD_DIM), lambda i: (i, 0))],
        out_specs=[pl.BlockSpec((tm, NSA_KV), lambda i: (i, 0))] * 4,
        out_shape=[kv] * 4,
        compiler_params=_cparams("parallel"),
    )(proj, proj, proj, proj, cos, sin)


def _compress_kernel(kt_ref, vt_ref, cos_ref, sin_ref, pe_ref, w1_ref, w2_ref, kc_ref, vc_ref):
    nb = kc_ref.shape[0]
    half = CMP_BLOCK // 2

    def compress(tok_ref, which):
        acc_lo = jnp.zeros((nb, NSA_HEAD_DIM), F32)
        acc_hi = jnp.zeros((nb, NSA_HEAD_DIM), F32)
        for l in range(half):
            r = tok_ref[pl.ds(l, nb, stride=CMP_STRIDE), :]
            lo = (r + pe_ref[which, l:l + 1, :]).astype(BF16)
            hi = (r + pe_ref[which, half + l:half + l + 1, :]).astype(BF16)
            acc_lo = acc_lo + _dot(lo, w1_ref[which, l * NSA_HEAD_DIM:(l + 1) * NSA_HEAD_DIM, :])
            acc_hi = acc_hi + _dot(hi, w1_ref[which, (half + l) * NSA_HEAD_DIM:(half + l + 1) * NSA_HEAD_DIM, :])
        pre = acc_lo + pltpu.roll(acc_hi, nb - 1, axis=0)
        return _dot(_silu(pre).astype(BF16), w2_ref[which])

    cos_e = pltpu.roll(cos_ref[pl.ds(CMP_STRIDE - 1, nb, stride=CMP_STRIDE), :], nb - 1, axis=0)
    sin_e = pltpu.roll(sin_ref[pl.ds(CMP_STRIDE - 1, nb, stride=CMP_STRIDE), :], nb - 1, axis=0)
    kc_ref[...] = _rope(compress(kt_ref, 0), cos_e, sin_e).astype(BF16)
    vc_ref[...] = compress(vt_ref, 1).astype(BF16)


def compress_kv(proj, cos, sin, pe, w1, w2, bsz, seq):
    nb = seq // CMP_STRIDE
    out = jax.ShapeDtypeStruct((bsz, NSA_KV_HEADS, nb, NSA_HEAD_DIM), BF16)
    full = lambda *shape: pl.BlockSpec(shape, lambda b, g: (0,) * len(shape))
    return pl.pallas_call(
        _compress_kernel,
        grid=(bsz, NSA_KV_HEADS),
        in_specs=[pl.BlockSpec((seq, NSA_HEAD_DIM), lambda b, g: (b, 40 + g)),
                  pl.BlockSpec((seq, NSA_HEAD_DIM), lambda b, g: (b, 44 + g)),
                  pl.BlockSpec((seq, NSA_HEAD_DIM), lambda b, g: (b, 0)),
                  pl.BlockSpec((seq, NSA_HEAD_DIM), lambda b, g: (b, 0)),
                  full(2, CMP_BLOCK, NSA_HEAD_DIM),
                  full(2, CMP_BLOCK * NSA_HEAD_DIM, NSA_HEAD_DIM),
                  full(2, NSA_HEAD_DIM, NSA_HEAD_DIM)],
        out_specs=[pl.BlockSpec((None, None, nb, NSA_HEAD_DIM), lambda b, g: (b, g, 0, 0))] * 2,
        out_shape=[out, out],
        compiler_params=_cparams("parallel", "parallel"),
    )(proj, proj, cos, sin, pe, w1, w2)


def _flash(qs, k_ref, v_ref, lo, hi, mask_fn):
    rows = qs.shape[0]

    def body(kt, carry):
        m, l, acc = carry
        k0 = pl.multiple_of(kt * TK, TK)
        s = _dot_nt(qs, k_ref[pl.ds(k0, TK), :])
        mask = mask_fn(k0)
        sm = jnp.where(mask, s, NEG_BIG)
        m_new = jnp.maximum(m, jnp.max(sm, axis=-1, keepdims=True))
        p = jnp.where(mask, jnp.exp(sm - m_new), 0.0)
        alpha = jnp.exp(m - m_new)
        l = alpha * l + jnp.sum(p, axis=-1, keepdims=True)
        acc = alpha * acc + _dot(p.astype(BF16), v_ref[pl.ds(k0, TK), :])
        return m_new, l, acc

    init = (jnp.full((rows, 1), NEG_BIG, F32), jnp.zeros((rows, 1), F32), jnp.zeros((rows, NSA_HEAD_DIM), F32))
    _, l, acc = lax.fori_loop(lo, hi, body, init)
    return acc * (1.0 / jnp.maximum(l, 1e-30))


def _nsa_kernel(q_ref, cos_ref, sin_ref, gate_ref, kc_ref, vc_ref, ks_ref, vs_ref, kw_ref, vw_ref,
                o_ref, qs_ref):
    G = NSA_GROUP
    q0 = pl.program_id(2) * TQ
    scale = NSA_HEAD_DIM ** -0.5
    cos, sin = cos_ref[...], sin_ref[...]
    for e in range(G):
        xq = _rope(q_ref[:, e * NSA_HEAD_DIM:(e + 1) * NSA_HEAD_DIM], cos, sin)
        qs_ref[e * TQ:(e + 1) * TQ, :] = (xq * scale).astype(BF16)
    qs = qs_ref[...]
    t_col = q0 + lax.broadcasted_iota(jnp.int32, (TQ, 1), 0)
    t_all = q0 + lax.broadcasted_iota(jnp.int32, (G * TQ, 1), 0) % TQ
    n_cmp = kc_ref.shape[0]
    n_slc = ks_ref.shape[0] // SLC_BLOCK

    ends = lax.broadcasted_iota(jnp.int32, (1, n_cmp), 1) * CMP_STRIDE + (CMP_BLOCK - 1)
    cmask = ends <= t_all
    s = jnp.where(cmask, _dot_nt(qs, kc_ref[...]), NEG_BIG)
    m = jnp.max(s, axis=-1, keepdims=True)
    p = jnp.where(cmask, jnp.exp(s - m), 0.0)
    l = jnp.sum(p, axis=-1, keepdims=True)
    p = p * (1.0 / jnp.maximum(l, 1e-30))
    o_cmp = _dot(p.astype(BF16), vc_ref[...])

    psum = p[0:TQ]
    for e in range(1, G):
        psum = psum + p[e * TQ:(e + 1) * TQ]
    n_i = lax.broadcasted_iota(jnp.int32, (n_cmp, LANES), 0)
    j_i = lax.broadcasted_iota(jnp.int32, (n_cmp, LANES), 1)
    overlap = ((n_i * CMP_STRIDE < (j_i + 1) * SLC_BLOCK)
               & (n_i * CMP_STRIDE + CMP_BLOCK - 1 >= j_i * SLC_BLOCK)).astype(BF16)
    imp = _dot_f32_rhs01(psum, overlap)
    j = lax.broadcasted_iota(jnp.int32, (1, LANES), 1)
    cur = t_col // SLC_BLOCK
    forced = (j == 0) | ((j <= cur) & (j > cur - N_LOCAL))
    valid = j * SLC_BLOCK <= t_col
    imp = jnp.where(forced, FORCE, jnp.where(valid, imp, -FORCE))
    cnt = jnp.zeros((TQ, LANES), F32)
    for k in range(n_slc):
        ck = imp[:, k:k + 1]
        beats = (ck > imp) | ((ck == imp) & (j > k))
        cnt = cnt + jnp.where(beats, 1.0, 0.0)
    sel = jnp.where((cnt < N_SELECT) & (j < n_slc), 1.0, 0.0).astype(BF16)

    def slc_mask(k0):
        kpos = k0 + lax.broadcasted_iota(jnp.int32, (1, TK), 1)
        blk = k0 // SLC_BLOCK + lax.broadcasted_iota(jnp.int32, (LANES, TK), 1) // SLC_BLOCK
        expand = (blk == lax.broadcasted_iota(jnp.int32, (LANES, TK), 0)).astype(BF16)
        chosen = jnp.concatenate([_dot(sel, expand)] * G, axis=0)
        return (chosen > 0.5) & (kpos <= t_all)

    o_slc = _flash(qs, ks_ref, vs_ref, 0, (q0 + TQ + TK - 1) // TK, slc_mask)

    def win_mask(k0):
        kpos = k0 + lax.broadcasted_iota(jnp.int32, (1, TK), 1)
        return (kpos <= t_all) & (kpos > t_all - WINDOW)

    o_win = _flash(qs, kw_ref, vw_ref, jnp.maximum(q0 - WINDOW + 1, 0) // TK, (q0 + TQ + TK - 1) // TK, win_mask)

    gate = jax.nn.sigmoid(gate_ref[...])
    for e in range(G):
        rows = slice(e * TQ, (e + 1) * TQ)
        o = (gate[:, 3 * e:3 * e + 1] * o_cmp[rows] + gate[:, 3 * e + 1:3 * e + 2] * o_slc[rows]
             + gate[:, 3 * e + 2:3 * e + 3] * o_win[rows])
        o_ref[:, e * NSA_HEAD_DIM:(e + 1) * NSA_HEAD_DIM] = o.astype(o_ref.dtype)


def nsa_attention(proj, gates, cos, sin, kc, vc, ks, vs, kw, vw, bsz, seq):
    nq = seq // TQ
    qw = NSA_GROUP * NSA_HEAD_DIM
    row = lambda b, g, i: b * nq + i
    kvspec = pl.BlockSpec((seq, NSA_HEAD_DIM), lambda b, g, i: (b, g))
    cspec = pl.BlockSpec((None, None, seq // CMP_STRIDE, NSA_HEAD_DIM), lambda b, g, i: (b, g, 0, 0))
    return pl.pallas_call(
        _nsa_kernel,
        grid=(bsz, NSA_KV_HEADS, nq),
        in_specs=[pl.BlockSpec((TQ, qw), lambda b, g, i: (row(b, g, i), 6 + g)),
                  pl.BlockSpec((TQ, NSA_HEAD_DIM), lambda b, g, i: (row(b, g, i), 0)),
                  pl.BlockSpec((TQ, NSA_HEAD_DIM), lambda b, g, i: (row(b, g, i), 0)),
                  pl.BlockSpec((TQ, LANES), lambda b, g, i: (row(b, g, i), g)),
                  cspec, cspec, kvspec, kvspec, kvspec, kvspec],
        out_specs=pl.BlockSpec((TQ, qw), lambda b, g, i: (row(b, g, i), g)),
        out_shape=jax.ShapeDtypeStruct((bsz * seq, NSA_Q), BF16),
        scratch_shapes=[pltpu.VMEM((NSA_GROUP * TQ, NSA_HEAD_DIM), BF16)],
        compiler_params=_cparams("parallel", "parallel", "parallel"),
    )(proj, cos, sin, gates, kc, vc, ks, vs, kw, vw)


def _even_layer(h, gain, w_in, conv_w, conv_b, dt_bias, a_log, d_skip, ssm_norm,
                dw_w, dw_b, ln_g, ln_b, w_out, bsz, seq):
    xbc_end = SSM_INNER + SSM_INNER + 2 * SSM_GROUPS * SSM_STATE
    dt_end = xbc_end + SSM_HEADS
    w_main = jnp.concatenate([w_in[:, :xbc_end], w_in[:, dt_end:]], axis=1).astype(BF16)
    w_dt = jnp.pad(w_in[:, xbc_end:dt_end], ((0, 0), (0, LANES - SSM_HEADS))).astype(BF16)
    proj = norm_matmul(h, gain, w_main)
    dt_raw = norm_matmul(h, gain, w_dt)
    y = ssd_mixer(proj, dt_raw, conv_w, conv_b, dt_bias, a_log, d_skip, ssm_norm, bsz, seq)
    u = conformer_mixer(proj, dw_w, dw_b, ln_g, ln_b, seq)
    w_out = w_out.astype(BF16)
    return matmul_res([y, u], [w_out[:SSM_INNER], w_out[SSM_INNER:]], h)


def _odd_layer(h, gain, cos, sin, w_in, sc_w, pe, w1, w2, w_out, bsz, seq):
    main = 3 * SC_WIDTH + NSA_Q + 6 * NSA_KV
    per_g = 3 * NSA_GROUP
    wg = w_in[:, main:].reshape(D_MODEL, NSA_KV_HEADS, per_g)
    wg = jnp.pad(wg, ((0, 0), (0, 0), (0, LANES - per_g))).reshape(D_MODEL, NSA_KV_HEADS * LANES)
    proj = norm_matmul(h, gain, w_in[:, :main].astype(BF16))
    gates = norm_matmul(h, gain, wg.astype(BF16), tn=NSA_KV_HEADS * LANES)
    y_c = shortconv_mixer(proj, sc_w, seq)
    ks, vs, kw, vw = kv_prep(proj, cos, sin)
    kc, vc = compress_kv(proj, cos, sin, pe, w1.astype(BF16), w2.astype(BF16), bsz, seq)
    y_d = nsa_attention(proj, gates, cos, sin, kc, vc, ks, vs, kw, vw, bsz, seq)
    w_out = w_out.astype(BF16)
    return matmul_res([y_c, y_d], [w_out[:SC_WIDTH], w_out[SC_WIDTH:]], h)


def _ffn(h, gain, w_up, conv_w, conv_b, w_down, seq):
    u = norm_matmul(h, gain, w_up.astype(BF16))
    a = ffn_gate(u, conv_w, conv_b, seq)
    return matmul_res([a], [w_down.astype(BF16)], h)


def kernel(x, positions, mix_norm, ffn_norm, final_norm, ab_w_in, ssm_conv_w, ssm_conv_b, ssm_dt_bias, ssm_a_log, ssm_d, ssm_norm, conf_dw_w, conf_dw_b, conf_ln_g, conf_ln_b, ab_w_out, cd_w_in, sc_conv_w, nsa_cmp_pe, nsa_cmp_w1, nsa_cmp_w2, cd_w_out, ffn_w_up, ffn_conv_w, ffn_conv_b, ffn_w_down):
    bsz, seq, d = x.shape
    depth = mix_norm.shape[0]
    h = x.reshape(bsz * seq, d)
    cos, sin = rope_tables(positions)
    for layer in range(depth):
        i = layer // 2
        if layer % 2 == 0:
            h = _even_layer(h, mix_norm[layer], ab_w_in[i], ssm_conv_w[i], ssm_conv_b[i], ssm_dt_bias[i],
                            ssm_a_log[i], ssm_d[i], ssm_norm[i], conf_dw_w[i], conf_dw_b[i], conf_ln_g[i],
                            conf_ln_b[i], ab_w_out[i], bsz, seq)
        else:
            h = _odd_layer(h, mix_norm[layer], cos, sin, cd_w_in[i], sc_conv_w[i], nsa_cmp_pe[i],
                           nsa_cmp_w1[i], nsa_cmp_w2[i], cd_w_out[i], bsz, seq)
        h = _ffn(h, ffn_norm[layer], ffn_w_up[layer], ffn_conv_w[layer], ffn_conv_b[layer],
                 ffn_w_down[layer], seq)
    return rmsnorm_rows(h, final_norm).reshape(bsz, seq, d)
```

```python
import functools

import jax
import jax.numpy as jnp
from jax import lax
from jax.experimental import pallas as pl
from jax.experimental.pallas import tpu as pltpu

F32 = jnp.float32
BF16 = jnp.bfloat16

D_MODEL = 2048
NORM_EPS = 1e-6
SSM_INNER = 2048
SSM_HEADDIM = 64
SSM_HEADS = 32
SSM_GROUPS = 4
SSM_STATE = 128
SSM_CONV_WIDTH = 4
SSM_CHUNK = 128
CONF_WIDTH = 1024
CONF_CONV_WIDTH = 31
SC_WIDTH = 1024
SC_CONV_WIDTH = 3
NSA_HEAD_DIM = 128
NSA_HEADS = 16
NSA_KV_HEADS = 4
NSA_GROUP = 4
CMP_BLOCK = 32
CMP_STRIDE = 16
SLC_BLOCK = 64
N_SELECT = 16
N_LOCAL = 2
WINDOW = 512
ROPE_THETA = 10000.0
NSA_Q = 2048
NSA_KV = 512
D_FF = 5632
FFN_CONV_WIDTH = 3
NEG_BIG = -1e30
FORCE = 1e9

V7X_VMEM_BYTES = 64 * 1024 * 1024
VMEM_LIMIT = V7X_VMEM_BYTES - 8 * 1024 * 1024
LANES = 128
SUBLANES = 8

TM_PROJ = 1024
TN_PROJ = 1024
TM_OUT = 512
TN_OUT = 1024
TS_FFN = 512
TC_FFN = 512
TM_CONF = 256
CONF_HALO = 32
TM_SC = 512
TM_ROPE = 1024
TQ = 128
TK = 256


def _cparams(*sem):
    return pltpu.CompilerParams(dimension_semantics=sem, vmem_limit_bytes=VMEM_LIMIT)


def _silu(x):
    return x * jax.nn.sigmoid(x)


def _softplus(x):
    return jnp.maximum(x, 0.0) + jnp.log1p(jnp.exp(-jnp.abs(x)))


def _split3(x):
    hi = x.astype(BF16)
    r1 = x - hi.astype(F32)
    mid = r1.astype(BF16)
    lo = (r1 - mid.astype(F32)).astype(BF16)
    return hi, mid, lo


def _dot(a, b):
    return jnp.dot(a, b, preferred_element_type=F32)


def _dot_nt(a, b):
    return lax.dot_general(a, b, (((1,), (1,)), ((), ())), preferred_element_type=F32)


def _dot_f32_rhs01(x, e01):
    hi, mid, lo = _split3(x)
    return _dot(hi, e01) + _dot(mid, e01) + _dot(lo, e01)


def _dot_f32_lhs01(e01, x):
    hi, mid, lo = _split3(x)
    return _dot(e01, hi) + _dot(e01, mid) + _dot(e01, lo)


def _dwconv(ext_ref, x, halo, first, w_ref, width):
    rows = x.shape[0]
    hrows = halo.shape[0]
    ext_ref[0:hrows, :] = jnp.where(first, 0.0, halo)
    ext_ref[hrows:hrows + rows, :] = x
    acc = None
    for k in range(width):
        off = hrows - (width - 1) + k
        term = w_ref[k:k + 1, :] * ext_ref[off:off + rows, :]
        acc = term if acc is None else acc + term
    return acc


def _norm_matmul_kernel(h_ref, g_ref, w_ref, o_ref, xn_ref):
    @pl.when(pl.program_id(1) == 0)
    def _():
        x = h_ref[...]
        ms = jnp.mean(x * x, axis=-1, keepdims=True)
        xn_ref[...] = (x * lax.rsqrt(ms + NORM_EPS) * g_ref[...]).astype(BF16)

    o_ref[...] = _dot(xn_ref[...], w_ref[...]).astype(o_ref.dtype)


def norm_matmul(h, gain, w, *, tm=TM_PROJ, tn=TN_PROJ):
    t, k = h.shape
    n = w.shape[1]
    tn = min(tn, n)
    return pl.pallas_call(
        _norm_matmul_kernel,
        grid=(t // tm, n // tn),
        in_specs=[pl.BlockSpec((tm, k), lambda i, j: (i, 0)),
                  pl.BlockSpec((1, k), lambda i, j: (0, 0)),
                  pl.BlockSpec((k, tn), lambda i, j: (0, j))],
        out_specs=pl.BlockSpec((tm, tn), lambda i, j: (i, j)),
        out_shape=jax.ShapeDtypeStruct((t, n), F32),
        scratch_shapes=[pltpu.VMEM((tm, k), BF16)],
        compiler_params=_cparams("parallel", "arbitrary"),
    )(h, gain.reshape(1, k), w)


def _matmul_res_kernel(*refs, n_in):
    a_refs, w_refs = refs[:n_in], refs[n_in:2 * n_in]
    r_ref, o_ref = refs[2 * n_in], refs[2 * n_in + 1]
    acc = _dot(a_refs[0][...], w_refs[0][...])
    for a_ref, w_ref in zip(a_refs[1:], w_refs[1:]):
        acc = acc + _dot(a_ref[...], w_ref[...])
    o_ref[...] = r_ref[...] + acc


def matmul_res(a_list, w_list, res, *, tm=TM_OUT, tn=TN_OUT):
    t, n = res.shape
    n_in = len(a_list)
    in_specs = ([pl.BlockSpec((tm, a.shape[1]), lambda i, j: (i, 0)) for a in a_list]
                + [pl.BlockSpec((w.shape[0], tn), lambda i, j: (0, j)) for w in w_list]
                + [pl.BlockSpec((tm, tn), lambda i, j: (i, j))])
    return pl.pallas_call(
        functools.partial(_matmul_res_kernel, n_in=n_in),
        grid=(t // tm, n // tn),
        in_specs=in_specs,
        out_specs=pl.BlockSpec((tm, tn), lambda i, j: (i, j)),
        out_shape=jax.ShapeDtypeStruct((t, n), F32),
        compiler_params=_cparams("parallel", "parallel"),
    )(*a_list, *w_list, res)


def _rmsnorm_kernel(h_ref, g_ref, o_ref):
    x = h_ref[...]
    ms = jnp.mean(x * x, axis=-1, keepdims=True)
    o_ref[...] = x * lax.rsqrt(ms + NORM_EPS) * g_ref[...]


def rmsnorm_rows(h, gain, *, tm=512):
    t, k = h.shape
    return pl.pallas_call(
        _rmsnorm_kernel,
        grid=(t // tm,),
        in_specs=[pl.BlockSpec((tm, k), lambda i: (i, 0)), pl.BlockSpec((1, k), lambda i: (0, 0))],
        out_specs=pl.BlockSpec((tm, k), lambda i: (i, 0)),
        out_shape=jax.ShapeDtypeStruct((t, k), F32),
        compiler_params=_cparams("parallel"),
    )(h, gain.reshape(1, k))


def _ffn_gate_kernel(g_ref, gh_ref, v_ref, vh_ref, wg_ref, wv_ref, bg_ref, bv_ref, o_ref,
                     extg_ref, extv_ref, *, tiles_per_seq):
    first = (pl.program_id(0) % tiles_per_seq) == 0
    gc = _dwconv(extg_ref, g_ref[...], gh_ref[...], first, wg_ref, FFN_CONV_WIDTH) + bg_ref[...]
    vc = _dwconv(extv_ref, v_ref[...], vh_ref[...], first, wv_ref, FFN_CONV_WIDTH) + bv_ref[...]
    o_ref[...] = (_silu(gc) * vc).astype(o_ref.dtype)


def ffn_gate(u, conv_w, conv_b, seq, *, ts=TS_FFN, tc=TC_FFN):
    t = u.shape[0]
    nf = D_FF // tc
    hb = ts // SUBLANES
    halo = lambda i, j: (jnp.maximum(i * hb - 1, 0), j)
    halo_v = lambda i, j: (jnp.maximum(i * hb - 1, 0), j + nf)
    return pl.pallas_call(
        functools.partial(_ffn_gate_kernel, tiles_per_seq=seq // ts),
        grid=(t // ts, nf),
        in_specs=[pl.BlockSpec((ts, tc), lambda i, j: (i, j)),
                  pl.BlockSpec((SUBLANES, tc), halo),
                  pl.BlockSpec((ts, tc), lambda i, j: (i, j + nf)),
                  pl.BlockSpec((SUBLANES, tc), halo_v),
                  pl.BlockSpec((FFN_CONV_WIDTH, tc), lambda i, j: (0, j)),
                  pl.BlockSpec((FFN_CONV_WIDTH, tc), lambda i, j: (0, j + nf)),
                  pl.BlockSpec((1, tc), lambda i, j: (0, j)),
                  pl.BlockSpec((1, tc), lambda i, j: (0, j + nf))],
        out_specs=pl.BlockSpec((ts, tc), lambda i, j: (i, j)),
        out_shape=jax.ShapeDtypeStruct((t, D_FF), BF16),
        scratch_shapes=[pltpu.VMEM((ts + SUBLANES, tc), F32), pltpu.VMEM((ts + SUBLANES, tc), F32)],
        compiler_params=_cparams("parallel", "parallel"),
    )(u, u, u, u, conv_w, conv_w, conv_b.reshape(1, -1), conv_b.reshape(1, -1))


def _ssd_kernel(z_ref, xs_ref, xsh_ref, bc_ref, bch_ref, dt_ref, cwx_ref, cbx_ref, cwb_ref, cbb_ref,
                dtb_ref, alog_ref, dsk_ref, gn_ref, y_ref, extx_ref, extb_ref, state_ref):
    L = SSM_CHUNK
    GW = SSM_INNER // SSM_GROUPS
    first = pl.program_id(1) == 0

    @pl.when(first)
    def _():
        state_ref[...] = jnp.zeros_like(state_ref)

    xs = _silu(_dwconv(extx_ref, xs_ref[...], xsh_ref[...], first, cwx_ref, SSM_CONV_WIDTH) + cbx_ref[...])
    bc = _silu(_dwconv(extb_ref, bc_ref[...], bch_ref[...], first, cwb_ref, SSM_CONV_WIDTH) + cbb_ref[...])

    dt = _softplus(dt_ref[...] + dtb_ref[...])
    a = -jnp.exp(alog_ref[...])
    la = dt * a
    row = lax.broadcasted_iota(jnp.int32, (L, L), 0)
    col = lax.broadcasted_iota(jnp.int32, (L, L), 1)
    causal = col <= row
    cum = _dot_f32_lhs01(causal.astype(BF16), la)
    cum_t = cum.T
    clast = cum[L - 1:L, :]

    e_head = (lax.broadcasted_iota(jnp.int32, (LANES, SSM_INNER), 1) // SSM_HEADDIM
              == lax.broadcasted_iota(jnp.int32, (LANES, SSM_INNER), 0)).astype(BF16)
    x = xs * _dot_f32_rhs01(dt, e_head)
    ecum_x = _dot_f32_rhs01(jnp.exp(cum), e_head)
    dte_x = _dot_f32_rhs01(jnp.exp(clast - cum), e_head)
    cdec_x = _dot_f32_rhs01(jnp.broadcast_to(jnp.exp(clast), (SUBLANES, LANES)), e_head)[0:1, :]
    xb = x.astype(BF16)
    xdte = (x * dte_x).astype(BF16)
    lo_half = lax.broadcasted_iota(jnp.int32, (L, LANES), 1) < SSM_HEADDIM

    for g in range(SSM_GROUPS):
        bg = bc[:, g * SSM_STATE:(g + 1) * SSM_STATE]
        cg = bc[:, (SSM_GROUPS + g) * SSM_STATE:(SSM_GROUPS + g + 1) * SSM_STATE].astype(BF16)
        cb = _dot_nt(cg, bg.astype(BF16))
        hg = state_ref[g]
        y_off = _dot(cg, hg.astype(BF16)) * ecum_x[:, g * GW:(g + 1) * GW]
        pieces = []
        for pr in range(GW // LANES):
            h0 = g * (GW // SSM_HEADDIM) + 2 * pr
            xp = xb[:, h0 * SSM_HEADDIM:h0 * SSM_HEADDIM + LANES]
            ypair = None
            for s in range(2):
                h = h0 + s
                diff = cum[:, h:h + 1] - cum_t[h:h + 1, :]
                decay = jnp.exp(jnp.where(causal, diff, -jnp.inf))
                m = (cb * decay).astype(BF16)
                keep = lo_half if s == 0 else jnp.logical_not(lo_half)
                yh = _dot(m, jnp.where(keep, xp, jnp.zeros_like(xp)))
                ypair = yh if ypair is None else ypair + yh
            pieces.append(ypair)
        y_diag = jnp.concatenate(pieces, axis=-1)
        st = _dot(bg.T.astype(BF16), xdte[:, g * GW:(g + 1) * GW])
        state_ref[g] = hg * cdec_x[:, g * GW:(g + 1) * GW] + st

        yg = y_diag + y_off + dsk_ref[:, g * GW:(g + 1) * GW] * xs[:, g * GW:(g + 1) * GW]
        yg = yg * _silu(z_ref[:, g * GW:(g + 1) * GW])
        ms = jnp.mean(yg * yg, axis=-1, keepdims=True)
        yg = yg * lax.rsqrt(ms + NORM_EPS) * gn_ref[:, g * GW:(g + 1) * GW]
        y_ref[:, g * GW:(g + 1) * GW] = yg.astype(y_ref.dtype)


def ssd_mixer(proj, dt_raw, conv_w, conv_b, dt_bias, a_log, d_skip, ssm_norm, bsz, seq):
    L = SSM_CHUNK
    nc = seq // L
    hb = L // SUBLANES
    pad = lambda v: jnp.pad(v.reshape(1, -1), ((0, 0), (0, LANES - v.shape[-1])))
    rowblk = lambda b, c: b * nc + c
    halo_row = lambda b, c: jnp.maximum((b * nc + c) * hb - 1, 0)
    const = lambda b, c: (0, 0)
    bcw = 2 * SSM_GROUPS * SSM_STATE
    return pl.pallas_call(
        _ssd_kernel,
        grid=(bsz, nc),
        in_specs=[pl.BlockSpec((L, SSM_INNER), lambda b, c: (rowblk(b, c), 0)),
                  pl.BlockSpec((L, SSM_INNER), lambda b, c: (rowblk(b, c), 1)),
                  pl.BlockSpec((SUBLANES, SSM_INNER), lambda b, c: (halo_row(b, c), 1)),
                  pl.BlockSpec((L, bcw), lambda b, c: (rowblk(b, c), 4)),
                  pl.BlockSpec((SUBLANES, bcw), lambda b, c: (halo_row(b, c), 4)),
                  pl.BlockSpec((L, LANES), lambda b, c: (rowblk(b, c), 0)),
                  pl.BlockSpec((SSM_CONV_WIDTH, SSM_INNER), const),
                  pl.BlockSpec((1, SSM_INNER), const),
                  pl.BlockSpec((SSM_CONV_WIDTH, bcw), const),
                  pl.BlockSpec((1, bcw), const),
                  pl.BlockSpec((1, LANES), const),
                  pl.BlockSpec((1, LANES), const),
                  pl.BlockSpec((1, SSM_INNER), const),
                  pl.BlockSpec((1, SSM_INNER), const)],
        out_specs=pl.BlockSpec((L, SSM_INNER), lambda b, c: (rowblk(b, c), 0)),
        out_shape=jax.ShapeDtypeStruct((bsz * seq, SSM_INNER), BF16),
        scratch_shapes=[pltpu.VMEM((L + SUBLANES, SSM_INNER), F32),
                        pltpu.VMEM((L + SUBLANES, bcw), F32),
                        pltpu.VMEM((SSM_GROUPS, SSM_STATE, SSM_INNER // SSM_GROUPS), F32)],
        compiler_params=_cparams("parallel", "arbitrary"),
    )(proj, proj, proj, proj, proj, dt_raw,
      conv_w[:, :SSM_INNER], conv_b[:SSM_INNER].reshape(1, -1),
      conv_w[:, SSM_INNER:], conv_b[SSM_INNER:].reshape(1, -1),
      pad(dt_bias), pad(a_log), jnp.repeat(d_skip, SSM_HEADDIM).reshape(1, -1), ssm_norm.reshape(1, -1))


def _conformer_kernel(ua_ref, ug_ref, uah_ref, ugh_ref, w_ref, b_ref, lg_ref, lb_ref, o_ref, ext_ref,
                      *, tiles_per_seq):
    first = (pl.program_id(0) % tiles_per_seq) == 0
    u = ua_ref[...] * jax.nn.sigmoid(ug_ref[...])
    uh = uah_ref[...] * jax.nn.sigmoid(ugh_ref[...])
    c = _dwconv(ext_ref, u, uh, first, w_ref, CONF_CONV_WIDTH) + b_ref[...]
    mu = jnp.mean(c, axis=-1, keepdims=True)
    d = c - mu
    var = jnp.mean(d * d, axis=-1, keepdims=True)
    y = d * lax.rsqrt(var + NORM_EPS) * lg_ref[...] + lb_ref[...]
    o_ref[...] = _silu(y).astype(o_ref.dtype)


def conformer_mixer(proj, dw_w, dw_b, ln_g, ln_b, seq, *, tm=TM_CONF):
    t = proj.shape[0]
    hb = tm // CONF_HALO
    halo_row = lambda i: jnp.maximum(i * hb - 1, 0)
    const = lambda i: (0, 0)
    return pl.pallas_call(
        functools.partial(_conformer_kernel, tiles_per_seq=seq // tm),
        grid=(t // tm,),
        in_specs=[pl.BlockSpec((tm, CONF_WIDTH), lambda i: (i, 5)),
                  pl.BlockSpec((tm, CONF_WIDTH), lambda i: (i, 6)),
                  pl.BlockSpec((CONF_HALO, CONF_WIDTH), lambda i: (halo_row(i), 5)),
                  pl.BlockSpec((CONF_HALO, CONF_WIDTH), lambda i: (halo_row(i), 6)),
                  pl.BlockSpec((CONF_CONV_WIDTH, CONF_WIDTH), const),
                  pl.BlockSpec((1, CONF_WIDTH), const),
                  pl.BlockSpec((1, CONF_WIDTH), const),
                  pl.BlockSpec((1, CONF_WIDTH), const)],
        out_specs=pl.BlockSpec((tm, CONF_WIDTH), lambda i: (i, 0)),
        out_shape=jax.ShapeDtypeStruct((t, CONF_WIDTH), BF16),
        scratch_shapes=[pltpu.VMEM((tm + CONF_HALO, CONF_WIDTH), F32)],
        compiler_params=_cparams("parallel"),
    )(proj, proj, proj, proj, dw_w, dw_b.reshape(1, -1), ln_g.reshape(1, -1), ln_b.reshape(1, -1))


def _shortconv_kernel(b_ref, c_ref, h_ref, ch_ref, hh_ref, w_ref, o_ref, ext_ref, *, tiles_per_seq):
    first = (pl.program_id(0) % tiles_per_seq) == 0
    conv = _dwconv(ext_ref, c_ref[...] * h_ref[...], ch_ref[...] * hh_ref[...], first, w_ref, SC_CONV_WIDTH)
    o_ref[...] = (b_ref[...] * conv).astype(o_ref.dtype)


def shortconv_mixer(proj, conv_w, seq, *, tm=TM_SC):
    t = proj.shape[0]
    hb = tm // SUBLANES
    halo_row = lambda i: jnp.maximum(i * hb - 1, 0)
    return pl.pallas_call(
        functools.partial(_shortconv_kernel, tiles_per_seq=seq // tm),
        grid=(t // tm,),
        in_specs=[pl.BlockSpec((tm, SC_WIDTH), lambda i: (i, 0)),
                  pl.BlockSpec((tm, SC_WIDTH), lambda i: (i, 1)),
                  pl.BlockSpec((tm, SC_WIDTH), lambda i: (i, 2)),
                  pl.BlockSpec((SUBLANES, SC_WIDTH), lambda i: (halo_row(i), 1)),
                  pl.BlockSpec((SUBLANES, SC_WIDTH), lambda i: (halo_row(i), 2)),
                  pl.BlockSpec((SC_CONV_WIDTH, SC_WIDTH), lambda i: (0, 0))],
        out_specs=pl.BlockSpec((tm, SC_WIDTH), lambda i: (i, 0)),
        out_shape=jax.ShapeDtypeStruct((t, SC_WIDTH), BF16),
        scratch_shapes=[pltpu.VMEM((tm + SUBLANES, SC_WIDTH), F32)],
        compiler_params=_cparams("parallel"),
    )(proj, proj, proj, proj, proj, conv_w)


def _rope_table_kernel(pos_ref, inv_ref, cos_ref, sin_ref):
    ang = pos_ref[...].astype(F32) * inv_ref[...]
    sign = jnp.where(lax.broadcasted_iota(jnp.int32, ang.shape, 1) < NSA_HEAD_DIM // 2, -1.0, 1.0)
    cos_ref[...] = jnp.cos(ang)
    sin_ref[...] = jnp.sin(ang) * sign


def rope_tables(positions, *, tm=TM_ROPE):
    t = positions.size
    half = NSA_HEAD_DIM // 2
    inv = ROPE_THETA ** (-jnp.arange(half, dtype=F32) / half)
    inv = jnp.concatenate([inv, inv]).reshape(1, NSA_HEAD_DIM)
    tab = jax.ShapeDtypeStruct((t, NSA_HEAD_DIM), F32)
    return pl.pallas_call(
        _rope_table_kernel,
        grid=(t // tm,),
        in_specs=[pl.BlockSpec((tm, 1), lambda i: (i, 0)), pl.BlockSpec((1, NSA_HEAD_DIM), lambda i: (0, 0))],
        out_specs=[pl.BlockSpec((tm, NSA_HEAD_DIM), lambda i: (i, 0))] * 2,
        out_shape=[tab, tab],
        compiler_params=_cparams("parallel"),
    )(positions.reshape(t, 1), inv)


def _rope(x, cos, sin_signed):
    return x * cos + pltpu.roll(x, NSA_HEAD_DIM // 2, axis=1) * sin_signed


def _kv_prep_kernel(ks_ref, vs_ref, kw_ref, vw_ref, cos_ref, sin_ref, kso_ref, vso_ref, kwo_ref, vwo_ref,
                    *, tiles_per_seq):
    D = NSA_HEAD_DIM
    tm = ks_ref.shape[0]
    cos, sin = cos_ref[...], sin_ref[...]
    row = (pl.program_id(0) % tiles_per_seq) * tm + lax.broadcasted_iota(jnp.int32, (tm, D), 0)
    lane = lax.broadcasted_iota(jnp.int32, (tm, D), 1)
    onehot = jnp.where(row // SLC_BLOCK == lane, 1.0, 0.0).astype(BF16)
    ones_col = jnp.where(lane == 0, 1.0, 0.0).astype(BF16)
    for g in range(NSA_KV_HEADS):
        sl = slice(g * D, (g + 1) * D)
        kso_ref[:, 2 * g * D:(2 * g + 1) * D] = _rope(ks_ref[:, sl], cos, sin).astype(BF16)
        kso_ref[:, (2 * g + 1) * D:(2 * g + 2) * D] = onehot
        kwo_ref[:, sl] = _rope(kw_ref[:, sl], cos, sin).astype(BF16)
        vso_ref[:, 2 * g * D:(2 * g + 1) * D] = vs_ref[:, sl].astype(BF16)
        vso_ref[:, (2 * g + 1) * D:(2 * g + 2) * D] = ones_col
        vwo_ref[:, 2 * g * D:(2 * g + 1) * D] = vw_ref[:, sl].astype(BF16)
        vwo_ref[:, (2 * g + 1) * D:(2 * g + 2) * D] = ones_col


def kv_prep(proj, cos, sin, seq, *, tm=512):
    t = proj.shape[0]
    kv = jax.ShapeDtypeStruct((t, NSA_KV), BF16)
    kv_aug = jax.ShapeDtypeStruct((t, 2 * NSA_KV), BF16)
    spec = pl.BlockSpec((tm, NSA_KV), lambda i: (i, 0))
    spec_aug = pl.BlockSpec((tm, 2 * NSA_KV), lambda i: (i, 0))
    return pl.pallas_call(
        functools.partial(_kv_prep_kernel, tiles_per_seq=seq // tm),
        grid=(t // tm,),
        in_specs=[pl.BlockSpec((tm, NSA_KV), lambda i: (i, 12)),
                  pl.BlockSpec((tm, NSA_KV), lambda i: (i, 13)),
                  pl.BlockSpec((tm, NSA_KV), lambda i: (i, 14)),
                  pl.BlockSpec((tm, NSA_KV), lambda i: (i, 15)),
                  pl.BlockSpec((tm, NSA_HEAD_DIM), lambda i: (i, 0)),
                  pl.BlockSpec((tm, NSA_HEAD_DIM), lambda i: (i, 0))],
        out_specs=[spec_aug, spec_aug, spec, spec_aug],
        out_shape=[kv_aug, kv_aug, kv, kv_aug],
        compiler_params=_cparams("parallel"),
    )(proj, proj, proj, proj, cos, sin)


def _compress_kernel(kt_ref, vt_ref, cos_ref, sin_ref, pe_ref, w1_ref, w2_ref, kc_ref, vc_ref):
    nb = kc_ref.shape[0]
    half = CMP_BLOCK // 2

    def compress(tok_ref, which):
        acc_lo = jnp.zeros((nb, NSA_HEAD_DIM), F32)
        acc_hi = jnp.zeros((nb, NSA_HEAD_DIM), F32)
        for l in range(half):
            r = tok_ref[pl.ds(l, nb, stride=CMP_STRIDE), :]
            lo = (r + pe_ref[which, l:l + 1, :]).astype(BF16)
            hi = (r + pe_ref[which, half + l:half + l + 1, :]).astype(BF16)
            acc_lo = acc_lo + _dot(lo, w1_ref[which, l * NSA_HEAD_DIM:(l + 1) * NSA_HEAD_DIM, :])
            acc_hi = acc_hi + _dot(hi, w1_ref[which, (half + l) * NSA_HEAD_DIM:(half + l + 1) * NSA_HEAD_DIM, :])
        pre = acc_lo + pltpu.roll(acc_hi, nb - 1, axis=0)
        return _dot(_silu(pre).astype(BF16), w2_ref[which])

    cos_e = pltpu.roll(cos_ref[pl.ds(CMP_STRIDE - 1, nb, stride=CMP_STRIDE), :], nb - 1, axis=0)
    sin_e = pltpu.roll(sin_ref[pl.ds(CMP_STRIDE - 1, nb, stride=CMP_STRIDE), :], nb - 1, axis=0)
    kc_ref[...] = _rope(compress(kt_ref, 0), cos_e, sin_e).astype(BF16)
    vc_ref[...] = compress(vt_ref, 1).astype(BF16)


def compress_kv(proj, cos, sin, pe, w1, w2, bsz, seq):
    nb = seq // CMP_STRIDE
    out = jax.ShapeDtypeStruct((bsz, NSA_KV_HEADS, nb, NSA_HEAD_DIM), BF16)
    full = lambda *shape: pl.BlockSpec(shape, lambda b, g: (0,) * len(shape))
    return pl.pallas_call(
        _compress_kernel,
        grid=(bsz, NSA_KV_HEADS),
        in_specs=[pl.BlockSpec((seq, NSA_HEAD_DIM), lambda b, g: (b, 40 + g)),
                  pl.BlockSpec((seq, NSA_HEAD_DIM), lambda b, g: (b, 44 + g)),
                  pl.BlockSpec((seq, NSA_HEAD_DIM), lambda b, g: (b, 0)),
                  pl.BlockSpec((seq, NSA_HEAD_DIM), lambda b, g: (b, 0)),
                  full(2, CMP_BLOCK, NSA_HEAD_DIM),
                  full(2, CMP_BLOCK * NSA_HEAD_DIM, NSA_HEAD_DIM),
                  full(2, NSA_HEAD_DIM, NSA_HEAD_DIM)],
        out_specs=[pl.BlockSpec((None, None, nb, NSA_HEAD_DIM), lambda b, g: (b, g, 0, 0))] * 2,
        out_shape=[out, out],
        compiler_params=_cparams("parallel", "parallel"),
    )(proj, proj, cos, sin, pe, w1, w2)


def _nsa_kernel(q_ref, cos_ref, sin_ref, gate_ref, kc_ref, vc_ref, ks_ref, vs_ref, kw_ref, vw_ref,
                o_ref, qa_ref, impt_ref):
    G = NSA_GROUP
    D = NSA_HEAD_DIM
    R = G * TQ
    q0 = pl.program_id(2) * TQ
    scale = D ** -0.5
    cos, sin = cos_ref[...], sin_ref[...]
    for e in range(G):
        xq = _rope(q_ref[:, e * D:(e + 1) * D], cos, sin)
        qa_ref[e * TQ:(e + 1) * TQ, 0:D] = (xq * scale).astype(BF16)
    qs = qa_ref[:, 0:D]
    t_col = q0 + lax.broadcasted_iota(jnp.int32, (TQ, 1), 0)
    n_cmp = kc_ref.shape[0]
    n_slc = ks_ref.shape[0] // SLC_BLOCK

    ends = lax.broadcasted_iota(jnp.int32, (1, n_cmp), 1) * CMP_STRIDE + (CMP_BLOCK - 1)
    cvalid = ends <= t_col
    cbias = jnp.where(cvalid, 0.0, NEG_BIG)
    c01 = jnp.where(cvalid, 1.0, 0.0)
    s = _dot_nt(qs, kc_ref[...]).reshape(G, TQ, n_cmp) + cbias[None]
    m = jnp.max(s, axis=-1, keepdims=True)
    p = jnp.exp(s - m) * c01[None]
    l = jnp.sum(p, axis=-1, keepdims=True)
    p = p * (1.0 / jnp.maximum(l, 1e-30))
    o_cmp = _dot(p.reshape(R, n_cmp).astype(BF16), vc_ref[...])

    psum = p[0]
    for e in range(1, G):
        psum = psum + p[e]
    n_i = lax.broadcasted_iota(jnp.int32, (n_cmp, LANES), 0)
    j_i = lax.broadcasted_iota(jnp.int32, (n_cmp, LANES), 1)
    overlap = ((n_i * CMP_STRIDE < (j_i + 1) * SLC_BLOCK)
               & (n_i * CMP_STRIDE + CMP_BLOCK - 1 >= j_i * SLC_BLOCK)).astype(BF16)
    imp = _dot_f32_rhs01(psum, overlap)
    j = lax.broadcasted_iota(jnp.int32, (1, LANES), 1)
    cur = t_col // SLC_BLOCK
    forced = (j == 0) | ((j <= cur) & (j > cur - N_LOCAL))
    valid = j * SLC_BLOCK <= t_col
    imp = jnp.where(forced, FORCE, jnp.where(valid, imp, -FORCE))
    impt_ref[...] = imp.T
    nrb = n_slc // SUBLANES
    vals = [impt_ref[rb * SUBLANES:(rb + 1) * SUBLANES, :] for rb in range(nrb)]
    cnt = [jnp.zeros((SUBLANES, TQ), F32) for _ in range(nrb)]
    sub = lax.broadcasted_iota(jnp.int32, (SUBLANES, TQ), 0)
    for k in range(n_slc):
        vk = jnp.broadcast_to(impt_ref[k:k + 1, :], (SUBLANES, TQ))
        for rb in range(nrb):
            if rb * SUBLANES > k:
                beats = vk >= vals[rb]
            elif rb * SUBLANES + SUBLANES - 1 < k:
                beats = vk > vals[rb]
            else:
                beats = (vk > vals[rb]) | ((vk == vals[rb]) & (sub > k - rb * SUBLANES))
            cnt[rb] = cnt[rb] + jnp.where(beats, 1.0, 0.0)
    sel_t = jnp.concatenate([jnp.where(c < N_SELECT, 1.0, 0.0) for c in cnt]
                            + [jnp.zeros((LANES - n_slc, TQ), F32)], axis=0)
    sel = sel_t.T
    sbias = jnp.where((sel > 0.5) & (j * SLC_BLOCK < q0), 0.0, NEG_BIG).astype(BF16)
    for e in range(G):
        qa_ref[e * TQ:(e + 1) * TQ, D:2 * D] = sbias

    lower = (lax.broadcasted_iota(jnp.int32, (TQ, TQ), 1) <= lax.broadcasted_iota(jnp.int32, (TQ, TQ), 0))
    s = _dot_nt(qs, ks_ref[pl.ds(q0, TQ), 0:D]).reshape(G, TQ, TQ) + jnp.where(lower, 0.0, NEG_BIG)[None]
    m = jnp.max(s, axis=-1, keepdims=True)
    p = jnp.exp(s - m).reshape(R, TQ)
    acc = _dot(p.astype(BF16), vs_ref[pl.ds(q0, TQ), :])
    m = m.reshape(R, 1)

    def slc_body(kt, carry):
        m, acc = carry
        k0 = pl.multiple_of(kt * TK, TK)
        s = _dot_nt(qa_ref[...], ks_ref[pl.ds(k0, TK), :])
        m_new = jnp.maximum(m, jnp.max(s, axis=-1, keepdims=True))
        p = jnp.exp(s - m_new)
        acc = jnp.exp(m - m_new) * acc + _dot(p.astype(BF16), vs_ref[pl.ds(k0, TK), :])
        return m_new, acc

    _, acc = lax.fori_loop(0, (q0 + TK - 1) // TK, slc_body, (m, acc))
    o_slc = acc[:, 0:D] * (1.0 / acc[:, D:D + 1])

    wk = WINDOW + TQ
    start = pl.multiple_of(jnp.maximum(q0 - WINDOW, 0), TQ)
    d = q0 - start
    rel = lax.broadcasted_iota(jnp.int32, (TQ, wk), 1) - lax.broadcasted_iota(jnp.int32, (TQ, wk), 0)
    wbias = jnp.where((rel <= d) & (rel > d - WINDOW), 0.0, NEG_BIG)
    s = _dot_nt(qs, kw_ref[pl.ds(start, wk), :]).reshape(G, TQ, wk) + wbias[None]
    m = jnp.max(s, axis=-1, keepdims=True)
    p = jnp.exp(s - m).reshape(R, wk)
    accw = _dot(p.astype(BF16), vw_ref[pl.ds(start, wk), :])
    o_win = accw[:, 0:D] * (1.0 / accw[:, D:D + 1])

    gate = jax.nn.sigmoid(gate_ref[...])
    for e in range(G):
        rows = slice(e * TQ, (e + 1) * TQ)
        o = (gate[:, 3 * e:3 * e + 1] * o_cmp[rows] + gate[:, 3 * e + 1:3 * e + 2] * o_slc[rows]
             + gate[:, 3 * e + 2:3 * e + 3] * o_win[rows])
        o_ref[:, e * D:(e + 1) * D] = o.astype(o_ref.dtype)


def nsa_attention(proj, gates, cos, sin, kc, vc, ks, vs, kw, vw, bsz, seq):
    nq = seq // TQ
    qw = NSA_GROUP * NSA_HEAD_DIM
    row = lambda b, g, i: b * nq + i
    kvspec = pl.BlockSpec((seq, NSA_HEAD_DIM), lambda b, g, i: (b, g))
    augspec = pl.BlockSpec((seq, 2 * NSA_HEAD_DIM), lambda b, g, i: (b, g))
    cspec = pl.BlockSpec((None, None, seq // CMP_STRIDE, NSA_HEAD_DIM), lambda b, g, i: (b, g, 0, 0))
    return pl.pallas_call(
        _nsa_kernel,
        grid=(bsz, NSA_KV_HEADS, nq),
        in_specs=[pl.BlockSpec((TQ, qw), lambda b, g, i: (row(b, g, i), 6 + g)),
                  pl.BlockSpec((TQ, NSA_HEAD_DIM), lambda b, g, i: (row(b, g, i), 0)),
                  pl.BlockSpec((TQ, NSA_HEAD_DIM), lambda b, g, i: (row(b, g, i), 0)),
                  pl.BlockSpec((TQ, LANES), lambda b, g, i: (row(b, g, i), g)),
                  cspec, cspec, augspec, augspec, kvspec, augspec],
        out_specs=pl.BlockSpec((TQ, qw), lambda b, g, i: (row(b, g, i), g)),
        out_shape=jax.ShapeDtypeStruct((bsz * seq, NSA_Q), BF16),
        scratch_shapes=[pltpu.VMEM((NSA_GROUP * TQ, 2 * NSA_HEAD_DIM), BF16),
                        pltpu.VMEM((LANES, TQ), F32)],
        compiler_params=_cparams("parallel", "parallel", "parallel"),
    )(proj, cos, sin, gates, kc, vc, ks, vs, kw, vw)


def _even_layer(h, gain, w_in, conv_w, conv_b, dt_bias, a_log, d_skip, ssm_norm,
                dw_w, dw_b, ln_g, ln_b, w_out, bsz, seq):
    xbc_end = SSM_INNER + SSM_INNER + 2 * SSM_GROUPS * SSM_STATE
    dt_end = xbc_end + SSM_HEADS
    w_main = jnp.concatenate([w_in[:, :xbc_end], w_in[:, dt_end:]], axis=1).astype(BF16)
    w_dt = jnp.pad(w_in[:, xbc_end:dt_end], ((0, 0), (0, LANES - SSM_HEADS))).astype(BF16)
    proj = norm_matmul(h, gain, w_main)
    dt_raw = norm_matmul(h, gain, w_dt)
    y = ssd_mixer(proj, dt_raw, conv_w, conv_b, dt_bias, a_log, d_skip, ssm_norm, bsz, seq)
    u = conformer_mixer(proj, dw_w, dw_b, ln_g, ln_b, seq)
    w_out = w_out.astype(BF16)
    return matmul_res([y, u], [w_out[:SSM_INNER], w_out[SSM_INNER:]], h)


def _odd_layer(h, gain, cos, sin, w_in, sc_w, pe, w1, w2, w_out, bsz, seq):
    main = 3 * SC_WIDTH + NSA_Q + 6 * NSA_KV
    per_g = 3 * NSA_GROUP
    wg = w_in[:, main:].reshape(D_MODEL, NSA_KV_HEADS, per_g)
    wg = jnp.pad(wg, ((0, 0), (0, 0), (0, LANES - per_g))).reshape(D_MODEL, NSA_KV_HEADS * LANES)
    proj = norm_matmul(h, gain, w_in[:, :main].astype(BF16))
    gates = norm_matmul(h, gain, wg.astype(BF16), tn=NSA_KV_HEADS * LANES)
    y_c = shortconv_mixer(proj, sc_w, seq)
    ks, vs, kw, vw = kv_prep(proj, cos, sin, seq)
    kc, vc = compress_kv(proj, cos, sin, pe, w1.astype(BF16), w2.astype(BF16), bsz, seq)
    y_d = nsa_attention(proj, gates, cos, sin, kc, vc, ks, vs, kw, vw, bsz, seq)
    w_out = w_out.astype(BF16)
    return matmul_res([y_c, y_d], [w_out[:SC_WIDTH], w_out[SC_WIDTH:]], h)


def _ffn(h, gain, w_up, conv_w, conv_b, w_down, seq):
    u = norm_matmul(h, gain, w_up.astype(BF16))
    a = ffn_gate(u, conv_w, conv_b, seq)
    return matmul_res([a], [w_down.astype(BF16)], h)


def kernel(x, positions, mix_norm, ffn_norm, final_norm, ab_w_in, ssm_conv_w, ssm_conv_b, ssm_dt_bias, ssm_a_log, ssm_d, ssm_norm, conf_dw_w, conf_dw_b, conf_ln_g, conf_ln_b, ab_w_out, cd_w_in, sc_conv_w, nsa_cmp_pe, nsa_cmp_w1, nsa_cmp_w2, cd_w_out, ffn_w_up, ffn_conv_w, ffn_conv_b, ffn_w_down):
    bsz, seq, d = x.shape
    depth = mix_norm.shape[0]
    h = x.reshape(bsz * seq, d)
    cos, sin = rope_tables(positions)
    for layer in range(depth):
        i = layer // 2
        if layer % 2 == 0:
            h = _even_layer(h, mix_norm[layer], ab_w_in[i], ssm_conv_w[i], ssm_conv_b[i], ssm_dt_bias[i],
                            ssm_a_log[i], ssm_d[i], ssm_norm[i], conf_dw_w[i], conf_dw_b[i], conf_ln_g[i],
                            conf_ln_b[i], ab_w_out[i], bsz, seq)
        else:
            h = _odd_layer(h, mix_norm[layer], cos, sin, cd_w_in[i], sc_conv_w[i], nsa_cmp_pe[i],
                           nsa_cmp_w1[i], nsa_cmp_w2[i], cd_w_out[i], bsz, seq)
        h = _ffn(h, ffn_norm[layer], ffn_w_up[layer], ffn_conv_w[layer], ffn_conv_b[layer],
                 ffn_w_down[layer], seq)
    return rmsnorm_rows(h, final_norm).reshape(bsz, seq, d)
```

```python
import functools

import jax
import jax.numpy as jnp
from jax import lax
from jax.experimental import pallas as pl
from jax.experimental.pallas import tpu as pltpu

F32 = jnp.float32
BF16 = jnp.bfloat16

D_MODEL = 2048
NORM_EPS = 1e-6
SSM_INNER = 2048
SSM_HEADDIM = 64
SSM_HEADS = 32
SSM_GROUPS = 4
SSM_STATE = 128
SSM_CONV_WIDTH = 4
SSM_CHUNK = 128
CONF_WIDTH = 1024
CONF_CONV_WIDTH = 31
SC_WIDTH = 1024
SC_CONV_WIDTH = 3
NSA_HEAD_DIM = 128
NSA_HEADS = 16
NSA_KV_HEADS = 4
NSA_GROUP = 4
CMP_BLOCK = 32
CMP_STRIDE = 16
SLC_BLOCK = 64
N_SELECT = 16
N_LOCAL = 2
WINDOW = 512
ROPE_THETA = 10000.0
NSA_Q = 2048
NSA_KV = 512
D_FF = 5632
FFN_CONV_WIDTH = 3
NEG_BIG = -1e30
FORCE = 1e9

V7X_VMEM_BYTES = 64 * 1024 * 1024
VMEM_LIMIT = V7X_VMEM_BYTES - 8 * 1024 * 1024
LANES = 128
SUBLANES = 8

TM_PROJ = 1024
TN_PROJ = 1024
TM_OUT = 512
TN_OUT = 1024
TN_FFN = 512
FFN_HALO = 16
FFN_CHUNKS = 2
TM_CONF = 256
CONF_HALO = 32
TM_SC = 512
TM_ROPE = 1024
TQ = 128
TK = 512


def _cparams(*sem):
    return pltpu.CompilerParams(dimension_semantics=sem, vmem_limit_bytes=VMEM_LIMIT)


def _silu(x):
    return x * jax.nn.sigmoid(x)


def _softplus(x):
    return jnp.maximum(x, 0.0) + jnp.log1p(jnp.exp(-jnp.abs(x)))


def _split3(x):
    hi = x.astype(BF16)
    r1 = x - hi.astype(F32)
    mid = r1.astype(BF16)
    lo = (r1 - mid.astype(F32)).astype(BF16)
    return hi, mid, lo


def _dot(a, b):
    return jnp.dot(a, b, preferred_element_type=F32)


def _dot_nt(a, b):
    return lax.dot_general(a, b, (((1,), (1,)), ((), ())), preferred_element_type=F32)


def _dot_f32_rhs01(x, e01):
    hi, mid, lo = _split3(x)
    return _dot(hi, e01) + _dot(mid, e01) + _dot(lo, e01)


def _dot_f32_lhs01(e01, x):
    hi, mid, lo = _split3(x)
    return _dot(e01, hi) + _dot(e01, mid) + _dot(e01, lo)


def _dwconv(ext_ref, x, halo, first, w_ref, width):
    rows = x.shape[0]
    hrows = halo.shape[0]
    ext_ref[0:hrows, :] = jnp.where(first, 0.0, halo)
    ext_ref[hrows:hrows + rows, :] = x
    acc = None
    for k in range(width):
        off = hrows - (width - 1) + k
        term = w_ref[k:k + 1, :] * ext_ref[off:off + rows, :]
        acc = term if acc is None else acc + term
    return acc


def _norm_matmul_kernel(h_ref, g_ref, w_ref, o_ref, xn_ref):
    @pl.when(pl.program_id(1) == 0)
    def _():
        x = h_ref[...]
        ms = jnp.mean(x * x, axis=-1, keepdims=True)
        xn_ref[...] = (x * lax.rsqrt(ms + NORM_EPS) * g_ref[...]).astype(BF16)

    o_ref[...] = _dot(xn_ref[...], w_ref[...]).astype(o_ref.dtype)


def norm_matmul(h, gain, w, *, tm=TM_PROJ, tn=TN_PROJ):
    t, k = h.shape
    n = w.shape[1]
    tn = min(tn, n)
    return pl.pallas_call(
        _norm_matmul_kernel,
        grid=(t // tm, n // tn),
        in_specs=[pl.BlockSpec((tm, k), lambda i, j: (i, 0)),
                  pl.BlockSpec((1, k), lambda i, j: (0, 0)),
                  pl.BlockSpec((k, tn), lambda i, j: (0, j))],
        out_specs=pl.BlockSpec((tm, tn), lambda i, j: (i, j)),
        out_shape=jax.ShapeDtypeStruct((t, n), F32),
        scratch_shapes=[pltpu.VMEM((tm, k), BF16)],
        compiler_params=_cparams("parallel", "arbitrary"),
    )(h, gain.reshape(1, k), w)


def _matmul_res_kernel(*refs, n_in):
    a_refs, w_refs = refs[:n_in], refs[n_in:2 * n_in]
    r_ref, o_ref = refs[2 * n_in], refs[2 * n_in + 1]
    acc = _dot(a_refs[0][...], w_refs[0][...])
    for a_ref, w_ref in zip(a_refs[1:], w_refs[1:]):
        acc = acc + _dot(a_ref[...], w_ref[...])
    o_ref[...] = r_ref[...] + acc


def matmul_res(a_list, w_list, res, *, tm=TM_OUT, tn=TN_OUT):
    t, n = res.shape
    n_in = len(a_list)
    in_specs = ([pl.BlockSpec((tm, a.shape[1]), lambda i, j: (i, 0)) for a in a_list]
                + [pl.BlockSpec((w.shape[0], tn), lambda i, j: (0, j)) for w in w_list]
                + [pl.BlockSpec((tm, tn), lambda i, j: (i, j))])
    return pl.pallas_call(
        functools.partial(_matmul_res_kernel, n_in=n_in),
        grid=(t // tm, n // tn),
        in_specs=in_specs,
        out_specs=pl.BlockSpec((tm, tn), lambda i, j: (i, j)),
        out_shape=jax.ShapeDtypeStruct((t, n), F32),
        compiler_params=_cparams("parallel", "parallel"),
    )(*a_list, *w_list, res)


def _rmsnorm_kernel(h_ref, g_ref, o_ref):
    x = h_ref[...]
    ms = jnp.mean(x * x, axis=-1, keepdims=True)
    o_ref[...] = x * lax.rsqrt(ms + NORM_EPS) * g_ref[...]


def rmsnorm_rows(h, gain, *, tm=512):
    t, k = h.shape
    return pl.pallas_call(
        _rmsnorm_kernel,
        grid=(t // tm,),
        in_specs=[pl.BlockSpec((tm, k), lambda i: (i, 0)), pl.BlockSpec((1, k), lambda i: (0, 0))],
        out_specs=pl.BlockSpec((tm, k), lambda i: (i, 0)),
        out_shape=jax.ShapeDtypeStruct((t, k), F32),
        compiler_params=_cparams("parallel"),
    )(h, gain.reshape(1, k))


def _ffn_up_kernel(h_ref, hh_ref, gain_ref, wg_ref, wv_ref, cwg_ref, cwv_ref, bg_ref, bv_ref, o_ref,
                   xn_ref, *ext_refs, tiles_per_seq):
    tm, tn = o_ref.shape

    @pl.when(pl.program_id(1) == 0)
    def _():
        first = (pl.program_id(0) % tiles_per_seq) == 0

        def norm(x):
            ms = jnp.mean(x * x, axis=-1, keepdims=True)
            return x * lax.rsqrt(ms + NORM_EPS) * gain_ref[...]

        xn_ref[0:FFN_HALO, :] = jnp.where(first, 0.0, norm(hh_ref[...])).astype(BF16)
        xn_ref[FFN_HALO:, :] = norm(h_ref[...]).astype(BF16)

    def conv(ext_ref, cw_ref, b_ref, cs):
        acc = b_ref[:, cs]
        for k in range(FFN_CONV_WIDTH):
            off = FFN_HALO - (FFN_CONV_WIDTH - 1) + k
            acc = acc + cw_ref[k:k + 1, cs] * ext_ref[off:off + tm, :]
        return acc

    nch = len(ext_refs) // 2
    cw = tn // nch
    for c in range(nch):
        cs = slice(c * cw, (c + 1) * cw)
        extg_ref, extv_ref = ext_refs[2 * c], ext_refs[2 * c + 1]
        extg_ref[...] = _dot(xn_ref[...], wg_ref[:, cs])
        extv_ref[...] = _dot(xn_ref[...], wv_ref[:, cs])
        o_ref[:, cs] = (_silu(conv(extg_ref, cwg_ref, bg_ref, cs))
                        * conv(extv_ref, cwv_ref, bv_ref, cs)).astype(o_ref.dtype)


def ffn_up_gate(h, gain, w_up, conv_w, conv_b, seq, *, tm=TM_PROJ, tn=TN_FFN):
    t, k = h.shape
    nf = D_FF // tn
    hb = tm // FFN_HALO
    return pl.pallas_call(
        functools.partial(_ffn_up_kernel, tiles_per_seq=seq // tm),
        grid=(t // tm, nf),
        in_specs=[pl.BlockSpec((tm, k), lambda i, j: (i, 0)),
                  pl.BlockSpec((FFN_HALO, k), lambda i, j: (jnp.maximum(i * hb - 1, 0), 0)),
                  pl.BlockSpec((1, k), lambda i, j: (0, 0)),
                  pl.BlockSpec((k, tn), lambda i, j: (0, j)),
                  pl.BlockSpec((k, tn), lambda i, j: (0, j + nf)),
                  pl.BlockSpec((FFN_CONV_WIDTH, tn), lambda i, j: (0, j)),
                  pl.BlockSpec((FFN_CONV_WIDTH, tn), lambda i, j: (0, j + nf)),
                  pl.BlockSpec((1, tn), lambda i, j: (0, j)),
                  pl.BlockSpec((1, tn), lambda i, j: (0, j + nf))],
        out_specs=pl.BlockSpec((tm, tn), lambda i, j: (i, j)),
        out_shape=jax.ShapeDtypeStruct((t, D_FF), BF16),
        scratch_shapes=[pltpu.VMEM((tm + FFN_HALO, k), BF16)]
                       + [pltpu.VMEM((tm + FFN_HALO, tn // FFN_CHUNKS), F32)] * (2 * FFN_CHUNKS),
        compiler_params=_cparams("parallel", "arbitrary"),
    )(h, h, gain.reshape(1, k), w_up, w_up, conv_w, conv_w, conv_b.reshape(1, -1), conv_b.reshape(1, -1))


def _ssd_kernel(z_ref, xs_ref, xsh_ref, bc_ref, bch_ref, dt_ref, cwx_ref, cbx_ref, cwb_ref, cbb_ref,
                dtb_ref, alog_ref, dsk_ref, gn_ref, y_ref, extx_ref, extb_ref, state_ref):
    L = SSM_CHUNK
    GW = SSM_INNER // SSM_GROUPS
    first = pl.program_id(1) == 0

    @pl.when(first)
    def _():
        state_ref[...] = jnp.zeros_like(state_ref)

    xs = _silu(_dwconv(extx_ref, xs_ref[...], xsh_ref[...], first, cwx_ref, SSM_CONV_WIDTH) + cbx_ref[...])
    bc = _silu(_dwconv(extb_ref, bc_ref[...], bch_ref[...], first, cwb_ref, SSM_CONV_WIDTH) + cbb_ref[...])

    dt = _softplus(dt_ref[...] + dtb_ref[...])
    a = -jnp.exp(alog_ref[...])
    la = dt * a
    row = lax.broadcasted_iota(jnp.int32, (L, L), 0)
    col = lax.broadcasted_iota(jnp.int32, (L, L), 1)
    causal = col <= row
    cum = _dot_f32_lhs01(causal.astype(BF16), la)
    cum_t = cum.T
    clast = cum[L - 1:L, :]

    e_head = (lax.broadcasted_iota(jnp.int32, (LANES, SSM_INNER), 1) // SSM_HEADDIM
              == lax.broadcasted_iota(jnp.int32, (LANES, SSM_INNER), 0)).astype(BF16)
    x = xs * _dot_f32_rhs01(dt, e_head)
    ecum_x = _dot_f32_rhs01(jnp.exp(cum), e_head)
    dte_x = _dot_f32_rhs01(jnp.exp(clast - cum), e_head)
    cdec_x = _dot_f32_rhs01(jnp.broadcast_to(jnp.exp(clast), (SUBLANES, LANES)), e_head)[0:1, :]
    xb = x.astype(BF16)
    xdte = (x * dte_x).astype(BF16)
    lo_half = lax.broadcasted_iota(jnp.int32, (L, LANES), 1) < SSM_HEADDIM

    for g in range(SSM_GROUPS):
        bg = bc[:, g * SSM_STATE:(g + 1) * SSM_STATE]
        cg = bc[:, (SSM_GROUPS + g) * SSM_STATE:(SSM_GROUPS + g + 1) * SSM_STATE].astype(BF16)
        cb = _dot_nt(cg, bg.astype(BF16))
        hg = state_ref[g]
        y_off = _dot(cg, hg.astype(BF16)) * ecum_x[:, g * GW:(g + 1) * GW]
        pieces = []
        for pr in range(GW // LANES):
            h0 = g * (GW // SSM_HEADDIM) + 2 * pr
            xp = xb[:, h0 * SSM_HEADDIM:h0 * SSM_HEADDIM + LANES]
            ypair = None
            for s in range(2):
                h = h0 + s
                diff = cum[:, h:h + 1] - cum_t[h:h + 1, :]
                decay = jnp.exp(jnp.where(causal, diff, -jnp.inf))
                m = (cb * decay).astype(BF16)
                keep = lo_half if s == 0 else jnp.logical_not(lo_half)
                yh = _dot(m, jnp.where(keep, xp, jnp.zeros_like(xp)))
                ypair = yh if ypair is None else ypair + yh
            pieces.append(ypair)
        y_diag = jnp.concatenate(pieces, axis=-1)
        st = _dot(bg.T.astype(BF16), xdte[:, g * GW:(g + 1) * GW])
        state_ref[g] = hg * cdec_x[:, g * GW:(g + 1) * GW] + st

        yg = y_diag + y_off + dsk_ref[:, g * GW:(g + 1) * GW] * xs[:, g * GW:(g + 1) * GW]
        yg = yg * _silu(z_ref[:, g * GW:(g + 1) * GW])
        ms = jnp.mean(yg * yg, axis=-1, keepdims=True)
        yg = yg * lax.rsqrt(ms + NORM_EPS) * gn_ref[:, g * GW:(g + 1) * GW]
        y_ref[:, g * GW:(g + 1) * GW] = yg.astype(y_ref.dtype)


def ssd_mixer(proj, dt_raw, conv_w, conv_b, dt_bias, a_log, d_skip, ssm_norm, bsz, seq):
    L = SSM_CHUNK
    nc = seq // L
    hb = L // SUBLANES
    pad = lambda v: jnp.pad(v.reshape(1, -1), ((0, 0), (0, LANES - v.shape[-1])))
    rowblk = lambda b, c: b * nc + c
    halo_row = lambda b, c: jnp.maximum((b * nc + c) * hb - 1, 0)
    const = lambda b, c: (0, 0)
    bcw = 2 * SSM_GROUPS * SSM_STATE
    return pl.pallas_call(
        _ssd_kernel,
        grid=(bsz, nc),
        in_specs=[pl.BlockSpec((L, SSM_INNER), lambda b, c: (rowblk(b, c), 0)),
                  pl.BlockSpec((L, SSM_INNER), lambda b, c: (rowblk(b, c), 1)),
                  pl.BlockSpec((SUBLANES, SSM_INNER), lambda b, c: (halo_row(b, c), 1)),
                  pl.BlockSpec((L, bcw), lambda b, c: (rowblk(b, c), 4)),
                  pl.BlockSpec((SUBLANES, bcw), lambda b, c: (halo_row(b, c), 4)),
                  pl.BlockSpec((L, LANES), lambda b, c: (rowblk(b, c), 0)),
                  pl.BlockSpec((SSM_CONV_WIDTH, SSM_INNER), const),
                  pl.BlockSpec((1, SSM_INNER), const),
                  pl.BlockSpec((SSM_CONV_WIDTH, bcw), const),
                  pl.BlockSpec((1, bcw), const),
                  pl.BlockSpec((1, LANES), const),
                  pl.BlockSpec((1, LANES), const),
                  pl.BlockSpec((1, SSM_INNER), const),
                  pl.BlockSpec((1, SSM_INNER), const)],
        out_specs=pl.BlockSpec((L, SSM_INNER), lambda b, c: (rowblk(b, c), 0)),
        out_shape=jax.ShapeDtypeStruct((bsz * seq, SSM_INNER), BF16),
        scratch_shapes=[pltpu.VMEM((L + SUBLANES, SSM_INNER), F32),
                        pltpu.VMEM((L + SUBLANES, bcw), F32),
                        pltpu.VMEM((SSM_GROUPS, SSM_STATE, SSM_INNER // SSM_GROUPS), F32)],
        compiler_params=_cparams("parallel", "arbitrary"),
    )(proj, proj, proj, proj, proj, dt_raw,
      conv_w[:, :SSM_INNER], conv_b[:SSM_INNER].reshape(1, -1),
      conv_w[:, SSM_INNER:], conv_b[SSM_INNER:].reshape(1, -1),
      pad(dt_bias), pad(a_log), jnp.repeat(d_skip, SSM_HEADDIM).reshape(1, -1), ssm_norm.reshape(1, -1))


def _conformer_kernel(ua_ref, ug_ref, uah_ref, ugh_ref, w_ref, b_ref, lg_ref, lb_ref, o_ref, ext_ref,
                      shift_ref, conv_ref, *, tiles_per_seq):
    first = (pl.program_id(0) % tiles_per_seq) == 0
    tm = ua_ref.shape[0]
    ext_ref[0:CONF_HALO, :] = jnp.where(first, 0.0, uah_ref[...] * jax.nn.sigmoid(ugh_ref[...]))
    ext_ref[CONF_HALO:, :] = ua_ref[...] * jax.nn.sigmoid(ug_ref[...])
    span = tm + CONF_HALO - SUBLANES
    for s in range(1, SUBLANES):
        shift_ref[s - 1] = ext_ref[s:s + span, :]
    for c0 in range(0, CONF_WIDTH, LANES):
        cs = slice(c0, c0 + LANES)
        acc = jnp.broadcast_to(b_ref[:, cs], (tm, LANES))
        for k in range(CONF_CONV_WIDTH):
            q, s = divmod(CONF_HALO - (CONF_CONV_WIDTH - 1) + k, SUBLANES)
            rows = slice(q * SUBLANES, q * SUBLANES + tm)
            tap = ext_ref[rows, cs] if s == 0 else shift_ref[s - 1, rows, cs]
            acc = acc + w_ref[k:k + 1, cs] * tap
        conv_ref[:, cs] = acc
    c = conv_ref[...]
    mu = jnp.mean(c, axis=-1, keepdims=True)
    d = c - mu
    var = jnp.mean(d * d, axis=-1, keepdims=True)
    y = d * lax.rsqrt(var + NORM_EPS) * lg_ref[...] + lb_ref[...]
    o_ref[...] = _silu(y).astype(o_ref.dtype)


def conformer_mixer(proj, dw_w, dw_b, ln_g, ln_b, seq, *, tm=TM_CONF):
    t = proj.shape[0]
    hb = tm // CONF_HALO
    halo_row = lambda i: jnp.maximum(i * hb - 1, 0)
    const = lambda i: (0, 0)
    return pl.pallas_call(
        functools.partial(_conformer_kernel, tiles_per_seq=seq // tm),
        grid=(t // tm,),
        in_specs=[pl.BlockSpec((tm, CONF_WIDTH), lambda i: (i, 5)),
                  pl.BlockSpec((tm, CONF_WIDTH), lambda i: (i, 6)),
                  pl.BlockSpec((CONF_HALO, CONF_WIDTH), lambda i: (halo_row(i), 5)),
                  pl.BlockSpec((CONF_HALO, CONF_WIDTH), lambda i: (halo_row(i), 6)),
                  pl.BlockSpec((CONF_CONV_WIDTH, CONF_WIDTH), const),
                  pl.BlockSpec((1, CONF_WIDTH), const),
                  pl.BlockSpec((1, CONF_WIDTH), const),
                  pl.BlockSpec((1, CONF_WIDTH), const)],
        out_specs=pl.BlockSpec((tm, CONF_WIDTH), lambda i: (i, 0)),
        out_shape=jax.ShapeDtypeStruct((t, CONF_WIDTH), BF16),
        scratch_shapes=[pltpu.VMEM((tm + CONF_HALO, CONF_WIDTH), F32),
                        pltpu.VMEM((SUBLANES - 1, tm + CONF_HALO - SUBLANES, CONF_WIDTH), F32),
                        pltpu.VMEM((tm, CONF_WIDTH), F32)],
        compiler_params=_cparams("parallel"),
    )(proj, proj, proj, proj, dw_w, dw_b.reshape(1, -1), ln_g.reshape(1, -1), ln_b.reshape(1, -1))


def _shortconv_kernel(b_ref, c_ref, h_ref, ch_ref, hh_ref, w_ref, o_ref, ext_ref, *, tiles_per_seq):
    first = (pl.program_id(0) % tiles_per_seq) == 0
    conv = _dwconv(ext_ref, c_ref[...] * h_ref[...], ch_ref[...] * hh_ref[...], first, w_ref, SC_CONV_WIDTH)
    o_ref[...] = (b_ref[...] * conv).astype(o_ref.dtype)


def shortconv_mixer(proj, conv_w, seq, *, tm=TM_SC):
    t = proj.shape[0]
    hb = tm // SUBLANES
    halo_row = lambda i: jnp.maximum(i * hb - 1, 0)
    return pl.pallas_call(
        functools.partial(_shortconv_kernel, tiles_per_seq=seq // tm),
        grid=(t // tm,),
        in_specs=[pl.BlockSpec((tm, SC_WIDTH), lambda i: (i, 0)),
                  pl.BlockSpec((tm, SC_WIDTH), lambda i: (i, 1)),
                  pl.BlockSpec((tm, SC_WIDTH), lambda i: (i, 2)),
                  pl.BlockSpec((SUBLANES, SC_WIDTH), lambda i: (halo_row(i), 1)),
                  pl.BlockSpec((SUBLANES, SC_WIDTH), lambda i: (halo_row(i), 2)),
                  pl.BlockSpec((SC_CONV_WIDTH, SC_WIDTH), lambda i: (0, 0))],
        out_specs=pl.BlockSpec((tm, SC_WIDTH), lambda i: (i, 0)),
        out_shape=jax.ShapeDtypeStruct((t, SC_WIDTH), BF16),
        scratch_shapes=[pltpu.VMEM((tm + SUBLANES, SC_WIDTH), F32)],
        compiler_params=_cparams("parallel"),
    )(proj, proj, proj, proj, proj, conv_w)


def _rope_table_kernel(pos_ref, inv_ref, cos_ref, sin_ref):
    ang = pos_ref[...].astype(F32) * inv_ref[...]
    sign = jnp.where(lax.broadcasted_iota(jnp.int32, ang.shape, 1) < NSA_HEAD_DIM // 2, -1.0, 1.0)
    cos_ref[...] = jnp.cos(ang)
    sin_ref[...] = jnp.sin(ang) * sign


def rope_tables(positions, *, tm=TM_ROPE):
    t = positions.size
    half = NSA_HEAD_DIM // 2
    inv = ROPE_THETA ** (-jnp.arange(half, dtype=F32) / half)
    inv = jnp.concatenate([inv, inv]).reshape(1, NSA_HEAD_DIM)
    tab = jax.ShapeDtypeStruct((t, NSA_HEAD_DIM), F32)
    return pl.pallas_call(
        _rope_table_kernel,
        grid=(t // tm,),
        in_specs=[pl.BlockSpec((tm, 1), lambda i: (i, 0)), pl.BlockSpec((1, NSA_HEAD_DIM), lambda i: (0, 0))],
        out_specs=[pl.BlockSpec((tm, NSA_HEAD_DIM), lambda i: (i, 0))] * 2,
        out_shape=[tab, tab],
        compiler_params=_cparams("parallel"),
    )(positions.reshape(t, 1), inv)


def _rope(x, cos, sin_signed):
    return x * cos + pltpu.roll(x, NSA_HEAD_DIM // 2, axis=1) * sin_signed


def _kv_prep_kernel(ks_ref, vs_ref, kw_ref, vw_ref, cos_ref, sin_ref, kso_ref, vso_ref, kwo_ref, vwo_ref,
                    *, tiles_per_seq):
    D = NSA_HEAD_DIM
    tm = ks_ref.shape[0]
    cos, sin = cos_ref[...], sin_ref[...]
    row = (pl.program_id(0) % tiles_per_seq) * tm + lax.broadcasted_iota(jnp.int32, (tm, D), 0)
    lane = lax.broadcasted_iota(jnp.int32, (tm, D), 1)
    onehot = jnp.where(row // SLC_BLOCK == lane, 1.0, 0.0).astype(BF16)
    ones_col = jnp.where(lane == 0, 1.0, 0.0).astype(BF16)
    for g in range(NSA_KV_HEADS):
        sl = slice(g * D, (g + 1) * D)
        kso_ref[:, 2 * g * D:(2 * g + 1) * D] = _rope(ks_ref[:, sl], cos, sin).astype(BF16)
        kso_ref[:, (2 * g + 1) * D:(2 * g + 2) * D] = onehot
        kwo_ref[:, sl] = _rope(kw_ref[:, sl], cos, sin).astype(BF16)
        vso_ref[:, 2 * g * D:(2 * g + 1) * D] = vs_ref[:, sl].astype(BF16)
        vso_ref[:, (2 * g + 1) * D:(2 * g + 2) * D] = ones_col
        vwo_ref[:, 2 * g * D:(2 * g + 1) * D] = vw_ref[:, sl].astype(BF16)
        vwo_ref[:, (2 * g + 1) * D:(2 * g + 2) * D] = ones_col


def kv_prep(proj, cos, sin, seq, *, tm=512):
    t = proj.shape[0]
    kv = jax.ShapeDtypeStruct((t, NSA_KV), BF16)
    kv_aug = jax.ShapeDtypeStruct((t, 2 * NSA_KV), BF16)
    spec = pl.BlockSpec((tm, NSA_KV), lambda i: (i, 0))
    spec_aug = pl.BlockSpec((tm, 2 * NSA_KV), lambda i: (i, 0))
    return pl.pallas_call(
        functools.partial(_kv_prep_kernel, tiles_per_seq=seq // tm),
        grid=(t // tm,),
        in_specs=[pl.BlockSpec((tm, NSA_KV), lambda i: (i, 12)),
                  pl.BlockSpec((tm, NSA_KV), lambda i: (i, 13)),
                  pl.BlockSpec((tm, NSA_KV), lambda i: (i, 14)),
                  pl.BlockSpec((tm, NSA_KV), lambda i: (i, 15)),
                  pl.BlockSpec((tm, NSA_HEAD_DIM), lambda i: (i, 0)),
                  pl.BlockSpec((tm, NSA_HEAD_DIM), lambda i: (i, 0))],
        out_specs=[spec_aug, spec_aug, spec, spec_aug],
        out_shape=[kv_aug, kv_aug, kv, kv_aug],
        compiler_params=_cparams("parallel"),
    )(proj, proj, proj, proj, cos, sin)


def _compress_kernel(kt_ref, vt_ref, cos_ref, sin_ref, pe_ref, w1_ref, w2_ref, kc_ref, vc_ref):
    nb = kc_ref.shape[0]
    half = CMP_BLOCK // 2

    def compress(tok_ref, which):
        acc_lo = jnp.zeros((nb, NSA_HEAD_DIM), F32)
        acc_hi = jnp.zeros((nb, NSA_HEAD_DIM), F32)
        for l in range(half):
            r = tok_ref[pl.ds(l, nb, stride=CMP_STRIDE), :]
            lo = (r + pe_ref[which, l:l + 1, :]).astype(BF16)
            hi = (r + pe_ref[which, half + l:half + l + 1, :]).astype(BF16)
            acc_lo = acc_lo + _dot(lo, w1_ref[which, l * NSA_HEAD_DIM:(l + 1) * NSA_HEAD_DIM, :])
            acc_hi = acc_hi + _dot(hi, w1_ref[which, (half + l) * NSA_HEAD_DIM:(half + l + 1) * NSA_HEAD_DIM, :])
        pre = acc_lo + pltpu.roll(acc_hi, nb - 1, axis=0)
        return _dot(_silu(pre).astype(BF16), w2_ref[which])

    cos_e = pltpu.roll(cos_ref[pl.ds(CMP_STRIDE - 1, nb, stride=CMP_STRIDE), :], nb - 1, axis=0)
    sin_e = pltpu.roll(sin_ref[pl.ds(CMP_STRIDE - 1, nb, stride=CMP_STRIDE), :], nb - 1, axis=0)
    kc_ref[...] = _rope(compress(kt_ref, 0), cos_e, sin_e).astype(BF16)
    vc_ref[...] = compress(vt_ref, 1).astype(BF16)


def compress_kv(proj, cos, sin, pe, w1, w2, bsz, seq):
    nb = seq // CMP_STRIDE
    out = jax.ShapeDtypeStruct((bsz, NSA_KV_HEADS, nb, NSA_HEAD_DIM), BF16)
    full = lambda *shape: pl.BlockSpec(shape, lambda b, g: (0,) * len(shape))
    return pl.pallas_call(
        _compress_kernel,
        grid=(bsz, NSA_KV_HEADS),
        in_specs=[pl.BlockSpec((seq, NSA_HEAD_DIM), lambda b, g: (b, 40 + g)),
                  pl.BlockSpec((seq, NSA_HEAD_DIM), lambda b, g: (b, 44 + g)),
                  pl.BlockSpec((seq, NSA_HEAD_DIM), lambda b, g: (b, 0)),
                  pl.BlockSpec((seq, NSA_HEAD_DIM), lambda b, g: (b, 0)),
                  full(2, CMP_BLOCK, NSA_HEAD_DIM),
                  full(2, CMP_BLOCK * NSA_HEAD_DIM, NSA_HEAD_DIM),
                  full(2, NSA_HEAD_DIM, NSA_HEAD_DIM)],
        out_specs=[pl.BlockSpec((None, None, nb, NSA_HEAD_DIM), lambda b, g: (b, g, 0, 0))] * 2,
        out_shape=[out, out],
        compiler_params=_cparams("parallel", "parallel"),
    )(proj, proj, cos, sin, pe, w1, w2)


def _nsa_kernel(q_ref, cos_ref, sin_ref, gate_ref, kc_ref, vc_ref, ks_ref, vs_ref, kw_ref, vw_ref,
                o_ref, qa_ref, impt_ref, sc_ref, mrun_ref, acc_ref):
    G = NSA_GROUP
    D = NSA_HEAD_DIM
    R = G * TQ
    q0 = pl.program_id(2) * TQ
    scale = D ** -0.5
    cos, sin = cos_ref[...], sin_ref[...]
    for e in range(G):
        xq = _rope(q_ref[:, e * D:(e + 1) * D], cos, sin)
        qa_ref[e * TQ:(e + 1) * TQ, 0:D] = (xq * scale).astype(BF16)
    qs = qa_ref[:, 0:D]
    t_col = q0 + lax.broadcasted_iota(jnp.int32, (TQ, 1), 0)
    n_cmp = kc_ref.shape[0]
    n_slc = ks_ref.shape[0] // SLC_BLOCK

    ends = lax.broadcasted_iota(jnp.int32, (1, n_cmp), 1) * CMP_STRIDE + (CMP_BLOCK - 1)
    cvalid = ends <= t_col
    cbias = jnp.where(cvalid, 0.0, NEG_BIG)
    c01 = jnp.where(cvalid, 1.0, 0.0)
    s = _dot_nt(qs, kc_ref[...]).reshape(G, TQ, n_cmp) + cbias[None]
    m = jnp.max(s, axis=-1, keepdims=True)
    p = jnp.exp(s - m) * c01[None]
    l = jnp.sum(p, axis=-1, keepdims=True)
    p = p * (1.0 / jnp.maximum(l, 1e-30))
    o_cmp = _dot(p.reshape(R, n_cmp).astype(BF16), vc_ref[...])

    wk = WINDOW + TQ
    start = pl.multiple_of(jnp.maximum(q0 - WINDOW, 0), TQ)
    d = q0 - start
    rel = lax.broadcasted_iota(jnp.int32, (TQ, wk), 1) - lax.broadcasted_iota(jnp.int32, (TQ, wk), 0)
    wbias = jnp.where((rel <= d) & (rel > d - WINDOW), 0.0, NEG_BIG)
    sw = _dot_nt(qs, kw_ref[pl.ds(start, wk), :]).reshape(G, TQ, wk) + wbias[None]
    mw = jnp.max(sw, axis=-1, keepdims=True)
    pw = jnp.exp(sw - mw).reshape(R, wk)
    accw = _dot(pw.astype(BF16), vw_ref[pl.ds(start, wk), :])
    o_win = accw[:, 0:D] * (1.0 / accw[:, D:D + 1])

    psum = p[0]
    for e in range(1, G):
        psum = psum + p[e]
    n_i = lax.broadcasted_iota(jnp.int32, (n_cmp, LANES), 0)
    j_i = lax.broadcasted_iota(jnp.int32, (n_cmp, LANES), 1)
    overlap = ((n_i * CMP_STRIDE < (j_i + 1) * SLC_BLOCK)
               & (n_i * CMP_STRIDE + CMP_BLOCK - 1 >= j_i * SLC_BLOCK)).astype(BF16)
    imp = _dot_f32_rhs01(psum, overlap)
    j = lax.broadcasted_iota(jnp.int32, (1, LANES), 1)
    cur = t_col // SLC_BLOCK
    forced = (j == 0) | ((j <= cur) & (j > cur - N_LOCAL))
    valid = j * SLC_BLOCK <= t_col
    imp = jnp.where(forced, FORCE, jnp.where(valid, imp, -FORCE))
    impt_ref[...] = imp.T
    nrb = n_slc // SUBLANES
    vals = [impt_ref[rb * SUBLANES:(rb + 1) * SUBLANES, :] for rb in range(nrb)]
    cnt = [jnp.zeros((SUBLANES, TQ), F32) for _ in range(nrb)]
    sub = lax.broadcasted_iota(jnp.int32, (SUBLANES, TQ), 0)
    for k in range(n_slc):
        vk = jnp.broadcast_to(impt_ref[k:k + 1, :], (SUBLANES, TQ))
        for rb in range(nrb):
            if rb * SUBLANES > k:
                beats = vk >= vals[rb]
            elif rb * SUBLANES + SUBLANES - 1 < k:
                beats = vk > vals[rb]
            else:
                beats = (vk > vals[rb]) | ((vk == vals[rb]) & (sub > k - rb * SUBLANES))
            cnt[rb] = cnt[rb] + jnp.where(beats, 1.0, 0.0)
    sel_t = jnp.concatenate([jnp.where(c < N_SELECT, 1.0, 0.0) for c in cnt]
                            + [jnp.zeros((LANES - n_slc, TQ), F32)], axis=0)
    sel = sel_t.T
    sbias = jnp.where((sel > 0.5) & (j * SLC_BLOCK < q0), 0.0, NEG_BIG).astype(BF16)
    for e in range(G):
        qa_ref[e * TQ:(e + 1) * TQ, D:2 * D] = sbias

    lower = (lax.broadcasted_iota(jnp.int32, (TQ, TQ), 1) <= lax.broadcasted_iota(jnp.int32, (TQ, TQ), 0))
    s_own = (_dot_nt(qs, ks_ref[pl.ds(q0, TQ), 0:D]).reshape(G, TQ, TQ)
             + jnp.where(lower, 0.0, NEG_BIG)[None]).reshape(R, TQ)
    mrun_ref[...] = s_own
    n_kt = (q0 + TK - 1) // TK

    def score_body(kt, carry):
        k0 = pl.multiple_of(kt * TK, TK)
        s = _dot_nt(qa_ref[...], ks_ref[pl.ds(k0, TK), :])
        sc_ref[kt] = s
        t = s[:, 0:LANES]
        for c in range(1, TK // LANES):
            t = jnp.maximum(t, s[:, c * LANES:(c + 1) * LANES])
        mrun_ref[...] = jnp.maximum(mrun_ref[...], t)
        return carry

    lax.fori_loop(0, n_kt, score_body, 0)
    m = jnp.max(mrun_ref[...], axis=-1, keepdims=True)
    acc_ref[...] = _dot(jnp.exp(s_own - m).astype(BF16), vs_ref[pl.ds(q0, TQ), :])

    def pv_body(kt, carry):
        k0 = pl.multiple_of(kt * TK, TK)
        p = jnp.exp(sc_ref[kt] - m).astype(BF16)
        acc_ref[...] += _dot(p, vs_ref[pl.ds(k0, TK), :])
        return carry

    lax.fori_loop(0, n_kt, pv_body, 0)
    acc = acc_ref[...]
    o_slc = acc[:, 0:D] * (1.0 / acc[:, D:D + 1])

    gate =jax.nn.sigmoid(gate_ref[...])
    for e in range(G):
        rows = slice(e * TQ, (e + 1) * TQ)
        o = (gate[:, 3 * e:3 * e + 1] * o_cmp[rows] + gate[:, 3 * e + 1:3 * e + 2] * o_slc[rows]
             + gate[:, 3 * e + 2:3 * e + 3] * o_win[rows])
        o_ref[:, e * D:(e + 1) * D] = o.astype(o_ref.dtype)


def nsa_attention(proj, gates, cos, sin, kc, vc, ks, vs, kw, vw, bsz, seq):
    nq = seq // TQ
    qw = NSA_GROUP * NSA_HEAD_DIM
    row = lambda b, g, i: b * nq + i
    kvspec = pl.BlockSpec((seq, NSA_HEAD_DIM), lambda b, g, i: (b, g))
    augspec = pl.BlockSpec((seq, 2 * NSA_HEAD_DIM), lambda b, g, i: (b, g))
    cspec = pl.BlockSpec((None, None, seq // CMP_STRIDE, NSA_HEAD_DIM), lambda b, g, i: (b, g, 0, 0))
    return pl.pallas_call(
        _nsa_kernel,
        grid=(bsz, NSA_KV_HEADS, nq),
        in_specs=[pl.BlockSpec((TQ, qw), lambda b, g, i: (row(b, g, i), 6 + g)),
                  pl.BlockSpec((TQ, NSA_HEAD_DIM), lambda b, g, i: (row(b, g, i), 0)),
                  pl.BlockSpec((TQ, NSA_HEAD_DIM), lambda b, g, i: (row(b, g, i), 0)),
                  pl.BlockSpec((TQ, LANES), lambda b, g, i: (row(b, g, i), g)),
                  cspec, cspec, augspec, augspec, kvspec, augspec],
        out_specs=pl.BlockSpec((TQ, qw), lambda b, g, i: (row(b, g, i), g)),
        out_shape=jax.ShapeDtypeStruct((bsz * seq, NSA_Q), BF16),
        scratch_shapes=[pltpu.VMEM((NSA_GROUP * TQ, 2 * NSA_HEAD_DIM), BF16),
                        pltpu.VMEM((LANES, TQ), F32),
                        pltpu.VMEM((seq // TK, NSA_GROUP * TQ, TK), F32),
                        pltpu.VMEM((NSA_GROUP * TQ, LANES), F32),
                        pltpu.VMEM((NSA_GROUP * TQ, 2 * NSA_HEAD_DIM), F32)],
        compiler_params=_cparams("parallel", "parallel", "parallel"),
    )(proj, cos, sin, gates, kc, vc, ks, vs, kw, vw)


def _even_layer(h, gain, w_in, conv_w, conv_b, dt_bias, a_log, d_skip, ssm_norm,
                dw_w, dw_b, ln_g, ln_b, w_out, bsz, seq):
    xbc_end = SSM_INNER + SSM_INNER + 2 * SSM_GROUPS * SSM_STATE
    dt_end = xbc_end + SSM_HEADS
    w_main = jnp.concatenate([w_in[:, :xbc_end], w_in[:, dt_end:]], axis=1).astype(BF16)
    w_dt = jnp.pad(w_in[:, xbc_end:dt_end], ((0, 0), (0, LANES - SSM_HEADS))).astype(BF16)
    proj = norm_matmul(h, gain, w_main)
    dt_raw = norm_matmul(h, gain, w_dt)
    y = ssd_mixer(proj, dt_raw, conv_w, conv_b, dt_bias, a_log, d_skip, ssm_norm, bsz, seq)
    u = conformer_mixer(proj, dw_w, dw_b, ln_g, ln_b, seq)
    w_out = w_out.astype(BF16)
    return matmul_res([y, u], [w_out[:SSM_INNER], w_out[SSM_INNER:]], h)


def _odd_layer(h, gain, cos, sin, w_in, sc_w, pe, w1, w2, w_out, bsz, seq):
    main = 3 * SC_WIDTH + NSA_Q + 6 * NSA_KV
    per_g = 3 * NSA_GROUP
    wg = w_in[:, main:].reshape(D_MODEL, NSA_KV_HEADS, per_g)
    wg = jnp.pad(wg, ((0, 0), (0, 0), (0, LANES - per_g))).reshape(D_MODEL, NSA_KV_HEADS * LANES)
    proj = norm_matmul(h, gain, w_in[:, :main].astype(BF16))
    gates = norm_matmul(h, gain, wg.astype(BF16), tn=NSA_KV_HEADS * LANES)
    y_c = shortconv_mixer(proj, sc_w, seq)
    ks, vs, kw, vw = kv_prep(proj, cos, sin, seq)
    kc, vc = compress_kv(proj, cos, sin, pe, w1.astype(BF16), w2.astype(BF16), bsz, seq)
    y_d = nsa_attention(proj, gates, cos, sin, kc, vc, ks, vs, kw, vw, bsz, seq)
    w_out = w_out.astype(BF16)
    return matmul_res([y_c, y_d], [w_out[:SC_WIDTH], w_out[SC_WIDTH:]], h)


def _ffn(h, gain, w_up, conv_w, conv_b, w_down, seq):
    a = ffn_up_gate(h, gain, w_up.astype(BF16), conv_w, conv_b, seq)
    return matmul_res([a], [w_down.astype(BF16)], h)


def kernel(x, positions, mix_norm, ffn_norm, final_norm, ab_w_in, ssm_conv_w, ssm_conv_b, ssm_dt_bias, ssm_a_log, ssm_d, ssm_norm, conf_dw_w, conf_dw_b, conf_ln_g, conf_ln_b, ab_w_out, cd_w_in, sc_conv_w, nsa_cmp_pe, nsa_cmp_w1, nsa_cmp_w2, cd_w_out, ffn_w_up, ffn_conv_w, ffn_conv_b, ffn_w_down):
    bsz, seq, d = x.shape
    depth = mix_norm.shape[0]
    h = x.reshape(bsz * seq, d)
    cos, sin = rope_tables(positions)
    for layer in range(depth):
        i = layer // 2
        if layer % 2 == 0:
            h = _even_layer(h, mix_norm[layer], ab_w_in[i], ssm_conv_w[i], ssm_conv_b[i], ssm_dt_bias[i],
                            ssm_a_log[i], ssm_d[i], ssm_norm[i], conf_dw_w[i], conf_dw_b[i], conf_ln_g[i],
                            conf_ln_b[i], ab_w_out[i], bsz, seq)
        else:
            h = _odd_layer(h, mix_norm[layer], cos, sin, cd_w_in[i], sc_conv_w[i], nsa_cmp_pe[i],
                           nsa_cmp_w1[i], nsa_cmp_w2[i], cd_w_out[i], bsz, seq)
        h = _ffn(h, ffn_norm[layer], ffn_w_up[layer], ffn_conv_w[layer], ffn_conv_b[layer],
                 ffn_w_down[layer], seq)
    return rmsnorm_rows(h, final_norm).reshape(bsz, seq, d)
```

```python
import functools

import jax
import jax.numpy as jnp
from jax import lax
from jax.experimental import pallas as pl
from jax.experimental.pallas import tpu as pltpu

F32 = jnp.float32
BF16 = jnp.bfloat16

D_MODEL = 2048
NORM_EPS = 1e-6
SSM_INNER = 2048
SSM_HEADDIM = 64
SSM_HEADS = 32
SSM_GROUPS = 4
SSM_STATE = 128
SSM_CONV_WIDTH = 4
SSM_CHUNK = 128
CONF_WIDTH = 1024
CONF_CONV_WIDTH = 31
SC_WIDTH = 1024
SC_CONV_WIDTH = 3
NSA_HEAD_DIM = 128
NSA_HEADS = 16
NSA_KV_HEADS = 4
NSA_GROUP = 4
CMP_BLOCK = 32
CMP_STRIDE = 16
SLC_BLOCK = 64
N_SELECT = 16
N_LOCAL = 2
WINDOW = 512
ROPE_THETA = 10000.0
NSA_Q = 2048
NSA_KV = 512
D_FF = 5632
FFN_CONV_WIDTH = 3
NEG_BIG = -1e30
FORCE = 1e9

V7X_VMEM_BYTES = 64 * 1024 * 1024
VMEM_LIMIT = V7X_VMEM_BYTES - 8 * 1024 * 1024
LANES = 128
SUBLANES = 8

TM_PROJ = 1024
TN_PROJ = 1024
TM_OUT = 512
TN_OUT = 1024
TN_FFN = 512
FFN_HALO = 16
FFN_CHUNKS = 2
TM_CONF = 256
CONF_HALO = 32
TM_SC = 512
TM_ROPE = 1024
TQ = 128
TK = 1024


def _cparams(*sem):
    return pltpu.CompilerParams(dimension_semantics=sem, vmem_limit_bytes=VMEM_LIMIT)


def _silu(x):
    return x * jax.nn.sigmoid(x)


def _softplus(x):
    return jnp.maximum(x, 0.0) + jnp.log1p(jnp.exp(-jnp.abs(x)))


def _split3(x):
    hi = x.astype(BF16)
    r1 = x - hi.astype(F32)
    mid = r1.astype(BF16)
    lo = (r1 - mid.astype(F32)).astype(BF16)
    return hi, mid, lo


def _dot(a, b):
    return jnp.dot(a, b, preferred_element_type=F32)


def _dot_nt(a, b):
    return lax.dot_general(a, b, (((1,), (1,)), ((), ())), preferred_element_type=F32)


def _dot_f32_rhs01(x, e01):
    hi, mid, lo = _split3(x)
    return _dot(hi, e01) + _dot(mid, e01) + _dot(lo, e01)


def _dot_f32_lhs01(e01, x):
    hi, mid, lo = _split3(x)
    return _dot(e01, hi) + _dot(e01, mid) + _dot(e01, lo)


def _dwconv(ext_ref, x, halo, first, w_ref, width):
    rows = x.shape[0]
    hrows = halo.shape[0]
    ext_ref[0:hrows, :] = jnp.where(first, 0.0, halo)
    ext_ref[hrows:hrows + rows, :] = x
    acc = None
    for k in range(width):
        off = hrows - (width - 1) + k
        term = w_ref[k:k + 1, :] * ext_ref[off:off + rows, :]
        acc = term if acc is None else acc + term
    return acc


def _norm_matmul_kernel(h_ref, g_ref, w_ref, o_ref, xn_ref):
    @pl.when(pl.program_id(1) == 0)
    def _():
        x = h_ref[...]
        ms = jnp.mean(x * x, axis=-1, keepdims=True)
        xn_ref[...] = (x * lax.rsqrt(ms + NORM_EPS) * g_ref[...]).astype(BF16)

    o_ref[...] = _dot(xn_ref[...], w_ref[...]).astype(o_ref.dtype)


def norm_matmul(h, gain, w, *, layer=None, n=None, tm=TM_PROJ, tn=TN_PROJ):
    t, k = h.shape
    if layer is None:
        n = w.shape[1]
        tn = min(tn, n)
        w_spec = pl.BlockSpec((k, tn), lambda i, j: (0, j))
    else:
        w_spec = pl.BlockSpec((None, k, tn), lambda i, j: (layer, 0, j))
    return pl.pallas_call(
        _norm_matmul_kernel,
        grid=(t // tm, n // tn),
        in_specs=[pl.BlockSpec((tm, k), lambda i, j: (i, 0)),
                  pl.BlockSpec((1, k), lambda i, j: (0, 0)),
                  w_spec],
        out_specs=pl.BlockSpec((tm, tn), lambda i, j: (i, j)),
        out_shape=jax.ShapeDtypeStruct((t, n), F32),
        scratch_shapes=[pltpu.VMEM((tm, k), BF16)],
        compiler_params=_cparams("parallel", "arbitrary"),
    )(h, gain.reshape(1, k), w)


def _matmul_res_kernel(*refs, n_in):
    a_refs, w_refs = refs[:n_in], refs[n_in:2 * n_in]
    r_ref, o_ref = refs[2 * n_in], refs[2 * n_in + 1]
    acc = _dot(a_refs[0][...], w_refs[0][...])
    for a_ref, w_ref in zip(a_refs[1:], w_refs[1:]):
        acc = acc + _dot(a_ref[...], w_ref[...])
    o_ref[...] = r_ref[...] + acc


def matmul_res(terms, w, layer, res, *, tm=TM_OUT, tn=TN_OUT):
    t, n = res.shape
    in_specs = ([pl.BlockSpec((tm, kw), lambda i, j, ca=ca: (i, ca)) for _, ca, _, kw in terms]
                + [pl.BlockSpec((None, kw, tn), lambda i, j, rw=rw: (layer, rw, j)) for _, _, rw, kw in terms]
                + [pl.BlockSpec((tm, tn), lambda i, j: (i, j))])
    return pl.pallas_call(
        functools.partial(_matmul_res_kernel, n_in=len(terms)),
        grid=(t // tm, n // tn),
        in_specs=in_specs,
        out_specs=pl.BlockSpec((tm, tn), lambda i, j: (i, j)),
        out_shape=jax.ShapeDtypeStruct((t, n), F32),
        compiler_params=_cparams("parallel", "parallel"),
    )(*[a for a, _, _, _ in terms], *([w] * len(terms)), res)


def _rmsnorm_kernel(h_ref, g_ref, o_ref):
    x = h_ref[...]
    ms = jnp.mean(x * x, axis=-1, keepdims=True)
    o_ref[...] = x * lax.rsqrt(ms + NORM_EPS) * g_ref[...]


def rmsnorm_rows(h, gain, *, tm=512):
    t, k = h.shape
    return pl.pallas_call(
        _rmsnorm_kernel,
        grid=(t // tm,),
        in_specs=[pl.BlockSpec((tm, k), lambda i: (i, 0)), pl.BlockSpec((1, k), lambda i: (0, 0))],
        out_specs=pl.BlockSpec((tm, k), lambda i: (i, 0)),
        out_shape=jax.ShapeDtypeStruct((t, k), F32),
        compiler_params=_cparams("parallel"),
    )(h, gain.reshape(1, k))


def _ffn_up_kernel(h_ref, hh_ref, gain_ref, wg_ref, wv_ref, cwg_ref, cwv_ref, bg_ref, bv_ref, o_ref,
                   xn_ref, *ext_refs, tiles_per_seq):
    tm, tn = o_ref.shape

    @pl.when(pl.program_id(1) == 0)
    def _():
        first = (pl.program_id(0) % tiles_per_seq) == 0

        def norm(x):
            ms = jnp.mean(x * x, axis=-1, keepdims=True)
            return x * lax.rsqrt(ms + NORM_EPS) * gain_ref[...]

        xn_ref[0:FFN_HALO, :] = jnp.where(first, 0.0, norm(hh_ref[...])).astype(BF16)
        xn_ref[FFN_HALO:, :] = norm(h_ref[...]).astype(BF16)

    def conv(ext_ref, cw_ref, b_ref, cs):
        acc = b_ref[:, cs]
        for k in range(FFN_CONV_WIDTH):
            off = FFN_HALO - (FFN_CONV_WIDTH - 1) + k
            acc = acc + cw_ref[k:k + 1, cs] * ext_ref[off:off + tm, :]
        return acc

    nch = len(ext_refs) // 2
    cw = tn // nch
    for c in range(nch):
        cs = slice(c * cw, (c + 1) * cw)
        extg_ref, extv_ref = ext_refs[2 * c], ext_refs[2 * c + 1]
        extg_ref[...] = _dot(xn_ref[...], wg_ref[:, cs])
        extv_ref[...] = _dot(xn_ref[...], wv_ref[:, cs])
        o_ref[:, cs] = (_silu(conv(extg_ref, cwg_ref, bg_ref, cs))
                        * conv(extv_ref, cwv_ref, bv_ref, cs)).astype(o_ref.dtype)


def ffn_up_gate(h, gain, w_up, layer, conv_w, conv_b, seq, *, tm=TM_PROJ, tn=TN_FFN):
    t, k = h.shape
    nf = D_FF // tn
    hb = tm // FFN_HALO
    return pl.pallas_call(
        functools.partial(_ffn_up_kernel, tiles_per_seq=seq // tm),
        grid=(t // tm, nf),
        in_specs=[pl.BlockSpec((tm, k), lambda i, j: (i, 0)),
                  pl.BlockSpec((FFN_HALO, k), lambda i, j: (jnp.maximum(i * hb - 1, 0), 0)),
                  pl.BlockSpec((1, k), lambda i, j: (0, 0)),
                  pl.BlockSpec((None, k, tn), lambda i, j: (layer, 0, j)),
                  pl.BlockSpec((None, k, tn), lambda i, j: (layer, 0, j + nf)),
                  pl.BlockSpec((FFN_CONV_WIDTH, tn), lambda i, j: (0, j)),
                  pl.BlockSpec((FFN_CONV_WIDTH, tn), lambda i, j: (0, j + nf)),
                  pl.BlockSpec((1, tn), lambda i, j: (0, j)),
                  pl.BlockSpec((1, tn), lambda i, j: (0, j + nf))],
        out_specs=pl.BlockSpec((tm, tn), lambda i, j: (i, j)),
        out_shape=jax.ShapeDtypeStruct((t, D_FF), BF16),
        scratch_shapes=[pltpu.VMEM((tm + FFN_HALO, k), BF16)]
                       + [pltpu.VMEM((tm + FFN_HALO, tn // FFN_CHUNKS), F32)] * (2 * FFN_CHUNKS),
        compiler_params=_cparams("parallel", "arbitrary"),
    )(h, h, gain.reshape(1, k), w_up, w_up, conv_w, conv_w, conv_b.reshape(1, -1), conv_b.reshape(1, -1))


def _ssd_kernel(z_ref, xs_ref, xsh_ref, bc_ref, bch_ref, dt_ref, cwx_ref, cbx_ref, cwb_ref, cbb_ref,
                dtb_ref, alog_ref, dsk_ref, gn_ref, y_ref, extx_ref, extb_ref, state_ref):
    L = SSM_CHUNK
    GW = SSM_INNER // SSM_GROUPS
    first = pl.program_id(1) == 0

    @pl.when(first)
    def _():
        state_ref[...] = jnp.zeros_like(state_ref)

    xs = _silu(_dwconv(extx_ref, xs_ref[...], xsh_ref[...], first, cwx_ref, SSM_CONV_WIDTH) + cbx_ref[...])
    bc = _silu(_dwconv(extb_ref, bc_ref[...], bch_ref[...], first, cwb_ref, SSM_CONV_WIDTH) + cbb_ref[...])

    dt = _softplus(dt_ref[...] + dtb_ref[...])
    a = -jnp.exp(alog_ref[...])
    la = dt * a
    row = lax.broadcasted_iota(jnp.int32, (L, L), 0)
    col = lax.broadcasted_iota(jnp.int32, (L, L), 1)
    causal = col <= row
    cum = _dot_f32_lhs01(causal.astype(BF16), la)
    cum_t = cum.T
    clast = cum[L - 1:L, :]

    e_head = (lax.broadcasted_iota(jnp.int32, (LANES, SSM_INNER), 1) // SSM_HEADDIM
              == lax.broadcasted_iota(jnp.int32, (LANES, SSM_INNER), 0)).astype(BF16)
    x = xs * _dot_f32_rhs01(dt, e_head)
    ecum_x = _dot_f32_rhs01(jnp.exp(cum), e_head)
    dte_x = _dot_f32_rhs01(jnp.exp(clast - cum), e_head)
    cdec_x = _dot_f32_rhs01(jnp.broadcast_to(jnp.exp(clast), (SUBLANES, LANES)), e_head)[0:1, :]
    xb = x.astype(BF16)
    xdte = (x * dte_x).astype(BF16)
    lo_half = lax.broadcasted_iota(jnp.int32, (L, LANES), 1) < SSM_HEADDIM

    for g in range(SSM_GROUPS):
        bg = bc[:, g * SSM_STATE:(g + 1) * SSM_STATE]
        cg = bc[:, (SSM_GROUPS + g) * SSM_STATE:(SSM_GROUPS + g + 1) * SSM_STATE].astype(BF16)
        cb = _dot_nt(cg, bg.astype(BF16))
        hg = state_ref[g]
        y_off = _dot(cg, hg.astype(BF16)) * ecum_x[:, g * GW:(g + 1) * GW]
        pieces = []
        for pr in range(GW // LANES):
            h0 = g * (GW // SSM_HEADDIM) + 2 * pr
            xp = xb[:, h0 * SSM_HEADDIM:h0 * SSM_HEADDIM + LANES]
            ypair = None
            for s in range(2):
                h = h0 + s
                diff = cum[:, h:h + 1] - cum_t[h:h + 1, :]
                decay = jnp.exp(jnp.where(causal, diff, -jnp.inf))
                m = (cb * decay).astype(BF16)
                keep = lo_half if s == 0 else jnp.logical_not(lo_half)
                yh = _dot(m, jnp.where(keep, xp, jnp.zeros_like(xp)))
                ypair = yh if ypair is None else ypair + yh
            pieces.append(ypair)
        y_diag = jnp.concatenate(pieces, axis=-1)
        st = _dot(bg.T.astype(BF16), xdte[:, g * GW:(g + 1) * GW])
        state_ref[g] = hg * cdec_x[:, g * GW:(g + 1) * GW] + st

        yg = y_diag + y_off + dsk_ref[:, g * GW:(g + 1) * GW] * xs[:, g * GW:(g + 1) * GW]
        yg = yg * _silu(z_ref[:, g * GW:(g + 1) * GW])
        ms = jnp.mean(yg * yg, axis=-1, keepdims=True)
        yg = yg * lax.rsqrt(ms + NORM_EPS) * gn_ref[:, g * GW:(g + 1) * GW]
        y_ref[:, g * GW:(g + 1) * GW] = yg.astype(y_ref.dtype)


def ssd_mixer(proj, dt_raw, conv_w, conv_b, dt_bias, a_log, d_skip, ssm_norm, bsz, seq):
    L = SSM_CHUNK
    nc = seq // L
    hb = L // SUBLANES
    pad = lambda v: jnp.pad(v.reshape(1, -1), ((0, 0), (0, LANES - v.shape[-1])))
    rowblk = lambda b, c: b * nc + c
    halo_row = lambda b, c: jnp.maximum((b * nc + c) * hb - 1, 0)
    const = lambda b, c: (0, 0)
    bcw = 2 * SSM_GROUPS * SSM_STATE
    return pl.pallas_call(
        _ssd_kernel,
        grid=(bsz, nc),
        in_specs=[pl.BlockSpec((L, SSM_INNER), lambda b, c: (rowblk(b, c), 0)),
                  pl.BlockSpec((L, SSM_INNER), lambda b, c: (rowblk(b, c), 1)),
                  pl.BlockSpec((SUBLANES, SSM_INNER), lambda b, c: (halo_row(b, c), 1)),
                  pl.BlockSpec((L, bcw), lambda b, c: (rowblk(b, c), 4)),
                  pl.BlockSpec((SUBLANES, bcw), lambda b, c: (halo_row(b, c), 4)),
                  pl.BlockSpec((L, LANES), lambda b, c: (rowblk(b, c), 0)),
                  pl.BlockSpec((SSM_CONV_WIDTH, SSM_INNER), const),
                  pl.BlockSpec((1, SSM_INNER), const),
                  pl.BlockSpec((SSM_CONV_WIDTH, bcw), const),
                  pl.BlockSpec((1, bcw), const),
                  pl.BlockSpec((1, LANES), const),
                  pl.BlockSpec((1, LANES), const),
                  pl.BlockSpec((1, SSM_INNER), const),
                  pl.BlockSpec((1, SSM_INNER), const)],
        out_specs=pl.BlockSpec((L, SSM_INNER), lambda b, c: (rowblk(b, c), 0)),
        out_shape=jax.ShapeDtypeStruct((bsz * seq, SSM_INNER), BF16),
        scratch_shapes=[pltpu.VMEM((L + SUBLANES, SSM_INNER), F32),
                        pltpu.VMEM((L + SUBLANES, bcw), F32),
                        pltpu.VMEM((SSM_GROUPS, SSM_STATE, SSM_INNER // SSM_GROUPS), F32)],
        compiler_params=_cparams("parallel", "arbitrary"),
    )(proj, proj, proj, proj, proj, dt_raw,
      conv_w[:, :SSM_INNER], conv_b[:SSM_INNER].reshape(1, -1),
      conv_w[:, SSM_INNER:], conv_b[SSM_INNER:].reshape(1, -1),
      pad(dt_bias), pad(a_log), jnp.repeat(d_skip, SSM_HEADDIM).reshape(1, -1), ssm_norm.reshape(1, -1))


def _conformer_kernel(ua_ref, ug_ref, uah_ref, ugh_ref, w_ref, b_ref, lg_ref, lb_ref, o_ref, ext_ref,
                      shift_ref, conv_ref, *, tiles_per_seq):
    first = (pl.program_id(0) % tiles_per_seq) == 0
    tm = ua_ref.shape[0]
    ext_ref[0:CONF_HALO, :] = jnp.where(first, 0.0, uah_ref[...] * jax.nn.sigmoid(ugh_ref[...]))
    ext_ref[CONF_HALO:, :] = ua_ref[...] * jax.nn.sigmoid(ug_ref[...])
    span = tm + CONF_HALO - SUBLANES
    for s in range(1, SUBLANES):
        shift_ref[s - 1] = ext_ref[s:s + span, :]
    for c0 in range(0, CONF_WIDTH, LANES):
        cs = slice(c0, c0 + LANES)
        acc = jnp.broadcast_to(b_ref[:, cs], (tm, LANES))
        for k in range(CONF_CONV_WIDTH):
            q, s = divmod(CONF_HALO - (CONF_CONV_WIDTH - 1) + k, SUBLANES)
            rows = slice(q * SUBLANES, q * SUBLANES + tm)
            tap = ext_ref[rows, cs] if s == 0 else shift_ref[s - 1, rows, cs]
            acc = acc + w_ref[k:k + 1, cs] * tap
        conv_ref[:, cs] = acc
    c = conv_ref[...]
    mu = jnp.mean(c, axis=-1, keepdims=True)
    d = c - mu
    var = jnp.mean(d * d, axis=-1, keepdims=True)
    y = d * lax.rsqrt(var + NORM_EPS) * lg_ref[...] + lb_ref[...]
    o_ref[...] = _silu(y).astype(o_ref.dtype)


def conformer_mixer(proj, dw_w, dw_b, ln_g, ln_b, seq, *, tm=TM_CONF):
    t = proj.shape[0]
    hb = tm // CONF_HALO
    halo_row = lambda i: jnp.maximum(i * hb - 1, 0)
    const = lambda i: (0, 0)
    return pl.pallas_call(
        functools.partial(_conformer_kernel, tiles_per_seq=seq // tm),
        grid=(t // tm,),
        in_specs=[pl.BlockSpec((tm, CONF_WIDTH), lambda i: (i, 5)),
                  pl.BlockSpec((tm, CONF_WIDTH), lambda i: (i, 6)),
                  pl.BlockSpec((CONF_HALO, CONF_WIDTH), lambda i: (halo_row(i), 5)),
                  pl.BlockSpec((CONF_HALO, CONF_WIDTH), lambda i: (halo_row(i), 6)),
                  pl.BlockSpec((CONF_CONV_WIDTH, CONF_WIDTH), const),
                  pl.BlockSpec((1, CONF_WIDTH), const),
                  pl.BlockSpec((1, CONF_WIDTH), const),
                  pl.BlockSpec((1, CONF_WIDTH), const)],
        out_specs=pl.BlockSpec((tm, CONF_WIDTH), lambda i: (i, 0)),
        out_shape=jax.ShapeDtypeStruct((t, CONF_WIDTH), BF16),
        scratch_shapes=[pltpu.VMEM((tm + CONF_HALO, CONF_WIDTH), F32),
                        pltpu.VMEM((SUBLANES - 1, tm + CONF_HALO - SUBLANES, CONF_WIDTH), F32),
                        pltpu.VMEM((tm, CONF_WIDTH), F32)],
        compiler_params=_cparams("parallel"),
    )(proj, proj, proj, proj, dw_w, dw_b.reshape(1, -1), ln_g.reshape(1, -1), ln_b.reshape(1, -1))


def _shortconv_kernel(b_ref, c_ref, h_ref, ch_ref, hh_ref, w_ref, o_ref, ext_ref, *, tiles_per_seq):
    first = (pl.program_id(0) % tiles_per_seq) == 0
    conv = _dwconv(ext_ref, c_ref[...] * h_ref[...], ch_ref[...] * hh_ref[...], first, w_ref, SC_CONV_WIDTH)
    o_ref[...] = (b_ref[...] * conv).astype(o_ref.dtype)


def shortconv_mixer(proj, conv_w, seq, *, tm=TM_SC):
    t = proj.shape[0]
    hb = tm // SUBLANES
    halo_row = lambda i: jnp.maximum(i * hb - 1, 0)
    return pl.pallas_call(
        functools.partial(_shortconv_kernel, tiles_per_seq=seq // tm),
        grid=(t // tm,),
        in_specs=[pl.BlockSpec((tm, SC_WIDTH), lambda i: (i, 0)),
                  pl.BlockSpec((tm, SC_WIDTH), lambda i: (i, 1)),
                  pl.BlockSpec((tm, SC_WIDTH), lambda i: (i, 2)),
                  pl.BlockSpec((SUBLANES, SC_WIDTH), lambda i: (halo_row(i), 1)),
                  pl.BlockSpec((SUBLANES, SC_WIDTH), lambda i: (halo_row(i), 2)),
                  pl.BlockSpec((SC_CONV_WIDTH, SC_WIDTH), lambda i: (0, 0))],
        out_specs=pl.BlockSpec((tm, SC_WIDTH), lambda i: (i, 0)),
        out_shape=jax.ShapeDtypeStruct((t, SC_WIDTH), BF16),
        scratch_shapes=[pltpu.VMEM((tm + SUBLANES, SC_WIDTH), F32)],
        compiler_params=_cparams("parallel"),
    )(proj, proj, proj, proj, proj, conv_w)


def _rope_table_kernel(pos_ref, inv_ref, cos_ref, sin_ref):
    ang = pos_ref[...].astype(F32) * inv_ref[...]
    sign = jnp.where(lax.broadcasted_iota(jnp.int32, ang.shape, 1) < NSA_HEAD_DIM // 2, -1.0, 1.0)
    cos_ref[...] = jnp.cos(ang)
    sin_ref[...] = jnp.sin(ang) * sign


def rope_tables(positions, *, tm=TM_ROPE):
    t = positions.size
    half = NSA_HEAD_DIM // 2
    inv = ROPE_THETA ** (-jnp.arange(half, dtype=F32) / half)
    inv = jnp.concatenate([inv, inv]).reshape(1, NSA_HEAD_DIM)
    tab = jax.ShapeDtypeStruct((t, NSA_HEAD_DIM), F32)
    return pl.pallas_call(
        _rope_table_kernel,
        grid=(t // tm,),
        in_specs=[pl.BlockSpec((tm, 1), lambda i: (i, 0)), pl.BlockSpec((1, NSA_HEAD_DIM), lambda i: (0, 0))],
        out_specs=[pl.BlockSpec((tm, NSA_HEAD_DIM), lambda i: (i, 0))] * 2,
        out_shape=[tab, tab],
        compiler_params=_cparams("parallel"),
    )(positions.reshape(t, 1), inv)


def _rope(x, cos, sin_signed):
    return x * cos + pltpu.roll(x, NSA_HEAD_DIM // 2, axis=1) * sin_signed


def _kv_prep_kernel(ks_ref, vs_ref, kw_ref, vw_ref, cos_ref, sin_ref, kso_ref, vso_ref, kwo_ref, vwo_ref,
                    *, tiles_per_seq):
    D = NSA_HEAD_DIM
    tm = ks_ref.shape[0]
    cos, sin = cos_ref[...], sin_ref[...]
    row = (pl.program_id(0) % tiles_per_seq) * tm + lax.broadcasted_iota(jnp.int32, (tm, D), 0)
    lane = lax.broadcasted_iota(jnp.int32, (tm, D), 1)
    onehot = jnp.where(row // SLC_BLOCK == lane, 1.0, 0.0).astype(BF16)
    ones_col = jnp.where(lane == 0, 1.0, 0.0).astype(BF16)
    for g in range(NSA_KV_HEADS):
        sl = slice(g * D, (g + 1) * D)
        kso_ref[:, 2 * g * D:(2 * g + 1) * D] = _rope(ks_ref[:, sl], cos, sin).astype(BF16)
        kso_ref[:, (2 * g + 1) * D:(2 * g + 2) * D] = onehot
        kwo_ref[:, sl] = _rope(kw_ref[:, sl], cos, sin).astype(BF16)
        vso_ref[:, 2 * g * D:(2 * g + 1) * D] = vs_ref[:, sl].astype(BF16)
        vso_ref[:, (2 * g + 1) * D:(2 * g + 2) * D] = ones_col
        vwo_ref[:, 2 * g * D:(2 * g + 1) * D] = vw_ref[:, sl].astype(BF16)
        vwo_ref[:, (2 * g + 1) * D:(2 * g + 2) * D] = ones_col


def kv_prep(proj, cos, sin, seq, *, tm=512):
    t = proj.shape[0]
    kv = jax.ShapeDtypeStruct((t, NSA_KV), BF16)
    kv_aug = jax.ShapeDtypeStruct((t, 2 * NSA_KV), BF16)
    spec = pl.BlockSpec((tm, NSA_KV), lambda i: (i, 0))
    spec_aug = pl.BlockSpec((tm, 2 * NSA_KV), lambda i: (i, 0))
    return pl.pallas_call(
        functools.partial(_kv_prep_kernel, tiles_per_seq=seq // tm),
        grid=(t // tm,),
        in_specs=[pl.BlockSpec((tm, NSA_KV), lambda i: (i, 12)),
                  pl.BlockSpec((tm, NSA_KV), lambda i: (i, 13)),
                  pl.BlockSpec((tm, NSA_KV), lambda i: (i, 14)),
                  pl.BlockSpec((tm, NSA_KV), lambda i: (i, 15)),
                  pl.BlockSpec((tm, NSA_HEAD_DIM), lambda i: (i, 0)),
                  pl.BlockSpec((tm, NSA_HEAD_DIM), lambda i: (i, 0))],
        out_specs=[spec_aug, spec_aug, spec, spec_aug],
        out_shape=[kv_aug, kv_aug, kv, kv_aug],
        compiler_params=_cparams("parallel"),
    )(proj, proj, proj, proj, cos, sin)


def _compress_kernel(kt_ref, vt_ref, cos_ref, sin_ref, pe_ref, w1_ref, w2_ref, kc_ref, vc_ref):
    nb = kc_ref.shape[0]
    half = CMP_BLOCK // 2

    def compress(tok_ref, which):
        acc_lo = jnp.zeros((nb, NSA_HEAD_DIM), F32)
        acc_hi = jnp.zeros((nb, NSA_HEAD_DIM), F32)
        for l in range(half):
            r = tok_ref[pl.ds(l, nb, stride=CMP_STRIDE), :]
            lo = (r + pe_ref[which, l:l + 1, :]).astype(BF16)
            hi = (r + pe_ref[which, half + l:half + l + 1, :]).astype(BF16)
            acc_lo = acc_lo + _dot(lo, w1_ref[which, l * NSA_HEAD_DIM:(l + 1) * NSA_HEAD_DIM, :])
            acc_hi = acc_hi + _dot(hi, w1_ref[which, (half + l) * NSA_HEAD_DIM:(half + l + 1) * NSA_HEAD_DIM, :])
        pre = acc_lo + pltpu.roll(acc_hi, nb - 1, axis=0)
        return _dot(_silu(pre).astype(BF16), w2_ref[which])

    cos_e = pltpu.roll(cos_ref[pl.ds(CMP_STRIDE - 1, nb, stride=CMP_STRIDE), :], nb - 1, axis=0)
    sin_e = pltpu.roll(sin_ref[pl.ds(CMP_STRIDE - 1, nb, stride=CMP_STRIDE), :], nb - 1, axis=0)
    kc_ref[...] = _rope(compress(kt_ref, 0), cos_e, sin_e).astype(BF16)
    vc_ref[...] = compress(vt_ref, 1).astype(BF16)


def compress_kv(proj, cos, sin, pe, w1, w2, bsz, seq):
    nb = seq // CMP_STRIDE
    out = jax.ShapeDtypeStruct((bsz, NSA_KV_HEADS, nb, NSA_HEAD_DIM), BF16)
    full = lambda *shape: pl.BlockSpec(shape, lambda b, g: (0,) * len(shape))
    return pl.pallas_call(
        _compress_kernel,
        grid=(bsz, NSA_KV_HEADS),
        in_specs=[pl.BlockSpec((seq, NSA_HEAD_DIM), lambda b, g: (b, 40 + g)),
                  pl.BlockSpec((seq, NSA_HEAD_DIM), lambda b, g: (b, 44 + g)),
                  pl.BlockSpec((seq, NSA_HEAD_DIM), lambda b, g: (b, 0)),
                  pl.BlockSpec((seq, NSA_HEAD_DIM), lambda b, g: (b, 0)),
                  full(2, CMP_BLOCK, NSA_HEAD_DIM),
                  full(2, CMP_BLOCK * NSA_HEAD_DIM, NSA_HEAD_DIM),
                  full(2, NSA_HEAD_DIM, NSA_HEAD_DIM)],
        out_specs=[pl.BlockSpec((None, None, nb, NSA_HEAD_DIM), lambda b, g: (b, g, 0, 0))] * 2,
        out_shape=[out, out],
        compiler_params=_cparams("parallel", "parallel"),
    )(proj, proj, cos, sin, pe, w1, w2)


def _nsa_kernel(q_ref, cos_ref, sin_ref, gate_ref, kc_ref, vc_ref, ks_ref, vs_ref, kw_ref, vw_ref,
                o_ref, qa_ref, impt_ref, sc_ref, mrun_ref, acc_ref):
    G = NSA_GROUP
    D = NSA_HEAD_DIM
    R = G * TQ
    q0 = pl.program_id(2) * TQ
    scale = D ** -0.5
    cos, sin = cos_ref[...], sin_ref[...]
    for e in range(G):
        xq = _rope(q_ref[:, e * D:(e + 1) * D], cos, sin)
        qa_ref[e * TQ:(e + 1) * TQ, 0:D] = (xq * scale).astype(BF16)
    qs = qa_ref[:, 0:D]
    t_col = q0 + lax.broadcasted_iota(jnp.int32, (TQ, 1), 0)
    n_cmp = kc_ref.shape[0]
    n_slc = ks_ref.shape[0] // SLC_BLOCK

    ends = lax.broadcasted_iota(jnp.int32, (1, n_cmp), 1) * CMP_STRIDE + (CMP_BLOCK - 1)
    cvalid = ends <= t_col
    cbias = jnp.where(cvalid, 0.0, NEG_BIG)
    c01 = jnp.where(cvalid, 1.0, 0.0)
    s = _dot_nt(qs, kc_ref[...]).reshape(G, TQ, n_cmp) + cbias[None]
    m = jnp.max(s, axis=-1, keepdims=True)
    p = jnp.exp(s - m) * c01[None]
    l = jnp.sum(p, axis=-1, keepdims=True)
    p = p * (1.0 / jnp.maximum(l, 1e-30))
    o_cmp = _dot(p.reshape(R, n_cmp).astype(BF16), vc_ref[...])

    wk = WINDOW + TQ
    start = pl.multiple_of(jnp.maximum(q0 - WINDOW, 0), TQ)
    d = q0 - start
    rel = lax.broadcasted_iota(jnp.int32, (TQ, wk), 1) - lax.broadcasted_iota(jnp.int32, (TQ, wk), 0)
    wbias = jnp.where((rel <= d) & (rel > d - WINDOW), 0.0, NEG_BIG)
    sw = _dot_nt(qs, kw_ref[pl.ds(start, wk), :]).reshape(G, TQ, wk) + wbias[None]
    mw = jnp.max(sw, axis=-1, keepdims=True)
    pw = jnp.exp(sw - mw).reshape(R, wk)
    accw = _dot(pw.astype(BF16), vw_ref[pl.ds(start, wk), :])
    o_win = accw[:, 0:D] * (1.0 / accw[:, D:D + 1])

    psum = p[0]
    for e in range(1, G):
        psum = psum + p[e]
    n_i = lax.broadcasted_iota(jnp.int32, (n_cmp, LANES), 0)
    j_i = lax.broadcasted_iota(jnp.int32, (n_cmp, LANES), 1)
    overlap = ((n_i * CMP_STRIDE < (j_i + 1) * SLC_BLOCK)
               & (n_i * CMP_STRIDE + CMP_BLOCK - 1 >= j_i * SLC_BLOCK)).astype(BF16)
    imp = _dot_f32_rhs01(psum, overlap)
    j = lax.broadcasted_iota(jnp.int32, (1, LANES), 1)
    cur = t_col // SLC_BLOCK
    forced = (j == 0) | ((j <= cur) & (j > cur - N_LOCAL))
    valid = j * SLC_BLOCK <= t_col
    imp = jnp.where(forced, FORCE, jnp.where(valid, imp, -FORCE))
    impt_ref[...] = imp.T
    nrb = n_slc // SUBLANES
    vals = [impt_ref[rb * SUBLANES:(rb + 1) * SUBLANES, :] for rb in range(nrb)]
    cnt = [jnp.zeros((SUBLANES, TQ), F32) for _ in range(nrb)]
    sub = lax.broadcasted_iota(jnp.int32, (SUBLANES, TQ), 0)
    for k in range(n_slc):
        vk = jnp.broadcast_to(impt_ref[k:k + 1, :], (SUBLANES, TQ))
        for rb in range(nrb):
            if rb * SUBLANES > k:
                beats = vk >= vals[rb]
            elif rb * SUBLANES + SUBLANES - 1 < k:
                beats = vk > vals[rb]
            else:
                beats = (vk > vals[rb]) | ((vk == vals[rb]) & (sub > k - rb * SUBLANES))
            cnt[rb] = cnt[rb] + jnp.where(beats, 1.0, 0.0)
    sel_t = jnp.concatenate([jnp.where(c < N_SELECT, 1.0, 0.0) for c in cnt]
                            + [jnp.zeros((LANES - n_slc, TQ), F32)], axis=0)
    sel = sel_t.T
    sbias = jnp.where((sel > 0.5) & (j * SLC_BLOCK < q0), 0.0, NEG_BIG).astype(BF16)
    for e in range(G):
        qa_ref[e * TQ:(e + 1) * TQ, D:2 * D] = sbias

    lower = (lax.broadcasted_iota(jnp.int32, (TQ, TQ), 1) <= lax.broadcasted_iota(jnp.int32, (TQ, TQ), 0))
    s_own = (_dot_nt(qs, ks_ref[pl.ds(q0, TQ), 0:D]).reshape(G, TQ, TQ)
             + jnp.where(lower, 0.0, NEG_BIG)[None]).reshape(R, TQ)
    mrun_ref[...] = s_own
    n_kt = (q0 + TK - 1) // TK

    def score_body(kt, carry):
        k0 = pl.multiple_of(kt * TK, TK)
        s = _dot_nt(qa_ref[...], ks_ref[pl.ds(k0, TK), :])
        sc_ref[kt] = s
        t = s[:, 0:LANES]
        for c in range(1, TK // LANES):
            t = jnp.maximum(t, s[:, c * LANES:(c + 1) * LANES])
        mrun_ref[...] = jnp.maximum(mrun_ref[...], t)
        return carry

    lax.fori_loop(0, n_kt, score_body, 0)
    m = jnp.max(mrun_ref[...], axis=-1, keepdims=True)
    acc_ref[...] = _dot(jnp.exp(s_own - m).astype(BF16), vs_ref[pl.ds(q0, TQ), :])

    def pv_body(kt, carry):
        k0 = pl.multiple_of(kt * TK, TK)
        p = jnp.exp(sc_ref[kt] - m).astype(BF16)
        acc_ref[...] += _dot(p, vs_ref[pl.ds(k0, TK), :])
        return carry

    lax.fori_loop(0, n_kt, pv_body, 0)
    acc = acc_ref[...]
    o_slc = acc[:, 0:D] * (1.0 / acc[:, D:D + 1])

    gate =jax.nn.sigmoid(gate_ref[...])
    for e in range(G):
        rows = slice(e * TQ, (e + 1) * TQ)
        o = (gate[:, 3 * e:3 * e + 1] * o_cmp[rows] + gate[:, 3 * e + 1:3 * e + 2] * o_slc[rows]
             + gate[:, 3 * e + 2:3 * e + 3] * o_win[rows])
        o_ref[:, e * D:(e + 1) * D] = o.astype(o_ref.dtype)


def nsa_attention(proj, gates, cos, sin, kc, vc, ks, vs, kw, vw, bsz, seq):
    nq = seq // TQ
    qw = NSA_GROUP * NSA_HEAD_DIM
    row = lambda b, g, i: b * nq + i
    kvspec = pl.BlockSpec((seq, NSA_HEAD_DIM), lambda b, g, i: (b, g))
    augspec = pl.BlockSpec((seq, 2 * NSA_HEAD_DIM), lambda b, g, i: (b, g))
    cspec = pl.BlockSpec((None, None, seq // CMP_STRIDE, NSA_HEAD_DIM), lambda b, g, i: (b, g, 0, 0))
    return pl.pallas_call(
        _nsa_kernel,
        grid=(bsz, NSA_KV_HEADS, nq),
        in_specs=[pl.BlockSpec((TQ, qw), lambda b, g, i: (row(b, g, i), 6 + g)),
                  pl.BlockSpec((TQ, NSA_HEAD_DIM), lambda b, g, i: (row(b, g, i), 0)),
                  pl.BlockSpec((TQ, NSA_HEAD_DIM), lambda b, g, i: (row(b, g, i), 0)),
                  pl.BlockSpec((TQ, LANES), lambda b, g, i: (row(b, g, i), g)),
                  cspec, cspec, augspec, augspec, kvspec, augspec],
        out_specs=pl.BlockSpec((TQ, qw), lambda b, g, i: (row(b, g, i), g)),
        out_shape=jax.ShapeDtypeStruct((bsz * seq, NSA_Q), BF16),
        scratch_shapes=[pltpu.VMEM((NSA_GROUP * TQ, 2 * NSA_HEAD_DIM), BF16),
                        pltpu.VMEM((LANES, TQ), F32),
                        pltpu.VMEM((seq // TK, NSA_GROUP * TQ, TK), F32),
                        pltpu.VMEM((NSA_GROUP * TQ, LANES), F32),
                        pltpu.VMEM((NSA_GROUP * TQ, 2 * NSA_HEAD_DIM), F32)],
        compiler_params=_cparams("parallel", "parallel", "parallel"),
    )(proj, cos, sin, gates, kc, vc, ks, vs, kw, vw)


def _even_layer(h, gain, w_in, conv_w, conv_b, dt_bias, a_log, d_skip, ssm_norm,
                dw_w, dw_b, ln_g, ln_b, w_out, i, bsz, seq):
    xbc_end = SSM_INNER + SSM_INNER + 2 * SSM_GROUPS * SSM_STATE
    dt_end = xbc_end + SSM_HEADS
    w_main = jnp.concatenate([w_in[:, :xbc_end], w_in[:, dt_end:]], axis=1).astype(BF16)
    w_dt = jnp.pad(w_in[:, xbc_end:dt_end], ((0, 0), (0, LANES - SSM_HEADS))).astype(BF16)
    proj = norm_matmul(h, gain, w_main)
    dt_raw = norm_matmul(h, gain, w_dt)
    y = ssd_mixer(proj, dt_raw, conv_w, conv_b, dt_bias, a_log, d_skip, ssm_norm, bsz, seq)
    u = conformer_mixer(proj, dw_w, dw_b, ln_g, ln_b, seq)
    return matmul_res([(y, 0, 0, SSM_INNER), (u, 0, SSM_INNER // CONF_WIDTH, CONF_WIDTH)], w_out, i, h)


def _odd_layer(h, gain, cos, sin, w_in, w_gates, sc_w, pe, w1, w2, w_out, i, bsz, seq):
    main = 3 * SC_WIDTH + NSA_Q + 6 * NSA_KV
    per_g = 3 * NSA_GROUP
    wg = w_gates.reshape(D_MODEL, NSA_KV_HEADS, per_g)
    wg = jnp.pad(wg, ((0, 0), (0, 0), (0, LANES - per_g))).reshape(D_MODEL, NSA_KV_HEADS * LANES)
    proj = norm_matmul(h, gain, w_in, layer=i, n=main)
    gates = norm_matmul(h, gain, wg.astype(BF16), tn=NSA_KV_HEADS * LANES)
    y_c = shortconv_mixer(proj, sc_w, seq)
    ks, vs, kw, vw = kv_prep(proj, cos, sin, seq)
    kc, vc = compress_kv(proj, cos, sin, pe, w1, w2, bsz, seq)
    y_d = nsa_attention(proj, gates, cos, sin, kc, vc, ks, vs, kw, vw, bsz, seq)
    return matmul_res([(y_c, 0, 0, SC_WIDTH), (y_d, 0, 1, SC_WIDTH), (y_d, 1, 2, SC_WIDTH)], w_out, i, h)


def _ffn(h, gain, w_up, conv_w, conv_b, w_down, layer, seq):
    a = ffn_up_gate(h, gain, w_up, layer, conv_w, conv_b, seq)
    return matmul_res([(a, 0, 0, D_FF)], w_down, layer, h)


def kernel(x, positions, mix_norm, ffn_norm, final_norm, ab_w_in, ssm_conv_w, ssm_conv_b, ssm_dt_bias, ssm_a_log, ssm_d, ssm_norm, conf_dw_w, conf_dw_b, conf_ln_g, conf_ln_b, ab_w_out, cd_w_in, sc_conv_w, nsa_cmp_pe, nsa_cmp_w1, nsa_cmp_w2, cd_w_out, ffn_w_up, ffn_conv_w, ffn_conv_b, ffn_w_down):
    bsz, seq, d = x.shape
    depth = mix_norm.shape[0]
    h = x.reshape(bsz * seq, d)
    cos, sin = rope_tables(positions)
    ab_w_out_b, cd_w_in_b, cd_w_out_b = ab_w_out.astype(BF16), cd_w_in.astype(BF16), cd_w_out.astype(BF16)
    ffn_w_up_b, ffn_w_down_b = ffn_w_up.astype(BF16), ffn_w_down.astype(BF16)
    cmp_w1_b, cmp_w2_b = nsa_cmp_w1.astype(BF16), nsa_cmp_w2.astype(BF16)
    gate_cols = 3 * SC_WIDTH + NSA_Q + 6 * NSA_KV
    for layer in range(depth):
        i = layer // 2
        if layer % 2 == 0:
            h = _even_layer(h, mix_norm[layer], ab_w_in[i], ssm_conv_w[i], ssm_conv_b[i], ssm_dt_bias[i],
                            ssm_a_log[i], ssm_d[i], ssm_norm[i], conf_dw_w[i], conf_dw_b[i], conf_ln_g[i],
                            conf_ln_b[i], ab_w_out_b, i, bsz, seq)
        else:
            h = _odd_layer(h, mix_norm[layer], cos, sin, cd_w_in_b, cd_w_in[i][:, gate_cols:], sc_conv_w[i],
                           nsa_cmp_pe[i], cmp_w1_b[i], cmp_w2_b[i], cd_w_out_b, i, bsz, seq)
        h = _ffn(h, ffn_norm[layer], ffn_w_up_b, ffn_conv_w[layer], ffn_conv_b[layer], ffn_w_down_b, layer, seq)
    return rmsnorm_rows(h, final_norm).reshape(bsz, seq, d)
```

```python
import functools

import jax
import jax.numpy as jnp
from jax import lax
from jax.experimental import pallas as pl
from jax.experimental.pallas import tpu as pltpu

F32 = jnp.float32
BF16 = jnp.bfloat16

D_MODEL = 2048
NORM_EPS = 1e-6
SSM_INNER = 2048
SSM_HEADDIM = 64
SSM_HEADS = 32
SSM_GROUPS = 4
SSM_STATE = 128
SSM_CONV_WIDTH = 4
SSM_CHUNK = 128
CONF_WIDTH = 1024
CONF_CONV_WIDTH = 31
SC_WIDTH = 1024
SC_CONV_WIDTH = 3
NSA_HEAD_DIM = 128
NSA_HEADS = 16
NSA_KV_HEADS = 4
NSA_GROUP = 4
CMP_BLOCK = 32
CMP_STRIDE = 16
SLC_BLOCK = 64
N_SELECT = 16
N_LOCAL = 2
WINDOW = 512
ROPE_THETA = 10000.0
NSA_Q = 2048
NSA_KV = 512
D_FF = 5632
FFN_CONV_WIDTH = 3
NEG_BIG = -1e30
FORCE = 1e9

V7X_VMEM_BYTES = 64 * 1024 * 1024
VMEM_LIMIT = V7X_VMEM_BYTES - 8 * 1024 * 1024
LANES = 128
SUBLANES = 8

TM_PROJ = 1024
TN_PROJ = 1024
TM_OUT = 512
TN_OUT = 1024
TN_FFN = 512
FFN_HALO = 16
FFN_CHUNKS = 2
TM_CONF = 256
CONF_HALO = 32
TM_SC = 512
TM_ROPE = 1024
TQ = 128
TK = 1024
NSA_TILES = 2


def _cparams(*sem):
    return pltpu.CompilerParams(dimension_semantics=sem, vmem_limit_bytes=VMEM_LIMIT)


def _silu(x):
    return x * jax.nn.sigmoid(x)


def _softplus(x):
    return jnp.maximum(x, 0.0) + jnp.log1p(jnp.exp(-jnp.abs(x)))


def _split3(x):
    hi = x.astype(BF16)
    r1 = x - hi.astype(F32)
    mid = r1.astype(BF16)
    lo = (r1 - mid.astype(F32)).astype(BF16)
    return hi, mid, lo


def _dot(a, b):
    return jnp.dot(a, b, preferred_element_type=F32)


def _dot_nt(a, b):
    return lax.dot_general(a, b, (((1,), (1,)), ((), ())), preferred_element_type=F32)


def _dot_f32_rhs01(x, e01):
    hi, mid, lo = _split3(x)
    return _dot(hi, e01) + _dot(mid, e01) + _dot(lo, e01)


def _dot_f32_lhs01(e01, x):
    hi, mid, lo = _split3(x)
    return _dot(e01, hi) + _dot(e01, mid) + _dot(e01, lo)


def _dwconv(ext_ref, x, halo, first, w_ref, width):
    rows = x.shape[0]
    hrows = halo.shape[0]
    ext_ref[0:hrows, :] = jnp.where(first, 0.0, halo)
    ext_ref[hrows:hrows + rows, :] = x
    acc = None
    for k in range(width):
        off = hrows - (width - 1) + k
        term = w_ref[k:k + 1, :] * ext_ref[off:off + rows, :]
        acc = term if acc is None else acc + term
    return acc


def _norm_matmul_kernel(h_ref, g_ref, w_ref, o_ref, xn_ref):
    @pl.when(pl.program_id(1) == 0)
    def _():
        x = h_ref[...]
        ms = jnp.mean(x * x, axis=-1, keepdims=True)
        xn_ref[...] = (x * lax.rsqrt(ms + NORM_EPS) * g_ref[...]).astype(BF16)

    o_ref[...] = _dot(xn_ref[...], w_ref[...]).astype(o_ref.dtype)


def norm_matmul(h, gain, w, *, layer=None, n=None, tm=TM_PROJ, tn=TN_PROJ):
    t, k = h.shape
    if layer is None:
        n = w.shape[1]
        tn = min(tn, n)
        w_spec = pl.BlockSpec((k, tn), lambda i, j: (0, j))
    else:
        w_spec = pl.BlockSpec((None, k, tn), lambda i, j: (layer, 0, j))
    return pl.pallas_call(
        _norm_matmul_kernel,
        grid=(t // tm, n // tn),
        in_specs=[pl.BlockSpec((tm, k), lambda i, j: (i, 0)),
                  pl.BlockSpec((1, k), lambda i, j: (0, 0)),
                  w_spec],
        out_specs=pl.BlockSpec((tm, tn), lambda i, j: (i, j)),
        out_shape=jax.ShapeDtypeStruct((t, n), F32),
        scratch_shapes=[pltpu.VMEM((tm, k), BF16)],
        compiler_params=_cparams("parallel", "arbitrary"),
    )(h, gain.reshape(1, k), w)


def _matmul_res_kernel(*refs, n_in):
    a_refs, w_refs = refs[:n_in], refs[n_in:2 * n_in]
    r_ref, o_ref = refs[2 * n_in], refs[2 * n_in + 1]
    acc = _dot(a_refs[0][...], w_refs[0][...])
    for a_ref, w_ref in zip(a_refs[1:], w_refs[1:]):
        acc = acc + _dot(a_ref[...], w_ref[...])
    o_ref[...] = r_ref[...] + acc


def matmul_res(terms, w, layer, res, *, tm=TM_OUT, tn=TN_OUT):
    t, n = res.shape
    in_specs = ([pl.BlockSpec((tm, kw), lambda i, j, ca=ca: (i, ca)) for _, ca, _, kw in terms]
                + [pl.BlockSpec((None, kw, tn), lambda i, j, rw=rw: (layer, rw, j)) for _, _, rw, kw in terms]
                + [pl.BlockSpec((tm, tn), lambda i, j: (i, j))])
    return pl.pallas_call(
        functools.partial(_matmul_res_kernel, n_in=len(terms)),
        grid=(t // tm, n // tn),
        in_specs=in_specs,
        out_specs=pl.BlockSpec((tm, tn), lambda i, j: (i, j)),
        out_shape=jax.ShapeDtypeStruct((t, n), F32),
        compiler_params=_cparams("parallel", "parallel"),
    )(*[a for a, _, _, _ in terms], *([w] * len(terms)), res)


def _rmsnorm_kernel(h_ref, g_ref, o_ref):
    x = h_ref[...]
    ms = jnp.mean(x * x, axis=-1, keepdims=True)
    o_ref[...] = x * lax.rsqrt(ms + NORM_EPS) * g_ref[...]


def rmsnorm_rows(h, gain, *, tm=512):
    t, k = h.shape
    return pl.pallas_call(
        _rmsnorm_kernel,
        grid=(t // tm,),
        in_specs=[pl.BlockSpec((tm, k), lambda i: (i, 0)), pl.BlockSpec((1, k), lambda i: (0, 0))],
        out_specs=pl.BlockSpec((tm, k), lambda i: (i, 0)),
        out_shape=jax.ShapeDtypeStruct((t, k), F32),
        compiler_params=_cparams("parallel"),
    )(h, gain.reshape(1, k))


def _ffn_up_kernel(h_ref, hh_ref, gain_ref, wg_ref, wv_ref, cwg_ref, cwv_ref, bg_ref, bv_ref, o_ref,
                   xn_ref, *, tiles_per_seq):
    tm, tn = o_ref.shape

    @pl.when(pl.program_id(1) == 0)
    def _():
        first = (pl.program_id(0) % tiles_per_seq) == 0

        def norm(x):
            ms = jnp.mean(x * x, axis=-1, keepdims=True)
            return x * lax.rsqrt(ms + NORM_EPS) * gain_ref[...]

        xn_ref[0:FFN_HALO, :] = jnp.where(first, 0.0, norm(hh_ref[...])).astype(BF16)
        xn_ref[FFN_HALO:, :] = norm(h_ref[...]).astype(BF16)

    def conv(u, cw_ref, b_ref, cs):
        acc = b_ref[:, cs]
        for k in range(FFN_CONV_WIDTH):
            off = FFN_HALO - (FFN_CONV_WIDTH - 1) + k
            acc = acc + cw_ref[k:k + 1, cs] * u[off:off + tm, :]
        return acc

    cw = tn // FFN_CHUNKS
    for c in range(FFN_CHUNKS):
        cs = slice(c * cw, (c + 1) * cw)
        ug = _dot(xn_ref[...], wg_ref[:, cs])
        uv = _dot(xn_ref[...], wv_ref[:, cs])
        o_ref[:, cs] = (_silu(conv(ug, cwg_ref, bg_ref, cs)) * conv(uv, cwv_ref, bv_ref, cs)).astype(o_ref.dtype)


def ffn_up_gate(h, gain, w_up, layer, conv_w, conv_b, seq, *, tm=TM_PROJ, tn=TN_FFN):
    t, k = h.shape
    nf = D_FF // tn
    hb = tm // FFN_HALO
    return pl.pallas_call(
        functools.partial(_ffn_up_kernel, tiles_per_seq=seq // tm),
        grid=(t // tm, nf),
        in_specs=[pl.BlockSpec((tm, k), lambda i, j: (i, 0)),
                  pl.BlockSpec((FFN_HALO, k), lambda i, j: (jnp.maximum(i * hb - 1, 0), 0)),
                  pl.BlockSpec((1, k), lambda i, j: (0, 0)),
                  pl.BlockSpec((None, k, tn), lambda i, j: (layer, 0, j)),
                  pl.BlockSpec((None, k, tn), lambda i, j: (layer, 0, j + nf)),
                  pl.BlockSpec((FFN_CONV_WIDTH, tn), lambda i, j: (0, j)),
                  pl.BlockSpec((FFN_CONV_WIDTH, tn), lambda i, j: (0, j + nf)),
                  pl.BlockSpec((1, tn), lambda i, j: (0, j)),
                  pl.BlockSpec((1, tn), lambda i, j: (0, j + nf))],
        out_specs=pl.BlockSpec((tm, tn), lambda i, j: (i, j)),
        out_shape=jax.ShapeDtypeStruct((t, D_FF), BF16),
        scratch_shapes=[pltpu.VMEM((tm + FFN_HALO, k), BF16)],
        compiler_params=_cparams("parallel", "arbitrary"),
    )(h, h, gain.reshape(1, k), w_up, w_up, conv_w, conv_w, conv_b.reshape(1, -1), conv_b.reshape(1, -1))


def _ssd_kernel(z_ref, xs_ref, xsh_ref, bc_ref, bch_ref, dt_ref, cwx_ref, cbx_ref, cwb_ref, cbb_ref,
                dtb_ref, alog_ref, dsk_ref, gn_ref, y_ref, extx_ref, extb_ref, state_ref):
    L = SSM_CHUNK
    GW = SSM_INNER // SSM_GROUPS
    first = pl.program_id(1) == 0

    @pl.when(first)
    def _():
        state_ref[...] = jnp.zeros_like(state_ref)

    xs = _silu(_dwconv(extx_ref, xs_ref[...], xsh_ref[...], first, cwx_ref, SSM_CONV_WIDTH) + cbx_ref[...])
    bc = _silu(_dwconv(extb_ref, bc_ref[...], bch_ref[...], first, cwb_ref, SSM_CONV_WIDTH) + cbb_ref[...])

    dt = _softplus(dt_ref[...] + dtb_ref[...])
    a = -jnp.exp(alog_ref[...])
    la = dt * a
    row = lax.broadcasted_iota(jnp.int32, (L, L), 0)
    col = lax.broadcasted_iota(jnp.int32, (L, L), 1)
    causal = col <= row
    cum = _dot_f32_lhs01(causal.astype(BF16), la)
    cum_t = cum.T
    clast = cum[L - 1:L, :]

    e_head = (lax.broadcasted_iota(jnp.int32, (LANES, SSM_INNER), 1) // SSM_HEADDIM
              == lax.broadcasted_iota(jnp.int32, (LANES, SSM_INNER), 0)).astype(BF16)
    x = xs * _dot_f32_rhs01(dt, e_head)
    ecum_x = _dot_f32_rhs01(jnp.exp(cum), e_head)
    dte_x = _dot_f32_rhs01(jnp.exp(clast - cum), e_head)
    cdec_x = _dot_f32_rhs01(jnp.broadcast_to(jnp.exp(clast), (SUBLANES, LANES)), e_head)[0:1, :]
    xb = x.astype(BF16)
    xdte = (x * dte_x).astype(BF16)
    lo_half = lax.broadcasted_iota(jnp.int32, (L, LANES), 1) < SSM_HEADDIM

    for g in range(SSM_GROUPS):
        bg = bc[:, g * SSM_STATE:(g + 1) * SSM_STATE]
        cg = bc[:, (SSM_GROUPS + g) * SSM_STATE:(SSM_GROUPS + g + 1) * SSM_STATE].astype(BF16)
        cb = _dot_nt(cg, bg.astype(BF16))
        hg = state_ref[g]
        y_off = _dot(cg, hg.astype(BF16)) * ecum_x[:, g * GW:(g + 1) * GW]
        pieces = []
        for pr in range(GW // LANES):
            h0 = g * (GW // SSM_HEADDIM) + 2 * pr
            xp = xb[:, h0 * SSM_HEADDIM:h0 * SSM_HEADDIM + LANES]
            ypair = None
            for s in range(2):
                h = h0 + s
                diff = cum[:, h:h + 1] - cum_t[h:h + 1, :]
                decay = jnp.exp(jnp.where(causal, diff, -jnp.inf))
                m = (cb * decay).astype(BF16)
                keep = lo_half if s == 0 else jnp.logical_not(lo_half)
                yh = _dot(m, jnp.where(keep, xp, jnp.zeros_like(xp)))
                ypair = yh if ypair is None else ypair + yh
            pieces.append(ypair)
        y_diag = jnp.concatenate(pieces, axis=-1)
        st = _dot(bg.T.astype(BF16), xdte[:, g * GW:(g + 1) * GW])
        state_ref[g] = hg * cdec_x[:, g * GW:(g + 1) * GW] + st

        yg = y_diag + y_off + dsk_ref[:, g * GW:(g + 1) * GW] * xs[:, g * GW:(g + 1) * GW]
        yg = yg * _silu(z_ref[:, g * GW:(g + 1) * GW])
        ms = jnp.mean(yg * yg, axis=-1, keepdims=True)
        yg = yg * lax.rsqrt(ms + NORM_EPS) * gn_ref[:, g * GW:(g + 1) * GW]
        y_ref[:, g * GW:(g + 1) * GW] = yg.astype(y_ref.dtype)


def ssd_mixer(proj, dt_raw, conv_w, conv_b, dt_bias, a_log, d_skip, ssm_norm, bsz, seq):
    L = SSM_CHUNK
    nc = seq // L
    hb = L // SUBLANES
    pad = lambda v: jnp.pad(v.reshape(1, -1), ((0, 0), (0, LANES - v.shape[-1])))
    rowblk = lambda b, c: b * nc + c
    halo_row = lambda b, c: jnp.maximum((b * nc + c) * hb - 1, 0)
    const = lambda b, c: (0, 0)
    bcw = 2 * SSM_GROUPS * SSM_STATE
    return pl.pallas_call(
        _ssd_kernel,
        grid=(bsz, nc),
        in_specs=[pl.BlockSpec((L, SSM_INNER), lambda b, c: (rowblk(b, c), 0)),
                  pl.BlockSpec((L, SSM_INNER), lambda b, c: (rowblk(b, c), 1)),
                  pl.BlockSpec((SUBLANES, SSM_INNER), lambda b, c: (halo_row(b, c), 1)),
                  pl.BlockSpec((L, bcw), lambda b, c: (rowblk(b, c), 4)),
                  pl.BlockSpec((SUBLANES, bcw), lambda b, c: (halo_row(b, c), 4)),
                  pl.BlockSpec((L, LANES), lambda b, c: (rowblk(b, c), 0)),
                  pl.BlockSpec((SSM_CONV_WIDTH, SSM_INNER), const),
                  pl.BlockSpec((1, SSM_INNER), const),
                  pl.BlockSpec((SSM_CONV_WIDTH, bcw), const),
                  pl.BlockSpec((1, bcw), const),
                  pl.BlockSpec((1, LANES), const),
                  pl.BlockSpec((1, LANES), const),
                  pl.BlockSpec((1, SSM_INNER), const),
                  pl.BlockSpec((1, SSM_INNER), const)],
        out_specs=pl.BlockSpec((L, SSM_INNER), lambda b, c: (rowblk(b, c), 0)),
        out_shape=jax.ShapeDtypeStruct((bsz * seq, SSM_INNER), BF16),
        scratch_shapes=[pltpu.VMEM((L + SUBLANES, SSM_INNER), F32),
                        pltpu.VMEM((L + SUBLANES, bcw), F32),
                        pltpu.VMEM((SSM_GROUPS, SSM_STATE, SSM_INNER // SSM_GROUPS), F32)],
        compiler_params=_cparams("parallel", "arbitrary"),
    )(proj, proj, proj, proj, proj, dt_raw,
      conv_w[:, :SSM_INNER], conv_b[:SSM_INNER].reshape(1, -1),
      conv_w[:, SSM_INNER:], conv_b[SSM_INNER:].reshape(1, -1),
      pad(dt_bias), pad(a_log), jnp.repeat(d_skip, SSM_HEADDIM).reshape(1, -1), ssm_norm.reshape(1, -1))


def _conformer_kernel(ua_ref, ug_ref, uah_ref, ugh_ref, w_ref, b_ref, lg_ref, lb_ref, o_ref, ext_ref,
                      shift_ref, conv_ref, *, tiles_per_seq):
    first = (pl.program_id(0) % tiles_per_seq) == 0
    tm = ua_ref.shape[0]
    ext_ref[0:CONF_HALO, :] = jnp.where(first, 0.0, uah_ref[...] * jax.nn.sigmoid(ugh_ref[...]))
    ext_ref[CONF_HALO:, :] = ua_ref[...] * jax.nn.sigmoid(ug_ref[...])
    span = tm + CONF_HALO - SUBLANES
    for s in range(1, SUBLANES):
        shift_ref[s - 1] = ext_ref[s:s + span, :]
    for c0 in range(0, CONF_WIDTH, LANES):
        cs = slice(c0, c0 + LANES)
        acc = jnp.broadcast_to(b_ref[:, cs], (tm, LANES))
        for k in range(CONF_CONV_WIDTH):
            q, s = divmod(CONF_HALO - (CONF_CONV_WIDTH - 1) + k, SUBLANES)
            rows = slice(q * SUBLANES, q * SUBLANES + tm)
            tap = ext_ref[rows, cs] if s == 0 else shift_ref[s - 1, rows, cs]
            acc = acc + w_ref[k:k + 1, cs] * tap
        conv_ref[:, cs] = acc
    c = conv_ref[...]
    mu = jnp.mean(c, axis=-1, keepdims=True)
    d = c - mu
    var = jnp.mean(d * d, axis=-1, keepdims=True)
    y = d * lax.rsqrt(var + NORM_EPS) * lg_ref[...] + lb_ref[...]
    o_ref[...] = _silu(y).astype(o_ref.dtype)


def conformer_mixer(proj, dw_w, dw_b, ln_g, ln_b, seq, *, tm=TM_CONF):
    t = proj.shape[0]
    hb = tm // CONF_HALO
    halo_row = lambda i: jnp.maximum(i * hb - 1, 0)
    const = lambda i: (0, 0)
    return pl.pallas_call(
        functools.partial(_conformer_kernel, tiles_per_seq=seq // tm),
        grid=(t // tm,),
        in_specs=[pl.BlockSpec((tm, CONF_WIDTH), lambda i: (i, 5)),
                  pl.BlockSpec((tm, CONF_WIDTH), lambda i: (i, 6)),
                  pl.BlockSpec((CONF_HALO, CONF_WIDTH), lambda i: (halo_row(i), 5)),
                  pl.BlockSpec((CONF_HALO, CONF_WIDTH), lambda i: (halo_row(i), 6)),
                  pl.BlockSpec((CONF_CONV_WIDTH, CONF_WIDTH), const),
                  pl.BlockSpec((1, CONF_WIDTH), const),
                  pl.BlockSpec((1, CONF_WIDTH), const),
                  pl.BlockSpec((1, CONF_WIDTH), const)],
        out_specs=pl.BlockSpec((tm, CONF_WIDTH), lambda i: (i, 0)),
        out_shape=jax.ShapeDtypeStruct((t, CONF_WIDTH), BF16),
        scratch_shapes=[pltpu.VMEM((tm + CONF_HALO, CONF_WIDTH), F32),
                        pltpu.VMEM((SUBLANES - 1, tm + CONF_HALO - SUBLANES, CONF_WIDTH), F32),
                        pltpu.VMEM((tm, CONF_WIDTH), F32)],
        compiler_params=_cparams("parallel"),
    )(proj, proj, proj, proj, dw_w, dw_b.reshape(1, -1), ln_g.reshape(1, -1), ln_b.reshape(1, -1))


def _shortconv_kernel(b_ref, c_ref, h_ref, ch_ref, hh_ref, w_ref, o_ref, ext_ref, *, tiles_per_seq):
    first = (pl.program_id(0) % tiles_per_seq) == 0
    conv = _dwconv(ext_ref, c_ref[...] * h_ref[...], ch_ref[...] * hh_ref[...], first, w_ref, SC_CONV_WIDTH)
    o_ref[...] = (b_ref[...] * conv).astype(o_ref.dtype)


def shortconv_mixer(proj, conv_w, seq, *, tm=TM_SC):
    t = proj.shape[0]
    hb = tm // SUBLANES
    halo_row = lambda i: jnp.maximum(i * hb - 1, 0)
    return pl.pallas_call(
        functools.partial(_shortconv_kernel, tiles_per_seq=seq // tm),
        grid=(t // tm,),
        in_specs=[pl.BlockSpec((tm, SC_WIDTH), lambda i: (i, 0)),
                  pl.BlockSpec((tm, SC_WIDTH), lambda i: (i, 1)),
                  pl.BlockSpec((tm, SC_WIDTH), lambda i: (i, 2)),
                  pl.BlockSpec((SUBLANES, SC_WIDTH), lambda i: (halo_row(i), 1)),
                  pl.BlockSpec((SUBLANES, SC_WIDTH), lambda i: (halo_row(i), 2)),
                  pl.BlockSpec((SC_CONV_WIDTH, SC_WIDTH), lambda i: (0, 0))],
        out_specs=pl.BlockSpec((tm, SC_WIDTH), lambda i: (i, 0)),
        out_shape=jax.ShapeDtypeStruct((t, SC_WIDTH), BF16),
        scratch_shapes=[pltpu.VMEM((tm + SUBLANES, SC_WIDTH), F32)],
        compiler_params=_cparams("parallel"),
    )(proj, proj, proj, proj, proj, conv_w)


def _rope_table_kernel(pos_ref, inv_ref, cos_ref, sin_ref):
    ang = pos_ref[...].astype(F32) * inv_ref[...]
    sign = jnp.where(lax.broadcasted_iota(jnp.int32, ang.shape, 1) < NSA_HEAD_DIM // 2, -1.0, 1.0)
    cos_ref[...] = jnp.cos(ang)
    sin_ref[...] = jnp.sin(ang) * sign


def rope_tables(positions, *, tm=TM_ROPE):
    t = positions.size
    half = NSA_HEAD_DIM // 2
    inv = ROPE_THETA ** (-jnp.arange(half, dtype=F32) / half)
    inv = jnp.concatenate([inv, inv]).reshape(1, NSA_HEAD_DIM)
    tab = jax.ShapeDtypeStruct((t, NSA_HEAD_DIM), F32)
    return pl.pallas_call(
        _rope_table_kernel,
        grid=(t // tm,),
        in_specs=[pl.BlockSpec((tm, 1), lambda i: (i, 0)), pl.BlockSpec((1, NSA_HEAD_DIM), lambda i: (0, 0))],
        out_specs=[pl.BlockSpec((tm, NSA_HEAD_DIM), lambda i: (i, 0))] * 2,
        out_shape=[tab, tab],
        compiler_params=_cparams("parallel"),
    )(positions.reshape(t, 1), inv)


def _rope(x, cos, sin_signed):
    return x * cos + pltpu.roll(x, NSA_HEAD_DIM // 2, axis=1) * sin_signed


def _kv_prep_kernel(ks_ref, vs_ref, kw_ref, vw_ref, cos_ref, sin_ref, kso_ref, vso_ref, kwo_ref, vwo_ref,
                    *, tiles_per_seq):
    D = NSA_HEAD_DIM
    tm = ks_ref.shape[0]
    cos, sin = cos_ref[...], sin_ref[...]
    row = (pl.program_id(0) % tiles_per_seq) * tm + lax.broadcasted_iota(jnp.int32, (tm, D), 0)
    lane = lax.broadcasted_iota(jnp.int32, (tm, D), 1)
    onehot = jnp.where(row // SLC_BLOCK == lane, 1.0, 0.0).astype(BF16)
    ones_col = jnp.where(lane == 0, 1.0, 0.0).astype(BF16)
    for g in range(NSA_KV_HEADS):
        sl = slice(g * D, (g + 1) * D)
        kso_ref[:, 2 * g * D:(2 * g + 1) * D] = _rope(ks_ref[:, sl], cos, sin).astype(BF16)
        kso_ref[:, (2 * g + 1) * D:(2 * g + 2) * D] = onehot
        kwo_ref[:, sl] = _rope(kw_ref[:, sl], cos, sin).astype(BF16)
        vso_ref[:, 2 * g * D:(2 * g + 1) * D] = vs_ref[:, sl].astype(BF16)
        vso_ref[:, (2 * g + 1) * D:(2 * g + 2) * D] = ones_col
        vwo_ref[:, 2 * g * D:(2 * g + 1) * D] = vw_ref[:, sl].astype(BF16)
        vwo_ref[:, (2 * g + 1) * D:(2 * g + 2) * D] = ones_col


def kv_prep(proj, cos, sin, seq, *, tm=512):
    t = proj.shape[0]
    kv = jax.ShapeDtypeStruct((t, NSA_KV), BF16)
    kv_aug = jax.ShapeDtypeStruct((t, 2 * NSA_KV), BF16)
    spec = pl.BlockSpec((tm, NSA_KV), lambda i: (i, 0))
    spec_aug = pl.BlockSpec((tm, 2 * NSA_KV), lambda i: (i, 0))
    return pl.pallas_call(
        functools.partial(_kv_prep_kernel, tiles_per_seq=seq // tm),
        grid=(t // tm,),
        in_specs=[pl.BlockSpec((tm, NSA_KV), lambda i: (i, 12)),
                  pl.BlockSpec((tm, NSA_KV), lambda i: (i, 13)),
                  pl.BlockSpec((tm, NSA_KV), lambda i: (i, 14)),
                  pl.BlockSpec((tm, NSA_KV), lambda i: (i, 15)),
                  pl.BlockSpec((tm, NSA_HEAD_DIM), lambda i: (i, 0)),
                  pl.BlockSpec((tm, NSA_HEAD_DIM), lambda i: (i, 0))],
        out_specs=[spec_aug, spec_aug, spec, spec_aug],
        out_shape=[kv_aug, kv_aug, kv, kv_aug],
        compiler_params=_cparams("parallel"),
    )(proj, proj, proj, proj, cos, sin)


def _compress_kernel(kt_ref, vt_ref, cos_ref, sin_ref, pe_ref, w1_ref, w2_ref, kc_ref, vc_ref):
    nb = kc_ref.shape[0]
    half = CMP_BLOCK // 2

    def compress(tok_ref, which):
        acc_lo = jnp.zeros((nb, NSA_HEAD_DIM), F32)
        acc_hi = jnp.zeros((nb, NSA_HEAD_DIM), F32)
        for l in range(half):
            r = tok_ref[pl.ds(l, nb, stride=CMP_STRIDE), :]
            lo = (r + pe_ref[which, l:l + 1, :]).astype(BF16)
            hi = (r + pe_ref[which, half + l:half + l + 1, :]).astype(BF16)
            acc_lo = acc_lo + _dot(lo, w1_ref[which, l * NSA_HEAD_DIM:(l + 1) * NSA_HEAD_DIM, :])
            acc_hi = acc_hi + _dot(hi, w1_ref[which, (half + l) * NSA_HEAD_DIM:(half + l + 1) * NSA_HEAD_DIM, :])
        pre = acc_lo + pltpu.roll(acc_hi, nb - 1, axis=0)
        return _dot(_silu(pre).astype(BF16), w2_ref[which])

    cos_e = pltpu.roll(cos_ref[pl.ds(CMP_STRIDE - 1, nb, stride=CMP_STRIDE), :], nb - 1, axis=0)
    sin_e = pltpu.roll(sin_ref[pl.ds(CMP_STRIDE - 1, nb, stride=CMP_STRIDE), :], nb - 1, axis=0)
    kc_ref[...] = _rope(compress(kt_ref, 0), cos_e, sin_e).astype(BF16)
    vc_ref[...] = compress(vt_ref, 1).astype(BF16)


def compress_kv(proj, cos, sin, pe, w1, w2, bsz, seq):
    nb = seq // CMP_STRIDE
    out = jax.ShapeDtypeStruct((bsz, NSA_KV_HEADS, nb, NSA_HEAD_DIM), BF16)
    full = lambda *shape: pl.BlockSpec(shape, lambda b, g: (0,) * len(shape))
    return pl.pallas_call(
        _compress_kernel,
        grid=(bsz, NSA_KV_HEADS),
        in_specs=[pl.BlockSpec((seq, NSA_HEAD_DIM), lambda b, g: (b, 40 + g)),
                  pl.BlockSpec((seq, NSA_HEAD_DIM), lambda b, g: (b, 44 + g)),
                  pl.BlockSpec((seq, NSA_HEAD_DIM), lambda b, g: (b, 0)),
                  pl.BlockSpec((seq, NSA_HEAD_DIM), lambda b, g: (b, 0)),
                  full(2, CMP_BLOCK, NSA_HEAD_DIM),
                  full(2, CMP_BLOCK * NSA_HEAD_DIM, NSA_HEAD_DIM),
                  full(2, NSA_HEAD_DIM, NSA_HEAD_DIM)],
        out_specs=[pl.BlockSpec((None, None, nb, NSA_HEAD_DIM), lambda b, g: (b, g, 0, 0))] * 2,
        out_shape=[out, out],
        compiler_params=_cparams("parallel", "parallel"),
    )(proj, proj, cos, sin, pe, w1, w2)


def _nsa_kernel(q_ref, cos_ref, sin_ref, gate_ref, kc_ref, vc_ref, ks_ref, vs_ref, kw_ref, vw_ref,
                o_ref, qa_ref, impt_ref, sc_ref, mrun_ref, acc_ref):
    G = NSA_GROUP
    D = NSA_HEAD_DIM
    R = G * TQ
    base = pl.program_id(2) * (NSA_TILES * TQ)
    scale = D ** -0.5
    n_cmp = kc_ref.shape[0]
    n_slc = ks_ref.shape[0] // SLC_BLOCK
    wk = WINDOW + TQ
    j = lax.broadcasted_iota(jnp.int32, (1, LANES), 1)
    ends = lax.broadcasted_iota(jnp.int32, (1, n_cmp), 1) * CMP_STRIDE + (CMP_BLOCK - 1)
    n_i = lax.broadcasted_iota(jnp.int32, (n_cmp, LANES), 0)
    j_i = lax.broadcasted_iota(jnp.int32, (n_cmp, LANES), 1)
    overlap = ((n_i * CMP_STRIDE < (j_i + 1) * SLC_BLOCK)
               & (n_i * CMP_STRIDE + CMP_BLOCK - 1 >= j_i * SLC_BLOCK)).astype(BF16)
    rel = lax.broadcasted_iota(jnp.int32, (TQ, wk), 1) - lax.broadcasted_iota(jnp.int32, (TQ, wk), 0)
    lower = (lax.broadcasted_iota(jnp.int32, (TQ, TQ), 1) <= lax.broadcasted_iota(jnp.int32, (TQ, TQ), 0))
    own_bias = jnp.where(lower, 0.0, NEG_BIG)
    sub = lax.broadcasted_iota(jnp.int32, (SUBLANES, TQ), 0)

    o_cmp, o_win, s_own = [], [], []
    for u in range(NSA_TILES):
        rows = slice(u * TQ, (u + 1) * TQ)
        q0 = base + u * TQ
        cos, sin = cos_ref[rows, :], sin_ref[rows, :]
        for e in range(G):
            xq = _rope(q_ref[rows, e * D:(e + 1) * D], cos, sin)
            qa_ref[u, e * TQ:(e + 1) * TQ, 0:D] = (xq * scale).astype(BF16)
        qs = qa_ref[u, :, 0:D]
        t_col = q0 + lax.broadcasted_iota(jnp.int32, (TQ, 1), 0)

        cvalid = ends <= t_col
        cbias = jnp.where(cvalid, 0.0, NEG_BIG)
        c01 = jnp.where(cvalid, 1.0, 0.0)
        s = _dot_nt(qs, kc_ref[...]).reshape(G, TQ, n_cmp) + cbias[None]
        m = jnp.max(s, axis=-1, keepdims=True)
        p = jnp.exp(s - m) * c01[None]
        l = jnp.sum(p, axis=-1, keepdims=True)
        p = p * (1.0 / jnp.maximum(l, 1e-30))
        o_cmp.append(_dot(p.reshape(R, n_cmp).astype(BF16), vc_ref[...]))

        start = pl.multiple_of(jnp.maximum(q0 - WINDOW, 0), TQ)
        d = q0 - start
        wbias = jnp.where((rel <= d) & (rel > d - WINDOW), 0.0, NEG_BIG)
        sw = _dot_nt(qs, kw_ref[pl.ds(start, wk), :]).reshape(G, TQ, wk) + wbias[None]
        mw = jnp.max(sw, axis=-1, keepdims=True)
        pw = jnp.exp(sw - mw).reshape(R, wk)
        accw = _dot(pw.astype(BF16), vw_ref[pl.ds(start, wk), :])
        o_win.append(accw[:, 0:D] * (1.0 / accw[:, D:D + 1]))

        psum = p[0]
        for e in range(1, G):
            psum = psum + p[e]
        imp = _dot_f32_rhs01(psum, overlap)
        cur = t_col // SLC_BLOCK
        forced = (j == 0) | ((j <= cur) & (j > cur - N_LOCAL))
        valid = j * SLC_BLOCK <= t_col
        imp = jnp.where(forced, FORCE, jnp.where(valid, imp, -FORCE))
        impt_ref[u] = imp.T
        nrb = n_slc // SUBLANES
        vals = [impt_ref[u, rb * SUBLANES:(rb + 1) * SUBLANES, :] for rb in range(nrb)]
        cnt = [jnp.zeros((SUBLANES, TQ), F32) for _ in range(nrb)]
        for k in range(n_slc):
            vk = jnp.broadcast_to(impt_ref[u, k:k + 1, :], (SUBLANES, TQ))
            for rb in range(nrb):
                if rb * SUBLANES > k:
                    beats = vk >= vals[rb]
                elif rb * SUBLANES + SUBLANES - 1 < k:
                    beats = vk > vals[rb]
                else:
                    beats = (vk > vals[rb]) | ((vk == vals[rb]) & (sub > k - rb * SUBLANES))
                cnt[rb] = cnt[rb] + jnp.where(beats, 1.0, 0.0)
        sel_t = jnp.concatenate([jnp.where(c < N_SELECT, 1.0, 0.0) for c in cnt]
                                + [jnp.zeros((LANES - n_slc, TQ), F32)], axis=0)
        sel = sel_t.T
        sbias = jnp.where((sel > 0.5) & (j * SLC_BLOCK < q0), 0.0, NEG_BIG).astype(BF16)
        for e in range(G):
            qa_ref[u, e * TQ:(e + 1) * TQ, D:2 * D] = sbias

        s_own.append((_dot_nt(qs, ks_ref[pl.ds(q0, TQ), 0:D]).reshape(G, TQ, TQ)
                      + own_bias[None]).reshape(R, TQ))
        mrun_ref[u] = s_own[u]

    n_kt = (base + (NSA_TILES - 1) * TQ + TK - 1) // TK

    def score_body(kt, carry):
        k0 = pl.multiple_of(kt * TK, TK)
        for u in range(NSA_TILES):
            s = _dot_nt(qa_ref[u], ks_ref[pl.ds(k0, TK), :])
            sc_ref[u, kt] = s
            t = s[:, 0:LANES]
            for c in range(1, TK // LANES):
                t = jnp.maximum(t, s[:, c * LANES:(c + 1) * LANES])
            mrun_ref[u] = jnp.maximum(mrun_ref[u], t)
        return carry

    lax.fori_loop(0, n_kt, score_body, 0)
    m_row = []
    for u in range(NSA_TILES):
        m_row.append(jnp.max(mrun_ref[u], axis=-1, keepdims=True))
        acc_ref[u] = _dot(jnp.exp(s_own[u] - m_row[u]).astype(BF16),
                          vs_ref[pl.ds(base + u * TQ, TQ), :])

    def pv_body(kt, carry):
        k0 = pl.multiple_of(kt * TK, TK)
        for u in range(NSA_TILES):
            p = jnp.exp(sc_ref[u, kt] - m_row[u]).astype(BF16)
            acc_ref[u] += _dot(p, vs_ref[pl.ds(k0, TK), :])
        return carry

    lax.fori_loop(0, n_kt, pv_body, 0)

    for u in range(NSA_TILES):
        rows = slice(u * TQ, (u + 1) * TQ)
        acc = acc_ref[u]
        o_slc = acc[:, 0:D] * (1.0 / acc[:, D:D + 1])
        gate = jax.nn.sigmoid(gate_ref[rows, :])
        for e in range(G):
            hr = slice(e * TQ, (e + 1) * TQ)
            o = (gate[:, 3 * e:3 * e + 1] * o_cmp[u][hr] + gate[:, 3 * e + 1:3 * e + 2] * o_slc[hr]
                 + gate[:, 3 * e + 2:3 * e + 3] * o_win[u][hr])
            o_ref[rows, e * D:(e + 1) * D] = o.astype(o_ref.dtype)


def nsa_attention(proj, gates, cos, sin, kc, vc, ks, vs, kw, vw, bsz, seq):
    tq = NSA_TILES * TQ
    nq = seq // tq
    qw = NSA_GROUP * NSA_HEAD_DIM
    rows = NSA_GROUP * TQ
    row = lambda b, g, i: b * nq + i
    kvspec = pl.BlockSpec((seq, NSA_HEAD_DIM), lambda b, g, i: (b, g))
    augspec = pl.BlockSpec((seq, 2 * NSA_HEAD_DIM), lambda b, g, i: (b, g))
    cspec = pl.BlockSpec((None, None, seq // CMP_STRIDE, NSA_HEAD_DIM), lambda b, g, i: (b, g, 0, 0))
    return pl.pallas_call(
        _nsa_kernel,
        grid=(bsz, NSA_KV_HEADS, nq),
        in_specs=[pl.BlockSpec((tq, qw), lambda b, g, i: (row(b, g, i), 6 + g)),
                  pl.BlockSpec((tq, NSA_HEAD_DIM), lambda b, g, i: (row(b, g, i), 0)),
                  pl.BlockSpec((tq, NSA_HEAD_DIM), lambda b, g, i: (row(b, g, i), 0)),
                  pl.BlockSpec((tq, LANES), lambda b, g, i: (row(b, g, i), g)),
                  cspec, cspec, augspec, augspec, kvspec, augspec],
        out_specs=pl.BlockSpec((tq, qw), lambda b, g, i: (row(b, g, i), g)),
        out_shape=jax.ShapeDtypeStruct((bsz * seq, NSA_Q), BF16),
        scratch_shapes=[pltpu.VMEM((NSA_TILES, rows, 2 * NSA_HEAD_DIM), BF16),
                        pltpu.VMEM((NSA_TILES, LANES, TQ), F32),
                        pltpu.VMEM((NSA_TILES, seq // TK, rows, TK), F32),
                        pltpu.VMEM((NSA_TILES, rows, LANES), F32),
                        pltpu.VMEM((NSA_TILES, rows, 2 * NSA_HEAD_DIM), F32)],
        compiler_params=_cparams("parallel", "parallel", "parallel"),
    )(proj, cos, sin, gates, kc, vc, ks, vs, kw, vw)


def _even_layer(h, gain, w_in, conv_w, conv_b, dt_bias, a_log, d_skip, ssm_norm,
                dw_w, dw_b, ln_g, ln_b, w_out, i, bsz, seq):
    xbc_end = SSM_INNER + SSM_INNER + 2 * SSM_GROUPS * SSM_STATE
    dt_end = xbc_end + SSM_HEADS
    w_main = jnp.concatenate([w_in[:, :xbc_end], w_in[:, dt_end:]], axis=1).astype(BF16)
    w_dt = jnp.pad(w_in[:, xbc_end:dt_end], ((0, 0), (0, LANES - SSM_HEADS))).astype(BF16)
    proj = norm_matmul(h, gain, w_main)
    dt_raw = norm_matmul(h, gain, w_dt)
    y = ssd_mixer(proj, dt_raw, conv_w, conv_b, dt_bias, a_log, d_skip, ssm_norm, bsz, seq)
    u = conformer_mixer(proj, dw_w, dw_b, ln_g, ln_b, seq)
    return matmul_res([(y, 0, 0, SSM_INNER), (u, 0, SSM_INNER // CONF_WIDTH, CONF_WIDTH)], w_out, i, h)


def _odd_layer(h, gain, cos, sin, w_in, w_gates, sc_w, pe, w1, w2, w_out, i, bsz, seq):
    main = 3 * SC_WIDTH + NSA_Q + 6 * NSA_KV
    per_g = 3 * NSA_GROUP
    wg = w_gates.reshape(D_MODEL, NSA_KV_HEADS, per_g)
    wg = jnp.pad(wg, ((0, 0), (0, 0), (0, LANES - per_g))).reshape(D_MODEL, NSA_KV_HEADS * LANES)
    proj = norm_matmul(h, gain, w_in, layer=i, n=main)
    gates = norm_matmul(h, gain, wg.astype(BF16), tn=NSA_KV_HEADS * LANES)
    y_c = shortconv_mixer(proj, sc_w, seq)
    ks, vs, kw, vw = kv_prep(proj, cos, sin, seq)
    kc, vc = compress_kv(proj, cos, sin, pe, w1, w2, bsz, seq)
    y_d = nsa_attention(proj, gates, cos, sin, kc, vc, ks, vs, kw, vw, bsz, seq)
    return matmul_res([(y_c, 0, 0, SC_WIDTH), (y_d, 0, 1, SC_WIDTH), (y_d, 1, 2, SC_WIDTH)], w_out, i, h)


def _ffn(h, gain, w_up, conv_w, conv_b, w_down, layer, seq):
    a = ffn_up_gate(h, gain, w_up, layer, conv_w, conv_b, seq)
    return matmul_res([(a, 0, 0, D_FF)], w_down, layer, h)


def kernel(x, positions, mix_norm, ffn_norm, final_norm, ab_w_in, ssm_conv_w, ssm_conv_b, ssm_dt_bias, ssm_a_log, ssm_d, ssm_norm, conf_dw_w, conf_dw_b, conf_ln_g, conf_ln_b, ab_w_out, cd_w_in, sc_conv_w, nsa_cmp_pe, nsa_cmp_w1, nsa_cmp_w2, cd_w_out, ffn_w_up, ffn_conv_w, ffn_conv_b, ffn_w_down):
    bsz, seq, d = x.shape
    depth = mix_norm.shape[0]
    h = x.reshape(bsz * seq, d)
    cos, sin = rope_tables(positions)
    ab_w_out_b, cd_w_in_b, cd_w_out_b = ab_w_out.astype(BF16), cd_w_in.astype(BF16), cd_w_out.astype(BF16)
    ffn_w_up_b, ffn_w_down_b = ffn_w_up.astype(BF16), ffn_w_down.astype(BF16)
    cmp_w1_b, cmp_w2_b = nsa_cmp_w1.astype(BF16), nsa_cmp_w2.astype(BF16)
    gate_cols = 3 * SC_WIDTH + NSA_Q + 6 * NSA_KV
    for layer in range(depth):
        i = layer // 2
        if layer % 2 == 0:
            h = _even_layer(h, mix_norm[layer], ab_w_in[i], ssm_conv_w[i], ssm_conv_b[i], ssm_dt_bias[i],
                            ssm_a_log[i], ssm_d[i], ssm_norm[i], conf_dw_w[i], conf_dw_b[i], conf_ln_g[i],
                            conf_ln_b[i], ab_w_out_b, i, bsz, seq)
        else:
            h = _odd_layer(h, mix_norm[layer], cos, sin, cd_w_in_b, cd_w_in[i][:, gate_cols:], sc_conv_w[i],
                           nsa_cmp_pe[i], cmp_w1_b[i], cmp_w2_b[i], cd_w_out_b, i, bsz, seq)
        h = _ffn(h, ffn_norm[layer], ffn_w_up_b, ffn_conv_w[layer], ffn_conv_b[layer], ffn_w_down_b, layer, seq)
    return rmsnorm_rows(h, final_norm).reshape(bsz, seq, d)
```

```python
import functools

import jax
import jax.numpy as jnp
from jax import lax
from jax.experimental import pallas as pl
from jax.experimental.pallas import tpu as pltpu

F32 = jnp.float32
BF16 = jnp.bfloat16

D_MODEL = 2048
NORM_EPS = 1e-6
SSM_INNER = 2048
SSM_HEADDIM = 64
SSM_HEADS = 32
SSM_GROUPS = 4
SSM_STATE = 128
SSM_CONV_WIDTH = 4
SSM_CHUNK = 128
CONF_WIDTH = 1024
CONF_CONV_WIDTH = 31
SC_WIDTH = 1024
SC_CONV_WIDTH = 3
NSA_HEAD_DIM = 128
NSA_HEADS = 16
NSA_KV_HEADS = 4
NSA_GROUP = 4
CMP_BLOCK = 32
CMP_STRIDE = 16
SLC_BLOCK = 64
N_SELECT = 16
N_LOCAL = 2
WINDOW = 512
ROPE_THETA = 10000.0
NSA_Q = 2048
NSA_KV = 512
D_FF = 5632
FFN_CONV_WIDTH = 3
NEG_BIG = -1e30
FORCE = 1e9

V7X_VMEM_BYTES = 64 * 1024 * 1024
VMEM_LIMIT = V7X_VMEM_BYTES - 8 * 1024 * 1024
LANES = 128
SUBLANES = 8

TM_PROJ = 1024
TN_PROJ = 1024
TM_OUT = 512
TN_OUT = 2048
TN_DOWN = 1024
TN_FFN = 512
FFN_HALO = 16
FFN_CHUNKS = 2
TM_CONF = 256
CONF_HALO = 32
TM_SC = 512
TM_ROPE = 1024
TQ = 128
TK = 1024
NSA_TILES = 2


def _cparams(*sem):
    return pltpu.CompilerParams(dimension_semantics=sem, vmem_limit_bytes=VMEM_LIMIT)


def _silu(x):
    return x * jax.nn.sigmoid(x)


def _softplus(x):
    return jnp.maximum(x, 0.0) + jnp.log1p(jnp.exp(-jnp.abs(x)))


def _split3(x):
    hi = x.astype(BF16)
    r1 = x - hi.astype(F32)
    mid = r1.astype(BF16)
    lo = (r1 - mid.astype(F32)).astype(BF16)
    return hi, mid, lo


def _dot(a, b):
    return jnp.dot(a, b, preferred_element_type=F32)


def _dot_nt(a, b):
    return lax.dot_general(a, b, (((1,), (1,)), ((), ())), preferred_element_type=F32)


def _dot_f32_rhs01(x, e01):
    return _dot(jnp.concatenate(_split3(x), axis=1), jnp.concatenate([e01] * 3, axis=0))


def _dot_f32_lhs01(e01, x):
    return _dot(jnp.concatenate([e01] * 3, axis=1), jnp.concatenate(_split3(x), axis=0))


def _dwconv(ext_ref, x, halo, first, w_ref, width):
    rows = x.shape[0]
    hrows = halo.shape[0]
    ext_ref[0:hrows, :] = jnp.where(first, 0.0, halo)
    ext_ref[hrows:hrows + rows, :] = x
    acc = None
    for k in range(width):
        off = hrows - (width - 1) + k
        term = w_ref[k:k + 1, :] * ext_ref[off:off + rows, :]
        acc = term if acc is None else acc + term
    return acc


def _norm_matmul_kernel(h_ref, g_ref, w_ref, o_ref, xn_ref):
    @pl.when(pl.program_id(1) == 0)
    def _():
        x = h_ref[...]
        ms = jnp.mean(x * x, axis=-1, keepdims=True)
        xn_ref[...] = (x * lax.rsqrt(ms + NORM_EPS) * g_ref[...]).astype(BF16)

    o_ref[...] = _dot(xn_ref[...], w_ref[...]).astype(o_ref.dtype)


def norm_matmul(h, gain, w, *, layer=None, n=None, tm=TM_PROJ, tn=TN_PROJ):
    t, k = h.shape
    if layer is None:
        n = w.shape[1]
        tn = min(tn, n)
        w_spec = pl.BlockSpec((k, tn), lambda i, j: (0, j))
    else:
        w_spec = pl.BlockSpec((None, k, tn), lambda i, j: (layer, 0, j))
    return pl.pallas_call(
        _norm_matmul_kernel,
        grid=(t // tm, n // tn),
        in_specs=[pl.BlockSpec((tm, k), lambda i, j: (i, 0)),
                  pl.BlockSpec((1, k), lambda i, j: (0, 0)),
                  w_spec],
        out_specs=pl.BlockSpec((tm, tn), lambda i, j: (i, j)),
        out_shape=jax.ShapeDtypeStruct((t, n), F32),
        scratch_shapes=[pltpu.VMEM((tm, k), BF16)],
        compiler_params=_cparams("parallel", "arbitrary"),
    )(h, gain.reshape(1, k), w)


def _matmul_res_kernel(*refs, n_in):
    a_refs, w_refs = refs[:n_in], refs[n_in:2 * n_in]
    r_ref, o_ref = refs[2 * n_in], refs[2 * n_in + 1]
    acc = _dot(a_refs[0][...], w_refs[0][...])
    for a_ref, w_ref in zip(a_refs[1:], w_refs[1:]):
        acc = acc + _dot(a_ref[...], w_ref[...])
    o_ref[...] = r_ref[...] + acc


def matmul_res(terms, w, layer, res, *, tm=TM_OUT, tn=TN_OUT):
    t, n = res.shape
    in_specs = ([pl.BlockSpec((tm, kw), lambda i, j, ca=ca: (i, ca)) for _, ca, _, kw in terms]
                + [pl.BlockSpec((None, kw, tn), lambda i, j, rw=rw: (layer, rw, j)) for _, _, rw, kw in terms]
                + [pl.BlockSpec((tm, tn), lambda i, j: (i, j))])
    return pl.pallas_call(
        functools.partial(_matmul_res_kernel, n_in=len(terms)),
        grid=(t // tm, n // tn),
        in_specs=in_specs,
        out_specs=pl.BlockSpec((tm, tn), lambda i, j: (i, j)),
        out_shape=jax.ShapeDtypeStruct((t, n), F32),
        compiler_params=_cparams("parallel", "parallel"),
    )(*[a for a, _, _, _ in terms], *([w] * len(terms)), res)


def _rmsnorm_kernel(h_ref, g_ref, o_ref):
    x = h_ref[...]
    ms = jnp.mean(x * x, axis=-1, keepdims=True)
    o_ref[...] = x * lax.rsqrt(ms + NORM_EPS) * g_ref[...]


def rmsnorm_rows(h, gain, *, tm=512):
    t, k = h.shape
    return pl.pallas_call(
        _rmsnorm_kernel,
        grid=(t // tm,),
        in_specs=[pl.BlockSpec((tm, k), lambda i: (i, 0)), pl.BlockSpec((1, k), lambda i: (0, 0))],
        out_specs=pl.BlockSpec((tm, k), lambda i: (i, 0)),
        out_shape=jax.ShapeDtypeStruct((t, k), F32),
        compiler_params=_cparams("parallel"),
    )(h, gain.reshape(1, k))


def _ffn_up_kernel(h_ref, hh_ref, gain_ref, wg_ref, wv_ref, cwg_ref, cwv_ref, bg_ref, bv_ref, o_ref,
                   xn_ref, *, tiles_per_seq):
    tm, tn = o_ref.shape

    @pl.when(pl.program_id(1) == 0)
    def _():
        first = (pl.program_id(0) % tiles_per_seq) == 0

        def norm(x):
            ms = jnp.mean(x * x, axis=-1, keepdims=True)
            return x * lax.rsqrt(ms + NORM_EPS) * gain_ref[...]

        xn_ref[0:FFN_HALO, :] = jnp.where(first, 0.0, norm(hh_ref[...])).astype(BF16)
        xn_ref[FFN_HALO:, :] = norm(h_ref[...]).astype(BF16)

    def conv(u, cw_ref, b_ref, cs):
        acc = cw_ref[0:1, cs] * u
        for k in range(1, FFN_CONV_WIDTH):
            acc = pltpu.roll(acc, 1, axis=0) + cw_ref[k:k + 1, cs] * u
        return acc[FFN_HALO:FFN_HALO + tm, :] + b_ref[:, cs]

    cw = tn // FFN_CHUNKS
    for c in range(FFN_CHUNKS):
        cs = slice(c * cw, (c + 1) * cw)
        ug = _dot(xn_ref[...], wg_ref[:, cs])
        uv = _dot(xn_ref[...], wv_ref[:, cs])
        o_ref[:, cs] = (_silu(conv(ug, cwg_ref, bg_ref, cs)) * conv(uv, cwv_ref, bv_ref, cs)).astype(o_ref.dtype)


def ffn_up_gate(h, gain, w_up, layer, conv_w, conv_b, seq, *, tm=TM_PROJ, tn=TN_FFN):
    t, k = h.shape
    nf = D_FF // tn
    hb = tm // FFN_HALO
    return pl.pallas_call(
        functools.partial(_ffn_up_kernel, tiles_per_seq=seq // tm),
        grid=(t // tm, nf),
        in_specs=[pl.BlockSpec((tm, k), lambda i, j: (i, 0)),
                  pl.BlockSpec((FFN_HALO, k), lambda i, j: (jnp.maximum(i * hb - 1, 0), 0)),
                  pl.BlockSpec((1, k), lambda i, j: (0, 0)),
                  pl.BlockSpec((None, k, tn), lambda i, j: (layer, 0, j)),
                  pl.BlockSpec((None, k, tn), lambda i, j: (layer, 0, j + nf)),
                  pl.BlockSpec((FFN_CONV_WIDTH, tn), lambda i, j: (0, j)),
                  pl.BlockSpec((FFN_CONV_WIDTH, tn), lambda i, j: (0, j + nf)),
                  pl.BlockSpec((1, tn), lambda i, j: (0, j)),
                  pl.BlockSpec((1, tn), lambda i, j: (0, j + nf))],
        out_specs=pl.BlockSpec((tm, tn), lambda i, j: (i, j)),
        out_shape=jax.ShapeDtypeStruct((t, D_FF), BF16),
        scratch_shapes=[pltpu.VMEM((tm + FFN_HALO, k), BF16)],
        compiler_params=_cparams("parallel", "arbitrary"),
    )(h, h, gain.reshape(1, k), w_up, w_up, conv_w, conv_w, conv_b.reshape(1, -1), conv_b.reshape(1, -1))


def _ssd_kernel(z_ref, xs_ref, xsh_ref, bc_ref, bch_ref, dt_ref, cwx_ref, cbx_ref, cwb_ref, cbb_ref,
                dtb_ref, alog_ref, dsk_ref, gn_ref, y_ref, extx_ref, extb_ref, state_ref):
    L = SSM_CHUNK
    GW = SSM_INNER // SSM_GROUPS
    first = pl.program_id(1) == 0

    @pl.when(first)
    def _():
        state_ref[...] = jnp.zeros_like(state_ref)

    xs = _silu(_dwconv(extx_ref, xs_ref[...], xsh_ref[...], first, cwx_ref, SSM_CONV_WIDTH) + cbx_ref[...])
    bc = _silu(_dwconv(extb_ref, bc_ref[...], bch_ref[...], first, cwb_ref, SSM_CONV_WIDTH) + cbb_ref[...])

    dt = _softplus(dt_ref[...] + dtb_ref[...])
    a = -jnp.exp(alog_ref[...])
    la = dt * a
    row = lax.broadcasted_iota(jnp.int32, (L, L), 0)
    col = lax.broadcasted_iota(jnp.int32, (L, L), 1)
    causal = col <= row
    cum = _dot_f32_lhs01(causal.astype(BF16), la)
    cum_t = cum.T
    clast = cum[L - 1:L, :]

    e_head = (lax.broadcasted_iota(jnp.int32, (LANES, SSM_INNER), 1) // SSM_HEADDIM
              == lax.broadcasted_iota(jnp.int32, (LANES, SSM_INNER), 0)).astype(BF16)
    x = xs * _dot_f32_rhs01(dt, e_head)
    ecum_x = _dot_f32_rhs01(jnp.exp(cum), e_head)
    dte_x = _dot_f32_rhs01(jnp.exp(clast - cum), e_head)
    cdec_x = _dot_f32_rhs01(jnp.broadcast_to(jnp.exp(clast), (SUBLANES, LANES)), e_head)[0:1, :]
    xb = x.astype(BF16)
    xdte = (x * dte_x).astype(BF16)
    lo_half = lax.broadcasted_iota(jnp.int32, (L, LANES), 1) < SSM_HEADDIM

    for g in range(SSM_GROUPS):
        bg = bc[:, g * SSM_STATE:(g + 1) * SSM_STATE]
        cg = bc[:, (SSM_GROUPS + g) * SSM_STATE:(SSM_GROUPS + g + 1) * SSM_STATE].astype(BF16)
        cb = _dot_nt(cg, bg.astype(BF16))
        hg = state_ref[g]
        y_off = _dot(cg, hg.astype(BF16)) * ecum_x[:, g * GW:(g + 1) * GW]
        pieces = []
        for pr in range(GW // LANES):
            h0 = g * (GW // SSM_HEADDIM) + 2 * pr
            xp = xb[:, h0 * SSM_HEADDIM:h0 * SSM_HEADDIM + LANES]
            ypair = None
            for s in range(2):
                h = h0 + s
                diff = cum[:, h:h + 1] - cum_t[h:h + 1, :]
                decay = jnp.exp(jnp.where(causal, diff, -jnp.inf))
                m = (cb * decay).astype(BF16)
                keep = lo_half if s == 0 else jnp.logical_not(lo_half)
                yh = _dot(m, jnp.where(keep, xp, jnp.zeros_like(xp)))
                ypair = yh if ypair is None else ypair + yh
            pieces.append(ypair)
        y_diag = jnp.concatenate(pieces, axis=-1)
        st = _dot(bg.T.astype(BF16), xdte[:, g * GW:(g + 1) * GW])
        state_ref[g] = hg * cdec_x[:, g * GW:(g + 1) * GW] + st

        yg = y_diag + y_off + dsk_ref[:, g * GW:(g + 1) * GW] * xs[:, g * GW:(g + 1) * GW]
        yg = yg * _silu(z_ref[:, g * GW:(g + 1) * GW])
        ms = jnp.mean(yg * yg, axis=-1, keepdims=True)
        yg = yg * lax.rsqrt(ms + NORM_EPS) * gn_ref[:, g * GW:(g + 1) * GW]
        y_ref[:, g * GW:(g + 1) * GW] = yg.astype(y_ref.dtype)


def ssd_mixer(proj, dt_raw, conv_w, conv_b, dt_bias, a_log, d_skip, ssm_norm, bsz, seq):
    L = SSM_CHUNK
    nc = seq // L
    hb = L // SUBLANES
    pad = lambda v: jnp.pad(v.reshape(1, -1), ((0, 0), (0, LANES - v.shape[-1])))
    rowblk = lambda b, c: b * nc + c
    halo_row = lambda b, c: jnp.maximum((b * nc + c) * hb - 1, 0)
    const = lambda b, c: (0, 0)
    bcw = 2 * SSM_GROUPS * SSM_STATE
    return pl.pallas_call(
        _ssd_kernel,
        grid=(bsz, nc),
        in_specs=[pl.BlockSpec((L, SSM_INNER), lambda b, c: (rowblk(b, c), 0)),
                  pl.BlockSpec((L, SSM_INNER), lambda b, c: (rowblk(b, c), 1)),
                  pl.BlockSpec((SUBLANES, SSM_INNER), lambda b, c: (halo_row(b, c), 1)),
                  pl.BlockSpec((L, bcw), lambda b, c: (rowblk(b, c), 4)),
                  pl.BlockSpec((SUBLANES, bcw), lambda b, c: (halo_row(b, c), 4)),
                  pl.BlockSpec((L, LANES), lambda b, c: (rowblk(b, c), 0)),
                  pl.BlockSpec((SSM_CONV_WIDTH, SSM_INNER), const),
                  pl.BlockSpec((1, SSM_INNER), const),
                  pl.BlockSpec((SSM_CONV_WIDTH, bcw), const),
                  pl.BlockSpec((1, bcw), const),
                  pl.BlockSpec((1, LANES), const),
                  pl.BlockSpec((1, LANES), const),
                  pl.BlockSpec((1, SSM_INNER), const),
                  pl.BlockSpec((1, SSM_INNER), const)],
        out_specs=pl.BlockSpec((L, SSM_INNER), lambda b, c: (rowblk(b, c), 0)),
        out_shape=jax.ShapeDtypeStruct((bsz * seq, SSM_INNER), BF16),
        scratch_shapes=[pltpu.VMEM((L + SUBLANES, SSM_INNER), F32),
                        pltpu.VMEM((L + SUBLANES, bcw), F32),
                        pltpu.VMEM((SSM_GROUPS, SSM_STATE, SSM_INNER // SSM_GROUPS), F32)],
        compiler_params=_cparams("parallel", "arbitrary"),
    )(proj, proj, proj, proj, proj, dt_raw,
      conv_w[:, :SSM_INNER], conv_b[:SSM_INNER].reshape(1, -1),
      conv_w[:, SSM_INNER:], conv_b[SSM_INNER:].reshape(1, -1),
      pad(dt_bias), pad(a_log), jnp.repeat(d_skip, SSM_HEADDIM).reshape(1, -1), ssm_norm.reshape(1, -1))


def _conformer_kernel(ua_ref, ug_ref, uah_ref, ugh_ref, w_ref, b_ref, lg_ref, lb_ref, o_ref, ext_ref,
                      shift_ref, conv_ref, *, tiles_per_seq):
    first = (pl.program_id(0) % tiles_per_seq) == 0
    tm = ua_ref.shape[0]
    ext_ref[0:CONF_HALO, :] = jnp.where(first, 0.0, uah_ref[...] * jax.nn.sigmoid(ugh_ref[...]))
    ext_ref[CONF_HALO:, :] = ua_ref[...] * jax.nn.sigmoid(ug_ref[...])
    span = tm + CONF_HALO - SUBLANES
    for s in range(1, SUBLANES):
        shift_ref[s - 1] = ext_ref[s:s + span, :]
    for c0 in range(0, CONF_WIDTH, LANES):
        cs = slice(c0, c0 + LANES)
        acc = jnp.broadcast_to(b_ref[:, cs], (tm, LANES))
        for k in range(CONF_CONV_WIDTH):
            q, s = divmod(CONF_HALO - (CONF_CONV_WIDTH - 1) + k, SUBLANES)
            rows = slice(q * SUBLANES, q * SUBLANES + tm)
            tap = ext_ref[rows, cs] if s == 0 else shift_ref[s - 1, rows, cs]
            acc = acc + w_ref[k:k + 1, cs] * tap
        conv_ref[:, cs] = acc
    c = conv_ref[...]
    mu = jnp.mean(c, axis=-1, keepdims=True)
    d = c - mu
    var = jnp.mean(d * d, axis=-1, keepdims=True)
    y = d * lax.rsqrt(var + NORM_EPS) * lg_ref[...] + lb_ref[...]
    o_ref[...] = _silu(y).astype(o_ref.dtype)


def conformer_mixer(proj, dw_w, dw_b, ln_g, ln_b, seq, *, tm=TM_CONF):
    t = proj.shape[0]
    hb = tm // CONF_HALO
    halo_row = lambda i: jnp.maximum(i * hb - 1, 0)
    const = lambda i: (0, 0)
    return pl.pallas_call(
        functools.partial(_conformer_kernel, tiles_per_seq=seq // tm),
        grid=(t // tm,),
        in_specs=[pl.BlockSpec((tm, CONF_WIDTH), lambda i: (i, 5)),
                  pl.BlockSpec((tm, CONF_WIDTH), lambda i: (i, 6)),
                  pl.BlockSpec((CONF_HALO, CONF_WIDTH), lambda i: (halo_row(i), 5)),
                  pl.BlockSpec((CONF_HALO, CONF_WIDTH), lambda i: (halo_row(i), 6)),
                  pl.BlockSpec((CONF_CONV_WIDTH, CONF_WIDTH), const),
                  pl.BlockSpec((1, CONF_WIDTH), const),
                  pl.BlockSpec((1, CONF_WIDTH), const),
                  pl.BlockSpec((1, CONF_WIDTH), const)],
        out_specs=pl.BlockSpec((tm, CONF_WIDTH), lambda i: (i, 0)),
        out_shape=jax.ShapeDtypeStruct((t, CONF_WIDTH), BF16),
        scratch_shapes=[pltpu.VMEM((tm + CONF_HALO, CONF_WIDTH), F32),
                        pltpu.VMEM((SUBLANES - 1, tm + CONF_HALO - SUBLANES, CONF_WIDTH), F32),
                        pltpu.VMEM((tm, CONF_WIDTH), F32)],
        compiler_params=_cparams("parallel"),
    )(proj, proj, proj, proj, dw_w, dw_b.reshape(1, -1), ln_g.reshape(1, -1), ln_b.reshape(1, -1))


def _shortconv_kernel(b_ref, c_ref, h_ref, ch_ref, hh_ref, w_ref, o_ref, ext_ref, *, tiles_per_seq):
    first = (pl.program_id(0) % tiles_per_seq) == 0
    conv = _dwconv(ext_ref, c_ref[...] * h_ref[...], ch_ref[...] * hh_ref[...], first, w_ref, SC_CONV_WIDTH)
    o_ref[...] = (b_ref[...] * conv).astype(o_ref.dtype)


def shortconv_mixer(proj, conv_w, seq, *, tm=TM_SC):
    t = proj.shape[0]
    hb = tm // SUBLANES
    halo_row = lambda i: jnp.maximum(i * hb - 1, 0)
    return pl.pallas_call(
        functools.partial(_shortconv_kernel, tiles_per_seq=seq // tm),
        grid=(t // tm,),
        in_specs=[pl.BlockSpec((tm, SC_WIDTH), lambda i: (i, 0)),
                  pl.BlockSpec((tm, SC_WIDTH), lambda i: (i, 1)),
                  pl.BlockSpec((tm, SC_WIDTH), lambda i: (i, 2)),
                  pl.BlockSpec((SUBLANES, SC_WIDTH), lambda i: (halo_row(i), 1)),
                  pl.BlockSpec((SUBLANES, SC_WIDTH), lambda i: (halo_row(i), 2)),
                  pl.BlockSpec((SC_CONV_WIDTH, SC_WIDTH), lambda i: (0, 0))],
        out_specs=pl.BlockSpec((tm, SC_WIDTH), lambda i: (i, 0)),
        out_shape=jax.ShapeDtypeStruct((t, SC_WIDTH), BF16),
        scratch_shapes=[pltpu.VMEM((tm + SUBLANES, SC_WIDTH), F32)],
        compiler_params=_cparams("parallel"),
    )(proj, proj, proj, proj, proj, conv_w)


def _rope_table_kernel(pos_ref, inv_ref, cos_ref, sin_ref):
    ang = pos_ref[...].astype(F32) * inv_ref[...]
    sign = jnp.where(lax.broadcasted_iota(jnp.int32, ang.shape, 1) < NSA_HEAD_DIM // 2, -1.0, 1.0)
    cos_ref[...] = jnp.cos(ang)
    sin_ref[...] = jnp.sin(ang) * sign


def rope_tables(positions, *, tm=TM_ROPE):
    t = positions.size
    half = NSA_HEAD_DIM // 2
    inv = ROPE_THETA ** (-jnp.arange(half, dtype=F32) / half)
    inv = jnp.concatenate([inv, inv]).reshape(1, NSA_HEAD_DIM)
    tab = jax.ShapeDtypeStruct((t, NSA_HEAD_DIM), F32)
    return pl.pallas_call(
        _rope_table_kernel,
        grid=(t // tm,),
        in_specs=[pl.BlockSpec((tm, 1), lambda i: (i, 0)), pl.BlockSpec((1, NSA_HEAD_DIM), lambda i: (0, 0))],
        out_specs=[pl.BlockSpec((tm, NSA_HEAD_DIM), lambda i: (i, 0))] * 2,
        out_shape=[tab, tab],
        compiler_params=_cparams("parallel"),
    )(positions.reshape(t, 1), inv)


def _rope(x, cos, sin_signed):
    return x * cos + pltpu.roll(x, NSA_HEAD_DIM // 2, axis=1) * sin_signed


def _kv_prep_kernel(ks_ref, vs_ref, kw_ref, vw_ref, cos_ref, sin_ref, kso_ref, vso_ref, kwo_ref, vwo_ref,
                    *, tiles_per_seq):
    D = NSA_HEAD_DIM
    tm = ks_ref.shape[0]
    cos, sin = cos_ref[...], sin_ref[...]
    row = (pl.program_id(0) % tiles_per_seq) * tm + lax.broadcasted_iota(jnp.int32, (tm, D), 0)
    lane = lax.broadcasted_iota(jnp.int32, (tm, D), 1)
    onehot = jnp.where(row // SLC_BLOCK == lane, 1.0, 0.0).astype(BF16)
    ones_col = jnp.where(lane == 0, 1.0, 0.0).astype(BF16)
    for g in range(NSA_KV_HEADS):
        sl = slice(g * D, (g + 1) * D)
        kso_ref[:, 2 * g * D:(2 * g + 1) * D] = _rope(ks_ref[:, sl], cos, sin).astype(BF16)
        kso_ref[:, (2 * g + 1) * D:(2 * g + 2) * D] = onehot
        kwo_ref[:, sl] = _rope(kw_ref[:, sl], cos, sin).astype(BF16)
        vso_ref[:, 2 * g * D:(2 * g + 1) * D] = vs_ref[:, sl].astype(BF16)
        vso_ref[:, (2 * g + 1) * D:(2 * g + 2) * D] = ones_col
        vwo_ref[:, 2 * g * D:(2 * g + 1) * D] = vw_ref[:, sl].astype(BF16)
        vwo_ref[:, (2 * g + 1) * D:(2 * g + 2) * D] = ones_col


def kv_prep(proj, cos, sin, seq, *, tm=512):
    t = proj.shape[0]
    kv = jax.ShapeDtypeStruct((t, NSA_KV), BF16)
    kv_aug = jax.ShapeDtypeStruct((t, 2 * NSA_KV), BF16)
    spec = pl.BlockSpec((tm, NSA_KV), lambda i: (i, 0))
    spec_aug = pl.BlockSpec((tm, 2 * NSA_KV), lambda i: (i, 0))
    return pl.pallas_call(
        functools.partial(_kv_prep_kernel, tiles_per_seq=seq // tm),
        grid=(t // tm,),
        in_specs=[pl.BlockSpec((tm, NSA_KV), lambda i: (i, 12)),
                  pl.BlockSpec((tm, NSA_KV), lambda i: (i, 13)),
                  pl.BlockSpec((tm, NSA_KV), lambda i: (i, 14)),
                  pl.BlockSpec((tm, NSA_KV), lambda i: (i, 15)),
                  pl.BlockSpec((tm, NSA_HEAD_DIM), lambda i: (i, 0)),
                  pl.BlockSpec((tm, NSA_HEAD_DIM), lambda i: (i, 0))],
        out_specs=[spec_aug, spec_aug, spec, spec_aug],
        out_shape=[kv_aug, kv_aug, kv, kv_aug],
        compiler_params=_cparams("parallel"),
    )(proj, proj, proj, proj, cos, sin)


def _compress_kernel(kt_ref, vt_ref, cos_ref, sin_ref, pe_ref, w1_ref, w2_ref, kc_ref, vc_ref):
    nb = kc_ref.shape[0]
    half = CMP_BLOCK // 2

    def compress(tok_ref, which):
        acc_lo = jnp.zeros((nb, NSA_HEAD_DIM), F32)
        acc_hi = jnp.zeros((nb, NSA_HEAD_DIM), F32)
        for l in range(half):
            r = tok_ref[pl.ds(l, nb, stride=CMP_STRIDE), :]
            lo = (r + pe_ref[which, l:l + 1, :]).astype(BF16)
            hi = (r + pe_ref[which, half + l:half + l + 1, :]).astype(BF16)
            acc_lo = acc_lo + _dot(lo, w1_ref[which, l * NSA_HEAD_DIM:(l + 1) * NSA_HEAD_DIM, :])
            acc_hi = acc_hi + _dot(hi, w1_ref[which, (half + l) * NSA_HEAD_DIM:(half + l + 1) * NSA_HEAD_DIM, :])
        pre = acc_lo + pltpu.roll(acc_hi, nb - 1, axis=0)
        return _dot(_silu(pre).astype(BF16), w2_ref[which])

    cos_e = pltpu.roll(cos_ref[pl.ds(CMP_STRIDE - 1, nb, stride=CMP_STRIDE), :], nb - 1, axis=0)
    sin_e = pltpu.roll(sin_ref[pl.ds(CMP_STRIDE - 1, nb, stride=CMP_STRIDE), :], nb - 1, axis=0)
    kc_ref[...] = _rope(compress(kt_ref, 0), cos_e, sin_e).astype(BF16)
    vc_ref[...] = compress(vt_ref, 1).astype(BF16)


def compress_kv(proj, cos, sin, pe, w1, w2, bsz, seq):
    nb = seq // CMP_STRIDE
    out = jax.ShapeDtypeStruct((bsz, NSA_KV_HEADS, nb, NSA_HEAD_DIM), BF16)
    full = lambda *shape: pl.BlockSpec(shape, lambda b, g: (0,) * len(shape))
    return pl.pallas_call(
        _compress_kernel,
        grid=(bsz, NSA_KV_HEADS),
        in_specs=[pl.BlockSpec((seq, NSA_HEAD_DIM), lambda b, g: (b, 40 + g)),
                  pl.BlockSpec((seq, NSA_HEAD_DIM), lambda b, g: (b, 44 + g)),
                  pl.BlockSpec((seq, NSA_HEAD_DIM), lambda b, g: (b, 0)),
                  pl.BlockSpec((seq, NSA_HEAD_DIM), lambda b, g: (b, 0)),
                  full(2, CMP_BLOCK, NSA_HEAD_DIM),
                  full(2, CMP_BLOCK * NSA_HEAD_DIM, NSA_HEAD_DIM),
                  full(2, NSA_HEAD_DIM, NSA_HEAD_DIM)],
        out_specs=[pl.BlockSpec((None, None, nb, NSA_HEAD_DIM), lambda b, g: (b, g, 0, 0))] * 2,
        out_shape=[out, out],
        compiler_params=_cparams("parallel", "parallel"),
    )(proj, proj, cos, sin, pe, w1, w2)


def _nsa_kernel(q_ref, cos_ref, sin_ref, gate_ref, kc_ref, vc_ref, ks_ref, vs_ref, kw_ref, vw_ref,
                o_ref, qa_ref, impt_ref, sc_ref, mrun_ref, acc_ref):
    G = NSA_GROUP
    D = NSA_HEAD_DIM
    R = G * TQ
    base = pl.program_id(2) * (NSA_TILES * TQ)
    scale = D ** -0.5
    n_cmp = kc_ref.shape[0]
    n_slc = ks_ref.shape[0] // SLC_BLOCK
    wk = WINDOW + TQ
    j = lax.broadcasted_iota(jnp.int32, (1, LANES), 1)
    ends = lax.broadcasted_iota(jnp.int32, (1, n_cmp), 1) * CMP_STRIDE + (CMP_BLOCK - 1)
    n_i = lax.broadcasted_iota(jnp.int32, (n_cmp, LANES), 0)
    j_i = lax.broadcasted_iota(jnp.int32, (n_cmp, LANES), 1)
    overlap = ((n_i * CMP_STRIDE < (j_i + 1) * SLC_BLOCK)
               & (n_i * CMP_STRIDE + CMP_BLOCK - 1 >= j_i * SLC_BLOCK)).astype(BF16)
    rel = lax.broadcasted_iota(jnp.int32, (TQ, wk), 1) - lax.broadcasted_iota(jnp.int32, (TQ, wk), 0)
    lower = (lax.broadcasted_iota(jnp.int32, (TQ, TQ), 1) <= lax.broadcasted_iota(jnp.int32, (TQ, TQ), 0))
    own_bias = jnp.where(lower, 0.0, NEG_BIG)
    sub = lax.broadcasted_iota(jnp.int32, (SUBLANES, TQ), 0)

    o_cmp, o_win, s_own = [], [], []
    for u in range(NSA_TILES):
        rows = slice(u * TQ, (u + 1) * TQ)
        q0 = base + u * TQ
        cos, sin = cos_ref[rows, :], sin_ref[rows, :]
        for e in range(G):
            xq = _rope(q_ref[rows, e * D:(e + 1) * D], cos, sin)
            qa_ref[u, e * TQ:(e + 1) * TQ, 0:D] = (xq * scale).astype(BF16)
        qs = qa_ref[u, :, 0:D]
        t_col = q0 + lax.broadcasted_iota(jnp.int32, (TQ, 1), 0)

        cvalid = ends <= t_col
        cbias = jnp.where(cvalid, 0.0, NEG_BIG)
        c01 = jnp.where(cvalid, 1.0, 0.0)
        s = _dot_nt(qs, kc_ref[...]).reshape(G, TQ, n_cmp) + cbias[None]
        m = jnp.max(s, axis=-1, keepdims=True)
        p = jnp.exp(s - m) * c01[None]
        l = jnp.sum(p, axis=-1, keepdims=True)
        p = p * (1.0 / jnp.maximum(l, 1e-30))
        o_cmp.append(_dot(p.reshape(R, n_cmp).astype(BF16), vc_ref[...]))

        start = pl.multiple_of(jnp.maximum(q0 - WINDOW, 0), TQ)
        d = q0 - start
        wbias = jnp.where((rel <= d) & (rel > d - WINDOW), 0.0, NEG_BIG)
        sw = _dot_nt(qs, kw_ref[pl.ds(start, wk), :]).reshape(G, TQ, wk) + wbias[None]
        mw = jnp.max(sw, axis=-1, keepdims=True)
        pw = jnp.exp(sw - mw).reshape(R, wk)
        accw = _dot(pw.astype(BF16), vw_ref[pl.ds(start, wk), :])
        o_win.append(accw[:, 0:D] * (1.0 / accw[:, D:D + 1]))

        psum = p[0]
        for e in range(1, G):
            psum = psum + p[e]
        imp = _dot_f32_rhs01(psum, overlap)
        cur = t_col // SLC_BLOCK
        forced = (j == 0) | ((j <= cur) & (j > cur - N_LOCAL))
        valid = j * SLC_BLOCK <= t_col
        imp = jnp.where(forced, FORCE, jnp.where(valid, imp, -FORCE))
        impt_ref[u] = imp.T
        nrb = n_slc // SUBLANES
        vals = [impt_ref[u, rb * SUBLANES:(rb + 1) * SUBLANES, :] for rb in range(nrb)]
        cnt = [jnp.zeros((SUBLANES, TQ), F32) for _ in range(nrb)]
        for k in range(n_slc):
            vk = jnp.broadcast_to(impt_ref[u, k:k + 1, :], (SUBLANES, TQ))
            for rb in range(nrb):
                if rb * SUBLANES > k:
                    beats = vk >= vals[rb]
                elif rb * SUBLANES + SUBLANES - 1 < k:
                    beats = vk > vals[rb]
                else:
                    beats = (vk > vals[rb]) | ((vk == vals[rb]) & (sub > k - rb * SUBLANES))
                cnt[rb] = cnt[rb] + jnp.where(beats, 1.0, 0.0)
        sel_t = jnp.concatenate([jnp.where(c < N_SELECT, 1.0, 0.0) for c in cnt]
                                + [jnp.zeros((LANES - n_slc, TQ), F32)], axis=0)
        sel = sel_t.T
        sbias = jnp.where((sel > 0.5) & (j * SLC_BLOCK < q0), 0.0, NEG_BIG).astype(BF16)
        for e in range(G):
            qa_ref[u, e * TQ:(e + 1) * TQ, D:2 * D] = sbias

        s_own.append((_dot_nt(qs, ks_ref[pl.ds(q0, TQ), 0:D]).reshape(G, TQ, TQ)
                      + own_bias[None]).reshape(R, TQ))
        mrun_ref[u] = s_own[u]

    n_kt = (base + (NSA_TILES - 1) * TQ + TK - 1) // TK

    def score_body(kt, carry):
        k0 = pl.multiple_of(kt * TK, TK)
        for u in range(NSA_TILES):
            s = _dot_nt(qa_ref[u], ks_ref[pl.ds(k0, TK), :])
            sc_ref[u, kt] = s
            t = s[:, 0:LANES]
            for c in range(1, TK // LANES):
                t = jnp.maximum(t, s[:, c * LANES:(c + 1) * LANES])
            mrun_ref[u] = jnp.maximum(mrun_ref[u], t)
        return carry

    lax.fori_loop(0, n_kt, score_body, 0)
    m_row = []
    for u in range(NSA_TILES):
        m_row.append(jnp.max(mrun_ref[u], axis=-1, keepdims=True))
        acc_ref[u] = _dot(jnp.exp(s_own[u] - m_row[u]).astype(BF16),
                          vs_ref[pl.ds(base + u * TQ, TQ), :])

    def pv_body(kt, carry):
        k0 = pl.multiple_of(kt * TK, TK)
        for u in range(NSA_TILES):
            p = jnp.exp(sc_ref[u, kt] - m_row[u]).astype(BF16)
            acc_ref[u] += _dot(p, vs_ref[pl.ds(k0, TK), :])
        return carry

    lax.fori_loop(0, n_kt, pv_body, 0)

    for u in range(NSA_TILES):
        rows = slice(u * TQ, (u + 1) * TQ)
        acc = acc_ref[u]
        o_slc = acc[:, 0:D] * (1.0 / acc[:, D:D + 1])
        gate = jax.nn.sigmoid(gate_ref[rows, :])
        for e in range(G):
            hr = slice(e * TQ, (e + 1) * TQ)
            o = (gate[:, 3 * e:3 * e + 1] * o_cmp[u][hr] + gate[:, 3 * e + 1:3 * e + 2] * o_slc[hr]
                 + gate[:, 3 * e + 2:3 * e + 3] * o_win[u][hr])
            o_ref[rows, e * D:(e + 1) * D] = o.astype(o_ref.dtype)


def nsa_attention(proj, gates, cos, sin, kc, vc, ks, vs, kw, vw, bsz, seq):
    tq = NSA_TILES * TQ
    nq = seq // tq
    qw = NSA_GROUP * NSA_HEAD_DIM
    rows = NSA_GROUP * TQ
    row = lambda b, g, i: b * nq + i
    kvspec = pl.BlockSpec((seq, NSA_HEAD_DIM), lambda b, g, i: (b, g))
    augspec = pl.BlockSpec((seq, 2 * NSA_HEAD_DIM), lambda b, g, i: (b, g))
    cspec = pl.BlockSpec((None, None, seq // CMP_STRIDE, NSA_HEAD_DIM), lambda b, g, i: (b, g, 0, 0))
    return pl.pallas_call(
        _nsa_kernel,
        grid=(bsz, NSA_KV_HEADS, nq),
        in_specs=[pl.BlockSpec((tq, qw), lambda b, g, i: (row(b, g, i), 6 + g)),
                  pl.BlockSpec((tq, NSA_HEAD_DIM), lambda b, g, i: (row(b, g, i), 0)),
                  pl.BlockSpec((tq, NSA_HEAD_DIM), lambda b, g, i: (row(b, g, i), 0)),
                  pl.BlockSpec((tq, LANES), lambda b, g, i: (row(b, g, i), g)),
                  cspec, cspec, augspec, augspec, kvspec, augspec],
        out_specs=pl.BlockSpec((tq, qw), lambda b, g, i: (row(b, g, i), g)),
        out_shape=jax.ShapeDtypeStruct((bsz * seq, NSA_Q), BF16),
        scratch_shapes=[pltpu.VMEM((NSA_TILES, rows, 2 * NSA_HEAD_DIM), BF16),
                        pltpu.VMEM((NSA_TILES, LANES, TQ), F32),
                        pltpu.VMEM((NSA_TILES, seq // TK, rows, TK), F32),
                        pltpu.VMEM((NSA_TILES, rows, LANES), F32),
                        pltpu.VMEM((NSA_TILES, rows, 2 * NSA_HEAD_DIM), F32)],
        compiler_params=_cparams("parallel", "parallel", "parallel"),
    )(proj, cos, sin, gates, kc, vc, ks, vs, kw, vw)


def _even_layer(h, gain, w_in, conv_w, conv_b, dt_bias, a_log, d_skip, ssm_norm,
                dw_w, dw_b, ln_g, ln_b, w_out, i, bsz, seq):
    xbc_end = SSM_INNER + SSM_INNER + 2 * SSM_GROUPS * SSM_STATE
    dt_end = xbc_end + SSM_HEADS
    w_main = jnp.concatenate([w_in[:, :xbc_end], w_in[:, dt_end:]], axis=1).astype(BF16)
    w_dt = jnp.pad(w_in[:, xbc_end:dt_end], ((0, 0), (0, LANES - SSM_HEADS))).astype(BF16)
    proj = norm_matmul(h, gain, w_main)
    dt_raw = norm_matmul(h, gain, w_dt)
    y = ssd_mixer(proj, dt_raw, conv_w, conv_b, dt_bias, a_log, d_skip, ssm_norm, bsz, seq)
    u = conformer_mixer(proj, dw_w, dw_b, ln_g, ln_b, seq)
    return matmul_res([(y, 0, 0, SSM_INNER), (u, 0, SSM_INNER // CONF_WIDTH, CONF_WIDTH)], w_out, i, h)


def _odd_layer(h, gain, cos, sin, w_in, w_gates, sc_w, pe, w1, w2, w_out, i, bsz, seq):
    main = 3 * SC_WIDTH + NSA_Q + 6 * NSA_KV
    per_g = 3 * NSA_GROUP
    wg = w_gates.reshape(D_MODEL, NSA_KV_HEADS, per_g)
    wg = jnp.pad(wg, ((0, 0), (0, 0), (0, LANES - per_g))).reshape(D_MODEL, NSA_KV_HEADS * LANES)
    proj = norm_matmul(h, gain, w_in, layer=i, n=main)
    gates = norm_matmul(h, gain, wg.astype(BF16), tn=NSA_KV_HEADS * LANES)
    y_c = shortconv_mixer(proj, sc_w, seq)
    ks, vs, kw, vw = kv_prep(proj, cos, sin, seq)
    kc, vc = compress_kv(proj, cos, sin, pe, w1, w2, bsz, seq)
    y_d = nsa_attention(proj, gates, cos, sin, kc, vc, ks, vs, kw, vw, bsz, seq)
    return matmul_res([(y_c, 0, 0, SC_WIDTH), (y_d, 0, 1, SC_WIDTH), (y_d, 1, 2, SC_WIDTH)], w_out, i, h)


def _ffn(h, gain, w_up, conv_w, conv_b, w_down, layer, seq):
    a = ffn_up_gate(h, gain, w_up, layer, conv_w, conv_b, seq)
    return matmul_res([(a, 0, 0, D_FF)], w_down, layer, h, tn=TN_DOWN)


def kernel(x, positions, mix_norm, ffn_norm, final_norm, ab_w_in, ssm_conv_w, ssm_conv_b, ssm_dt_bias, ssm_a_log, ssm_d, ssm_norm, conf_dw_w, conf_dw_b, conf_ln_g, conf_ln_b, ab_w_out, cd_w_in, sc_conv_w, nsa_cmp_pe, nsa_cmp_w1, nsa_cmp_w2, cd_w_out, ffn_w_up, ffn_conv_w, ffn_conv_b, ffn_w_down):
    bsz, seq, d = x.shape
    depth = mix_norm.shape[0]
    h = x.reshape(bsz * seq, d)
    cos, sin = rope_tables(positions)
    ab_w_out_b, cd_w_in_b, cd_w_out_b = ab_w_out.astype(BF16), cd_w_in.astype(BF16), cd_w_out.astype(BF16)
    ffn_w_up_b, ffn_w_down_b = ffn_w_up.astype(BF16), ffn_w_down.astype(BF16)
    cmp_w1_b, cmp_w2_b = nsa_cmp_w1.astype(BF16), nsa_cmp_w2.astype(BF16)
    gate_cols = 3 * SC_WIDTH + NSA_Q + 6 * NSA_KV
    for layer in range(depth):
        i = layer // 2
        if layer % 2 == 0:
            h = _even_layer(h, mix_norm[layer], ab_w_in[i], ssm_conv_w[i], ssm_conv_b[i], ssm_dt_bias[i],
                            ssm_a_log[i], ssm_d[i], ssm_norm[i], conf_dw_w[i], conf_dw_b[i], conf_ln_g[i],
                            conf_ln_b[i], ab_w_out_b, i, bsz, seq)
        else:
            h = _odd_layer(h, mix_norm[layer], cos, sin, cd_w_in_b, cd_w_in[i][:, gate_cols:], sc_conv_w[i],
                           nsa_cmp_pe[i], cmp_w1_b[i], cmp_w2_b[i], cd_w_out_b, i, bsz, seq)
        h = _ffn(h, ffn_norm[layer], ffn_w_up_b, ffn_conv_w[layer], ffn_conv_b[layer], ffn_w_down_b, layer, seq)
    return rmsnorm_rows(h, final_norm).reshape(bsz, seq, d)
```

```python
import functools

import jax
import jax.numpy as jnp
from jax import lax
from jax.experimental import pallas as pl
from jax.experimental.pallas import tpu as pltpu

F32 = jnp.float32
BF16 = jnp.bfloat16

D_MODEL = 2048
NORM_EPS = 1e-6
SSM_INNER = 2048
SSM_HEADDIM = 64
SSM_HEADS = 32
SSM_GROUPS = 4
SSM_STATE = 128
SSM_CONV_WIDTH = 4
SSM_CHUNK = 128
CONF_WIDTH = 1024
CONF_CONV_WIDTH = 31
SC_WIDTH = 1024
SC_CONV_WIDTH = 3
NSA_HEAD_DIM = 128
NSA_HEADS = 16
NSA_KV_HEADS = 4
NSA_GROUP = 4
CMP_BLOCK = 32
CMP_STRIDE = 16
SLC_BLOCK = 64
N_SELECT = 16
N_LOCAL = 2
WINDOW = 512
ROPE_THETA = 10000.0
NSA_Q = 2048
NSA_KV = 512
D_FF = 5632
FFN_CONV_WIDTH = 3
NEG_BIG = -1e30
FORCE = 1e9

V7X_VMEM_BYTES = 64 * 1024 * 1024
VMEM_LIMIT = V7X_VMEM_BYTES - 8 * 1024 * 1024
LANES = 128
SUBLANES = 8

TM_PROJ = 1024
TN_PROJ = 1024
TM_OUT = 512
TN_OUT = 2048
TN_DOWN = 1024
TN_FFN = 512
FFN_HALO = 16
FFN_CHUNKS = 2
TM_CONF = 256
CONF_HALO = 32
TM_SC = 512
TM_ROPE = 1024
TQ = 128
TK = 1024
NSA_TILES = 2


def _cparams(*sem):
    return pltpu.CompilerParams(dimension_semantics=sem, vmem_limit_bytes=VMEM_LIMIT)


def _silu(x):
    return x * jax.nn.sigmoid(x)


def _softplus(x):
    return jnp.maximum(x, 0.0) + jnp.log1p(jnp.exp(-jnp.abs(x)))


def _split3(x):
    hi = x.astype(BF16)
    r1 = x - hi.astype(F32)
    mid = r1.astype(BF16)
    lo = (r1 - mid.astype(F32)).astype(BF16)
    return hi, mid, lo


def _dot(a, b):
    return jnp.dot(a, b, preferred_element_type=F32)


def _dot_nt(a, b):
    return lax.dot_general(a, b, (((1,), (1,)), ((), ())), preferred_element_type=F32)


def _dot_f32_rhs01(x, e01):
    return _dot(jnp.concatenate(_split3(x), axis=1), jnp.concatenate([e01] * 3, axis=0))


def _dot_f32_lhs01(e01, x):
    return _dot(jnp.concatenate([e01] * 3, axis=1), jnp.concatenate(_split3(x), axis=0))


def _dwconv(ext_ref, x, halo, first, w_ref, width):
    rows = x.shape[0]
    hrows = halo.shape[0]
    ext_ref[0:hrows, :] = jnp.where(first, 0.0, halo)
    ext_ref[hrows:hrows + rows, :] = x
    acc = None
    for k in range(width):
        off = hrows - (width - 1) + k
        term = w_ref[k:k + 1, :] * ext_ref[off:off + rows, :]
        acc = term if acc is None else acc + term
    return acc


def _norm_matmul_kernel(h_ref, g_ref, w_ref, o_ref, xn_ref):
    @pl.when(pl.program_id(1) == 0)
    def _():
        x = h_ref[...]
        ms = jnp.mean(x * x, axis=-1, keepdims=True)
        xn_ref[...] = (x * lax.rsqrt(ms + NORM_EPS) * g_ref[...]).astype(BF16)

    o_ref[...] = _dot(xn_ref[...], w_ref[...]).astype(o_ref.dtype)


def norm_matmul(h, gain, w, *, layer=None, n=None, tm=TM_PROJ, tn=TN_PROJ):
    t, k = h.shape
    if layer is None:
        n = w.shape[1]
        tn = min(tn, n)
        w_spec = pl.BlockSpec((k, tn), lambda i, j: (0, j))
    else:
        w_spec = pl.BlockSpec((None, k, tn), lambda i, j: (layer, 0, j))
    return pl.pallas_call(
        _norm_matmul_kernel,
        grid=(t // tm, n // tn),
        in_specs=[pl.BlockSpec((tm, k), lambda i, j: (i, 0)),
                  pl.BlockSpec((1, k), lambda i, j: (0, 0)),
                  w_spec],
        out_specs=pl.BlockSpec((tm, tn), lambda i, j: (i, j)),
        out_shape=jax.ShapeDtypeStruct((t, n), F32),
        scratch_shapes=[pltpu.VMEM((tm, k), BF16)],
        compiler_params=_cparams("parallel", "arbitrary"),
    )(h, gain.reshape(1, k), w)


def _matmul_res_kernel(*refs, n_in):
    a_refs, w_refs = refs[:n_in], refs[n_in:2 * n_in]
    r_ref, o_ref = refs[2 * n_in], refs[2 * n_in + 1]
    acc = _dot(a_refs[0][...], w_refs[0][...])
    for a_ref, w_ref in zip(a_refs[1:], w_refs[1:]):
        acc = acc + _dot(a_ref[...], w_ref[...])
    o_ref[...] = r_ref[...] + acc


def matmul_res(terms, w, layer, res, *, tm=TM_OUT, tn=TN_OUT):
    t, n = res.shape
    in_specs = ([pl.BlockSpec((tm, kw), lambda j, i, ca=ca: (i, ca)) for _, ca, _, kw in terms]
                + [pl.BlockSpec((None, kw, tn), lambda j, i, rw=rw: (layer, rw, j)) for _, _, rw, kw in terms]
                + [pl.BlockSpec((tm, tn), lambda j, i: (i, j))])
    return pl.pallas_call(
        functools.partial(_matmul_res_kernel, n_in=len(terms)),
        grid=(n // tn, t // tm),
        in_specs=in_specs,
        out_specs=pl.BlockSpec((tm, tn), lambda j, i: (i, j)),
        out_shape=jax.ShapeDtypeStruct((t, n), F32),
        compiler_params=_cparams("parallel", "parallel"),
    )(*[a for a, _, _, _ in terms], *([w] * len(terms)), res)


def _rmsnorm_kernel(h_ref, g_ref, o_ref):
    x = h_ref[...]
    ms = jnp.mean(x * x, axis=-1, keepdims=True)
    o_ref[...] = x * lax.rsqrt(ms + NORM_EPS) * g_ref[...]


def rmsnorm_rows(h, gain, *, tm=512):
    t, k = h.shape
    return pl.pallas_call(
        _rmsnorm_kernel,
        grid=(t // tm,),
        in_specs=[pl.BlockSpec((tm, k), lambda i: (i, 0)), pl.BlockSpec((1, k), lambda i: (0, 0))],
        out_specs=pl.BlockSpec((tm, k), lambda i: (i, 0)),
        out_shape=jax.ShapeDtypeStruct((t, k), F32),
        compiler_params=_cparams("parallel"),
    )(h, gain.reshape(1, k))


def _ffn_up_kernel(h_ref, hh_ref, gain_ref, wg_ref, wv_ref, cwg_ref, cwv_ref, bg_ref, bv_ref, o_ref,
                   xn_ref, *, tiles_per_seq):
    tm, tn = o_ref.shape

    @pl.when(pl.program_id(1) == 0)
    def _():
        first = (pl.program_id(0) % tiles_per_seq) == 0

        def norm(x):
            ms = jnp.mean(x * x, axis=-1, keepdims=True)
            return x * lax.rsqrt(ms + NORM_EPS) * gain_ref[...]

        xn_ref[0:FFN_HALO, :] = jnp.where(first, 0.0, norm(hh_ref[...])).astype(BF16)
        xn_ref[FFN_HALO:, :] = norm(h_ref[...]).astype(BF16)

    def conv(u, cw_ref, b_ref, cs):
        acc = cw_ref[0:1, cs] * u
        for k in range(1, FFN_CONV_WIDTH):
            acc = pltpu.roll(acc, 1, axis=0) + cw_ref[k:k + 1, cs] * u
        return acc[FFN_HALO:FFN_HALO + tm, :] + b_ref[:, cs]

    cw = tn // FFN_CHUNKS
    for c in range(FFN_CHUNKS):
        cs = slice(c * cw, (c + 1) * cw)
        ug = _dot(xn_ref[...], wg_ref[:, cs])
        uv = _dot(xn_ref[...], wv_ref[:, cs])
        o_ref[:, cs] = (_silu(conv(ug, cwg_ref, bg_ref, cs)) * conv(uv, cwv_ref, bv_ref, cs)).astype(o_ref.dtype)


def ffn_up_gate(h, gain, w_up, layer, conv_w, conv_b, seq, *, tm=TM_PROJ, tn=TN_FFN):
    t, k = h.shape
    nf = D_FF // tn
    hb = tm // FFN_HALO
    return pl.pallas_call(
        functools.partial(_ffn_up_kernel, tiles_per_seq=seq // tm),
        grid=(t // tm, nf),
        in_specs=[pl.BlockSpec((tm, k), lambda i, j: (i, 0)),
                  pl.BlockSpec((FFN_HALO, k), lambda i, j: (jnp.maximum(i * hb - 1, 0), 0)),
                  pl.BlockSpec((1, k), lambda i, j: (0, 0)),
                  pl.BlockSpec((None, k, tn), lambda i, j: (layer, 0, j)),
                  pl.BlockSpec((None, k, tn), lambda i, j: (layer, 0, j + nf)),
                  pl.BlockSpec((FFN_CONV_WIDTH, tn), lambda i, j: (0, j)),
                  pl.BlockSpec((FFN_CONV_WIDTH, tn), lambda i, j: (0, j + nf)),
                  pl.BlockSpec((1, tn), lambda i, j: (0, j)),
                  pl.BlockSpec((1, tn), lambda i, j: (0, j + nf))],
        out_specs=pl.BlockSpec((tm, tn), lambda i, j: (i, j)),
        out_shape=jax.ShapeDtypeStruct((t, D_FF), BF16),
        scratch_shapes=[pltpu.VMEM((tm + FFN_HALO, k), BF16)],
        compiler_params=_cparams("parallel", "arbitrary"),
    )(h, h, gain.reshape(1, k), w_up, w_up, conv_w, conv_w, conv_b.reshape(1, -1), conv_b.reshape(1, -1))


def _ssd_kernel(z_ref, xs_ref, xsh_ref, bc_ref, bch_ref, dt_ref, cwx_ref, cbx_ref, cwb_ref, cbb_ref,
                dtb_ref, alog_ref, dsk_ref, gn_ref, y_ref, extx_ref, extb_ref, state_ref):
    L = SSM_CHUNK
    GW = SSM_INNER // SSM_GROUPS
    first = pl.program_id(1) == 0

    @pl.when(first)
    def _():
        state_ref[...] = jnp.zeros_like(state_ref)

    xs = _silu(_dwconv(extx_ref, xs_ref[...], xsh_ref[...], first, cwx_ref, SSM_CONV_WIDTH) + cbx_ref[...])
    bc = _silu(_dwconv(extb_ref, bc_ref[...], bch_ref[...], first, cwb_ref, SSM_CONV_WIDTH) + cbb_ref[...])

    dt = _softplus(dt_ref[...] + dtb_ref[...])
    a = -jnp.exp(alog_ref[...])
    la = dt * a
    row = lax.broadcasted_iota(jnp.int32, (L, L), 0)
    col = lax.broadcasted_iota(jnp.int32, (L, L), 1)
    causal = col <= row
    cum = _dot_f32_lhs01(causal.astype(BF16), la)
    cum_t = cum.T
    clast = cum[L - 1:L, :]

    e_head = (lax.broadcasted_iota(jnp.int32, (LANES, SSM_INNER), 1) // SSM_HEADDIM
              == lax.broadcasted_iota(jnp.int32, (LANES, SSM_INNER), 0)).astype(BF16)
    x = xs * _dot_f32_rhs01(dt, e_head)
    ecum_x = _dot_f32_rhs01(jnp.exp(cum), e_head)
    dte_x = _dot_f32_rhs01(jnp.exp(clast - cum), e_head)
    cdec_x = _dot_f32_rhs01(jnp.broadcast_to(jnp.exp(clast), (SUBLANES, LANES)), e_head)[0:1, :]
    xb = x.astype(BF16)
    xdte = (x * dte_x).astype(BF16)
    lo_half = lax.broadcasted_iota(jnp.int32, (L, LANES), 1) < SSM_HEADDIM

    for g in range(SSM_GROUPS):
        bg = bc[:, g * SSM_STATE:(g + 1) * SSM_STATE]
        cg = bc[:, (SSM_GROUPS + g) * SSM_STATE:(SSM_GROUPS + g + 1) * SSM_STATE].astype(BF16)
        cb = _dot_nt(cg, bg.astype(BF16))
        hg = state_ref[g]
        y_off = _dot(cg, hg.astype(BF16)) * ecum_x[:, g * GW:(g + 1) * GW]
        pieces = []
        for pr in range(GW // LANES):
            h0 = g * (GW // SSM_HEADDIM) + 2 * pr
            xp = xb[:, h0 * SSM_HEADDIM:h0 * SSM_HEADDIM + LANES]
            ypair = None
            for s in range(2):
                h = h0 + s
                diff = cum[:, h:h + 1] - cum_t[h:h + 1, :]
                decay = jnp.exp(jnp.where(causal, diff, -jnp.inf))
                m = (cb * decay).astype(BF16)
                keep = lo_half if s == 0 else jnp.logical_not(lo_half)
                yh = _dot(m, jnp.where(keep, xp, jnp.zeros_like(xp)))
                ypair = yh if ypair is None else ypair + yh
            pieces.append(ypair)
        y_diag = jnp.concatenate(pieces, axis=-1)
        st = _dot(bg.T.astype(BF16), xdte[:, g * GW:(g + 1) * GW])
        state_ref[g] = hg * cdec_x[:, g * GW:(g + 1) * GW] + st

        yg = y_diag + y_off + dsk_ref[:, g * GW:(g + 1) * GW] * xs[:, g * GW:(g + 1) * GW]
        yg = yg * _silu(z_ref[:, g * GW:(g + 1) * GW])
        ms = jnp.mean(yg * yg, axis=-1, keepdims=True)
        yg = yg * lax.rsqrt(ms + NORM_EPS) * gn_ref[:, g * GW:(g + 1) * GW]
        y_ref[:, g * GW:(g + 1) * GW] = yg.astype(y_ref.dtype)


def ssd_mixer(proj, dt_raw, conv_w, conv_b, dt_bias, a_log, d_skip, ssm_norm, bsz, seq):
    L = SSM_CHUNK
    nc = seq // L
    hb = L // SUBLANES
    pad = lambda v: jnp.pad(v.reshape(1, -1), ((0, 0), (0, LANES - v.shape[-1])))
    rowblk = lambda b, c: b * nc + c
    halo_row = lambda b, c: jnp.maximum((b * nc + c) * hb - 1, 0)
    const = lambda b, c: (0, 0)
    bcw = 2 * SSM_GROUPS * SSM_STATE
    return pl.pallas_call(
        _ssd_kernel,
        grid=(bsz, nc),
        in_specs=[pl.BlockSpec((L, SSM_INNER), lambda b, c: (rowblk(b, c), 0)),
                  pl.BlockSpec((L, SSM_INNER), lambda b, c: (rowblk(b, c), 1)),
                  pl.BlockSpec((SUBLANES, SSM_INNER), lambda b, c: (halo_row(b, c), 1)),
                  pl.BlockSpec((L, bcw), lambda b, c: (rowblk(b, c), 4)),
                  pl.BlockSpec((SUBLANES, bcw), lambda b, c: (halo_row(b, c), 4)),
                  pl.BlockSpec((L, LANES), lambda b, c: (rowblk(b, c), 0)),
                  pl.BlockSpec((SSM_CONV_WIDTH, SSM_INNER), const),
                  pl.BlockSpec((1, SSM_INNER), const),
                  pl.BlockSpec((SSM_CONV_WIDTH, bcw), const),
                  pl.BlockSpec((1, bcw), const),
                  pl.BlockSpec((1, LANES), const),
                  pl.BlockSpec((1, LANES), const),
                  pl.BlockSpec((1, SSM_INNER), const),
                  pl.BlockSpec((1, SSM_INNER), const)],
        out_specs=pl.BlockSpec((L, SSM_INNER), lambda b, c: (rowblk(b, c), 0)),
        out_shape=jax.ShapeDtypeStruct((bsz * seq, SSM_INNER), BF16),
        scratch_shapes=[pltpu.VMEM((L + SUBLANES, SSM_INNER), F32),
                        pltpu.VMEM((L + SUBLANES, bcw), F32),
                        pltpu.VMEM((SSM_GROUPS, SSM_STATE, SSM_INNER // SSM_GROUPS), F32)],
        compiler_params=_cparams("parallel", "arbitrary"),
    )(proj, proj, proj, proj, proj, dt_raw,
      conv_w[:, :SSM_INNER], conv_b[:SSM_INNER].reshape(1, -1),
      conv_w[:, SSM_INNER:], conv_b[SSM_INNER:].reshape(1, -1),
      pad(dt_bias), pad(a_log), jnp.repeat(d_skip, SSM_HEADDIM).reshape(1, -1), ssm_norm.reshape(1, -1))


def _conformer_kernel(ua_ref, ug_ref, uah_ref, ugh_ref, w_ref, b_ref, lg_ref, lb_ref, o_ref, ext_ref,
                      shift_ref, conv_ref, *, tiles_per_seq):
    first = (pl.program_id(0) % tiles_per_seq) == 0
    tm = ua_ref.shape[0]
    ext_ref[0:CONF_HALO, :] = jnp.where(first, 0.0, uah_ref[...] * jax.nn.sigmoid(ugh_ref[...]))
    ext_ref[CONF_HALO:, :] = ua_ref[...] * jax.nn.sigmoid(ug_ref[...])
    span = tm + CONF_HALO - SUBLANES
    for s in range(1, SUBLANES):
        shift_ref[s - 1] = ext_ref[s:s + span, :]
    for c0 in range(0, CONF_WIDTH, LANES):
        cs = slice(c0, c0 + LANES)
        acc = jnp.broadcast_to(b_ref[:, cs], (tm, LANES))
        for k in range(CONF_CONV_WIDTH):
            q, s = divmod(CONF_HALO - (CONF_CONV_WIDTH - 1) + k, SUBLANES)
            rows = slice(q * SUBLANES, q * SUBLANES + tm)
            tap = ext_ref[rows, cs] if s == 0 else shift_ref[s - 1, rows, cs]
            acc = acc + w_ref[k:k + 1, cs] * tap
        conv_ref[:, cs] = acc
    c = conv_ref[...]
    mu = jnp.mean(c, axis=-1, keepdims=True)
    d = c - mu
    var = jnp.mean(d * d, axis=-1, keepdims=True)
    y = d * lax.rsqrt(var + NORM_EPS) * lg_ref[...] + lb_ref[...]
    o_ref[...] = _silu(y).astype(o_ref.dtype)


def conformer_mixer(proj, dw_w, dw_b, ln_g, ln_b, seq, *, tm=TM_CONF):
    t = proj.shape[0]
    hb = tm // CONF_HALO
    halo_row = lambda i: jnp.maximum(i * hb - 1, 0)
    const = lambda i: (0, 0)
    return pl.pallas_call(
        functools.partial(_conformer_kernel, tiles_per_seq=seq // tm),
        grid=(t // tm,),
        in_specs=[pl.BlockSpec((tm, CONF_WIDTH), lambda i: (i, 5)),
                  pl.BlockSpec((tm, CONF_WIDTH), lambda i: (i, 6)),
                  pl.BlockSpec((CONF_HALO, CONF_WIDTH), lambda i: (halo_row(i), 5)),
                  pl.BlockSpec((CONF_HALO, CONF_WIDTH), lambda i: (halo_row(i), 6)),
                  pl.BlockSpec((CONF_CONV_WIDTH, CONF_WIDTH), const),
                  pl.BlockSpec((1, CONF_WIDTH), const),
                  pl.BlockSpec((1, CONF_WIDTH), const),
                  pl.BlockSpec((1, CONF_WIDTH), const)],
        out_specs=pl.BlockSpec((tm, CONF_WIDTH), lambda i: (i, 0)),
        out_shape=jax.ShapeDtypeStruct((t, CONF_WIDTH), BF16),
        scratch_shapes=[pltpu.VMEM((tm + CONF_HALO, CONF_WIDTH), F32),
                        pltpu.VMEM((SUBLANES - 1, tm + CONF_HALO - SUBLANES, CONF_WIDTH), F32),
                        pltpu.VMEM((tm, CONF_WIDTH), F32)],
        compiler_params=_cparams("parallel"),
    )(proj, proj, proj, proj, dw_w, dw_b.reshape(1, -1), ln_g.reshape(1, -1), ln_b.reshape(1, -1))


def _shortconv_kernel(b_ref, c_ref, h_ref, ch_ref, hh_ref, w_ref, o_ref, ext_ref, *, tiles_per_seq):
    first = (pl.program_id(0) % tiles_per_seq) == 0
    conv = _dwconv(ext_ref, c_ref[...] * h_ref[...], ch_ref[...] * hh_ref[...], first, w_ref, SC_CONV_WIDTH)
    o_ref[...] = (b_ref[...] * conv).astype(o_ref.dtype)


def shortconv_mixer(proj, conv_w, seq, *, tm=TM_SC):
    t = proj.shape[0]
    hb = tm // SUBLANES
    halo_row = lambda i: jnp.maximum(i * hb - 1, 0)
    return pl.pallas_call(
        functools.partial(_shortconv_kernel, tiles_per_seq=seq // tm),
        grid=(t // tm,),
        in_specs=[pl.BlockSpec((tm, SC_WIDTH), lambda i: (i, 0)),
                  pl.BlockSpec((tm, SC_WIDTH), lambda i: (i, 1)),
                  pl.BlockSpec((tm, SC_WIDTH), lambda i: (i, 2)),
                  pl.BlockSpec((SUBLANES, SC_WIDTH), lambda i: (halo_row(i), 1)),
                  pl.BlockSpec((SUBLANES, SC_WIDTH), lambda i: (halo_row(i), 2)),
                  pl.BlockSpec((SC_CONV_WIDTH, SC_WIDTH), lambda i: (0, 0))],
        out_specs=pl.BlockSpec((tm, SC_WIDTH), lambda i: (i, 0)),
        out_shape=jax.ShapeDtypeStruct((t, SC_WIDTH), BF16),
        scratch_shapes=[pltpu.VMEM((tm + SUBLANES, SC_WIDTH), F32)],
        compiler_params=_cparams("parallel"),
    )(proj, proj, proj, proj, proj, conv_w)


def _rope_table_kernel(pos_ref, inv_ref, cos_ref, sin_ref):
    ang = pos_ref[...].astype(F32) * inv_ref[...]
    sign = jnp.where(lax.broadcasted_iota(jnp.int32, ang.shape, 1) < NSA_HEAD_DIM // 2, -1.0, 1.0)
    cos_ref[...] = jnp.cos(ang)
    sin_ref[...] = jnp.sin(ang) * sign


def rope_tables(positions, *, tm=TM_ROPE):
    t = positions.size
    half = NSA_HEAD_DIM // 2
    inv = ROPE_THETA ** (-jnp.arange(half, dtype=F32) / half)
    inv = jnp.concatenate([inv, inv]).reshape(1, NSA_HEAD_DIM)
    tab = jax.ShapeDtypeStruct((t, NSA_HEAD_DIM), F32)
    return pl.pallas_call(
        _rope_table_kernel,
        grid=(t // tm,),
        in_specs=[pl.BlockSpec((tm, 1), lambda i: (i, 0)), pl.BlockSpec((1, NSA_HEAD_DIM), lambda i: (0, 0))],
        out_specs=[pl.BlockSpec((tm, NSA_HEAD_DIM), lambda i: (i, 0))] * 2,
        out_shape=[tab, tab],
        compiler_params=_cparams("parallel"),
    )(positions.reshape(t, 1), inv)


def _rope(x, cos, sin_signed):
    return x * cos + pltpu.roll(x, NSA_HEAD_DIM // 2, axis=1) * sin_signed


def _kv_prep_kernel(ks_ref, vs_ref, kw_ref, vw_ref, cos_ref, sin_ref, kso_ref, vso_ref, kwo_ref, vwo_ref,
                    *, tiles_per_seq):
    D = NSA_HEAD_DIM
    tm = ks_ref.shape[0]
    cos, sin = cos_ref[...], sin_ref[...]
    row = (pl.program_id(0) % tiles_per_seq) * tm + lax.broadcasted_iota(jnp.int32, (tm, D), 0)
    lane = lax.broadcasted_iota(jnp.int32, (tm, D), 1)
    onehot = jnp.where(row // SLC_BLOCK == lane, 1.0, 0.0).astype(BF16)
    ones_col = jnp.where(lane == 0, 1.0, 0.0).astype(BF16)
    for g in range(NSA_KV_HEADS):
        sl = slice(g * D, (g + 1) * D)
        kso_ref[:, 2 * g * D:(2 * g + 1) * D] = _rope(ks_ref[:, sl], cos, sin).astype(BF16)
        kso_ref[:, (2 * g + 1) * D:(2 * g + 2) * D] = onehot
        kwo_ref[:, sl] = _rope(kw_ref[:, sl], cos, sin).astype(BF16)
        vso_ref[:, 2 * g * D:(2 * g + 1) * D] = vs_ref[:, sl].astype(BF16)
        vso_ref[:, (2 * g + 1) * D:(2 * g + 2) * D] = ones_col
        vwo_ref[:, 2 * g * D:(2 * g + 1) * D] = vw_ref[:, sl].astype(BF16)
        vwo_ref[:, (2 * g + 1) * D:(2 * g + 2) * D] = ones_col


def kv_prep(proj, cos, sin, seq, *, tm=512):
    t = proj.shape[0]
    kv = jax.ShapeDtypeStruct((t, NSA_KV), BF16)
    kv_aug = jax.ShapeDtypeStruct((t, 2 * NSA_KV), BF16)
    spec = pl.BlockSpec((tm, NSA_KV), lambda i: (i, 0))
    spec_aug = pl.BlockSpec((tm, 2 * NSA_KV), lambda i: (i, 0))
    return pl.pallas_call(
        functools.partial(_kv_prep_kernel, tiles_per_seq=seq // tm),
        grid=(t // tm,),
        in_specs=[pl.BlockSpec((tm, NSA_KV), lambda i: (i, 12)),
                  pl.BlockSpec((tm, NSA_KV), lambda i: (i, 13)),
                  pl.BlockSpec((tm, NSA_KV), lambda i: (i, 14)),
                  pl.BlockSpec((tm, NSA_KV), lambda i: (i, 15)),
                  pl.BlockSpec((tm, NSA_HEAD_DIM), lambda i: (i, 0)),
                  pl.BlockSpec((tm, NSA_HEAD_DIM), lambda i: (i, 0))],
        out_specs=[spec_aug, spec_aug, spec, spec_aug],
        out_shape=[kv_aug, kv_aug, kv, kv_aug],
        compiler_params=_cparams("parallel"),
    )(proj, proj, proj, proj, cos, sin)


def _compress_kernel(kt_ref, vt_ref, cos_ref, sin_ref, pe_ref, w1_ref, w2_ref, kc_ref, vc_ref):
    nb = kc_ref.shape[0]
    half = CMP_BLOCK // 2

    def compress(tok_ref, which):
        acc_lo = jnp.zeros((nb, NSA_HEAD_DIM), F32)
        acc_hi = jnp.zeros((nb, NSA_HEAD_DIM), F32)
        for l in range(half):
            r = tok_ref[pl.ds(l, nb, stride=CMP_STRIDE), :]
            lo = (r + pe_ref[which, l:l + 1, :]).astype(BF16)
            hi = (r + pe_ref[which, half + l:half + l + 1, :]).astype(BF16)
            acc_lo = acc_lo + _dot(lo, w1_ref[which, l * NSA_HEAD_DIM:(l + 1) * NSA_HEAD_DIM, :])
            acc_hi = acc_hi + _dot(hi, w1_ref[which, (half + l) * NSA_HEAD_DIM:(half + l + 1) * NSA_HEAD_DIM, :])
        pre = acc_lo + pltpu.roll(acc_hi, nb - 1, axis=0)
        return _dot(_silu(pre).astype(BF16), w2_ref[which])

    cos_e = pltpu.roll(cos_ref[pl.ds(CMP_STRIDE - 1, nb, stride=CMP_STRIDE), :], nb - 1, axis=0)
    sin_e = pltpu.roll(sin_ref[pl.ds(CMP_STRIDE - 1, nb, stride=CMP_STRIDE), :], nb - 1, axis=0)
    kc_ref[...] = _rope(compress(kt_ref, 0), cos_e, sin_e).astype(BF16)
    vc_ref[...] = compress(vt_ref, 1).astype(BF16)


def compress_kv(proj, cos, sin, pe, w1, w2, bsz, seq):
    nb = seq // CMP_STRIDE
    out = jax.ShapeDtypeStruct((bsz, NSA_KV_HEADS, nb, NSA_HEAD_DIM), BF16)
    full = lambda *shape: pl.BlockSpec(shape, lambda b, g: (0,) * len(shape))
    return pl.pallas_call(
        _compress_kernel,
        grid=(bsz, NSA_KV_HEADS),
        in_specs=[pl.BlockSpec((seq, NSA_HEAD_DIM), lambda b, g: (b, 40 + g)),
                  pl.BlockSpec((seq, NSA_HEAD_DIM), lambda b, g: (b, 44 + g)),
                  pl.BlockSpec((seq, NSA_HEAD_DIM), lambda b, g: (b, 0)),
                  pl.BlockSpec((seq, NSA_HEAD_DIM), lambda b, g: (b, 0)),
                  full(2, CMP_BLOCK, NSA_HEAD_DIM),
                  full(2, CMP_BLOCK * NSA_HEAD_DIM, NSA_HEAD_DIM),
                  full(2, NSA_HEAD_DIM, NSA_HEAD_DIM)],
        out_specs=[pl.BlockSpec((None, None, nb, NSA_HEAD_DIM), lambda b, g: (b, g, 0, 0))] * 2,
        out_shape=[out, out],
        compiler_params=_cparams("parallel", "parallel"),
    )(proj, proj, cos, sin, pe, w1, w2)


def _nsa_kernel(q_ref, cos_ref, sin_ref, gate_ref, kc_ref, vc_ref, ks_ref, vs_ref, kw_ref, vw_ref,
                o_ref, qa_ref, impt_ref, sc_ref, mrun_ref, acc_ref):
    G = NSA_GROUP
    D = NSA_HEAD_DIM
    R = G * TQ
    base = pl.program_id(2) * (NSA_TILES * TQ)
    scale = D ** -0.5
    n_cmp = kc_ref.shape[0]
    n_slc = ks_ref.shape[0] // SLC_BLOCK
    wk = WINDOW + TQ
    j = lax.broadcasted_iota(jnp.int32, (1, LANES), 1)
    ends = lax.broadcasted_iota(jnp.int32, (1, n_cmp), 1) * CMP_STRIDE + (CMP_BLOCK - 1)
    n_i = lax.broadcasted_iota(jnp.int32, (n_cmp, LANES), 0)
    j_i = lax.broadcasted_iota(jnp.int32, (n_cmp, LANES), 1)
    overlap = ((n_i * CMP_STRIDE < (j_i + 1) * SLC_BLOCK)
               & (n_i * CMP_STRIDE + CMP_BLOCK - 1 >= j_i * SLC_BLOCK)).astype(BF16)
    rel = lax.broadcasted_iota(jnp.int32, (TQ, wk), 1) - lax.broadcasted_iota(jnp.int32, (TQ, wk), 0)
    lower = (lax.broadcasted_iota(jnp.int32, (TQ, TQ), 1) <= lax.broadcasted_iota(jnp.int32, (TQ, TQ), 0))
    own_bias = jnp.where(lower, 0.0, NEG_BIG)
    sub = lax.broadcasted_iota(jnp.int32, (SUBLANES, TQ), 0)

    o_cmp, o_win, s_own = [], [], []
    for u in range(NSA_TILES):
        rows = slice(u * TQ, (u + 1) * TQ)
        q0 = base + u * TQ
        cos, sin = cos_ref[rows, :], sin_ref[rows, :]
        for e in range(G):
            xq = _rope(q_ref[rows, e * D:(e + 1) * D], cos, sin)
            qa_ref[u, e * TQ:(e + 1) * TQ, 0:D] = (xq * scale).astype(BF16)
        qs = qa_ref[u, :, 0:D]
        t_col = q0 + lax.broadcasted_iota(jnp.int32, (TQ, 1), 0)

        cvalid = ends <= t_col
        cbias = jnp.where(cvalid, 0.0, NEG_BIG)
        c01 = jnp.where(cvalid, 1.0, 0.0)
        s = _dot_nt(qs, kc_ref[...]).reshape(G, TQ, n_cmp) + cbias[None]
        m = jnp.max(s, axis=-1, keepdims=True)
        p = jnp.exp(s - m) * c01[None]
        l = jnp.sum(p, axis=-1, keepdims=True)
        p = p * (1.0 / jnp.maximum(l, 1e-30))
        o_cmp.append(_dot(p.reshape(R, n_cmp).astype(BF16), vc_ref[...]))

        start = pl.multiple_of(jnp.maximum(q0 - WINDOW, 0), TQ)
        d = q0 - start
        wbias = jnp.where((rel <= d) & (rel > d - WINDOW), 0.0, NEG_BIG)
        sw = _dot_nt(qs, kw_ref[pl.ds(start, wk), :]).reshape(G, TQ, wk) + wbias[None]
        mw = jnp.max(sw, axis=-1, keepdims=True)
        pw = jnp.exp(sw - mw).reshape(R, wk)
        accw = _dot(pw.astype(BF16), vw_ref[pl.ds(start, wk), :])
        o_win.append(accw[:, 0:D] * (1.0 / accw[:, D:D + 1]))

        psum = p[0]
        for e in range(1, G):
            psum = psum + p[e]
        imp = _dot_f32_rhs01(psum, overlap)
        cur = t_col // SLC_BLOCK
        forced = (j == 0) | ((j <= cur) & (j > cur - N_LOCAL))
        valid = j * SLC_BLOCK <= t_col
        imp = jnp.where(forced, FORCE, jnp.where(valid, imp, -FORCE))
        impt_ref[u] = imp.T
        nrb = n_slc // SUBLANES
        vals = [impt_ref[u, rb * SUBLANES:(rb + 1) * SUBLANES, :] for rb in range(nrb)]
        cnt = [jnp.zeros((SUBLANES, TQ), F32) for _ in range(nrb)]
        for k in range(n_slc):
            vk = jnp.broadcast_to(impt_ref[u, k:k + 1, :], (SUBLANES, TQ))
            for rb in range(nrb):
                if rb * SUBLANES > k:
                    beats = vk >= vals[rb]
                elif rb * SUBLANES + SUBLANES - 1 < k:
                    beats = vk > vals[rb]
                else:
                    beats = (vk > vals[rb]) | ((vk == vals[rb]) & (sub > k - rb * SUBLANES))
                cnt[rb] = cnt[rb] + jnp.where(beats, 1.0, 0.0)
        sel_t = jnp.concatenate([jnp.where(c < N_SELECT, 1.0, 0.0) for c in cnt]
                                + [jnp.zeros((LANES - n_slc, TQ), F32)], axis=0)
        sel = sel_t.T
        sbias = jnp.where((sel > 0.5) & (j * SLC_BLOCK < q0), 0.0, NEG_BIG).astype(BF16)
        for e in range(G):
            qa_ref[u, e * TQ:(e + 1) * TQ, D:2 * D] = sbias

        s_own.append((_dot_nt(qs, ks_ref[pl.ds(q0, TQ), 0:D]).reshape(G, TQ, TQ)
                      + own_bias[None]).reshape(R, TQ))
        mrun_ref[u] = s_own[u]

    n_kt = (base + (NSA_TILES - 1) * TQ + TK - 1) // TK

    def score_body(kt, carry):
        k0 = pl.multiple_of(kt * TK, TK)
        for u in range(NSA_TILES):
            s = _dot_nt(qa_ref[u], ks_ref[pl.ds(k0, TK), :])
            sc_ref[u, kt] = s
            t = s[:, 0:LANES]
            for c in range(1, TK // LANES):
                t = jnp.maximum(t, s[:, c * LANES:(c + 1) * LANES])
            mrun_ref[u] = jnp.maximum(mrun_ref[u], t)
        return carry

    lax.fori_loop(0, n_kt, score_body, 0)
    m_row = []
    for u in range(NSA_TILES):
        m_row.append(jnp.max(mrun_ref[u], axis=-1, keepdims=True))
        acc_ref[u] = _dot(jnp.exp(s_own[u] - m_row[u]).astype(BF16),
                          vs_ref[pl.ds(base + u * TQ, TQ), :])

    def pv_body(kt, carry):
        k0 = pl.multiple_of(kt * TK, TK)
        for u in range(NSA_TILES):
            p = jnp.exp(sc_ref[u, kt] - m_row[u]).astype(BF16)
            acc_ref[u] += _dot(p, vs_ref[pl.ds(k0, TK), :])
        return carry

    lax.fori_loop(0, n_kt, pv_body, 0)

    for u in range(NSA_TILES):
        rows = slice(u * TQ, (u + 1) * TQ)
        acc = acc_ref[u]
        o_slc = acc[:, 0:D] * (1.0 / acc[:, D:D + 1])
        gate = jax.nn.sigmoid(gate_ref[rows, :])
        for e in range(G):
            hr = slice(e * TQ, (e + 1) * TQ)
            o = (gate[:, 3 * e:3 * e + 1] * o_cmp[u][hr] + gate[:, 3 * e + 1:3 * e + 2] * o_slc[hr]
                 + gate[:, 3 * e + 2:3 * e + 3] * o_win[u][hr])
            o_ref[rows, e * D:(e + 1) * D] = o.astype(o_ref.dtype)


def nsa_attention(proj, gates, cos, sin, kc, vc, ks, vs, kw, vw, bsz, seq):
    tq = NSA_TILES * TQ
    nq = seq // tq
    qw = NSA_GROUP * NSA_HEAD_DIM
    rows = NSA_GROUP * TQ
    row = lambda b, g, i: b * nq + i
    kvspec = pl.BlockSpec((seq, NSA_HEAD_DIM), lambda b, g, i: (b, g))
    augspec = pl.BlockSpec((seq, 2 * NSA_HEAD_DIM), lambda b, g, i: (b, g))
    cspec = pl.BlockSpec((None, None, seq // CMP_STRIDE, NSA_HEAD_DIM), lambda b, g, i: (b, g, 0, 0))
    return pl.pallas_call(
        _nsa_kernel,
        grid=(bsz, NSA_KV_HEADS, nq),
        in_specs=[pl.BlockSpec((tq, qw), lambda b, g, i: (row(b, g, i), 6 + g)),
                  pl.BlockSpec((tq, NSA_HEAD_DIM), lambda b, g, i: (row(b, g, i), 0)),
                  pl.BlockSpec((tq, NSA_HEAD_DIM), lambda b, g, i: (row(b, g, i), 0)),
                  pl.BlockSpec((tq, LANES), lambda b, g, i: (row(b, g, i), g)),
                  cspec, cspec, augspec, augspec, kvspec, augspec],
        out_specs=pl.BlockSpec((tq, qw), lambda b, g, i: (row(b, g, i), g)),
        out_shape=jax.ShapeDtypeStruct((bsz * seq, NSA_Q), BF16),
        scratch_shapes=[pltpu.VMEM((NSA_TILES, rows, 2 * NSA_HEAD_DIM), BF16),
                        pltpu.VMEM((NSA_TILES, LANES, TQ), F32),
                        pltpu.VMEM((NSA_TILES, seq // TK, rows, TK), F32),
                        pltpu.VMEM((NSA_TILES, rows, LANES), F32),
                        pltpu.VMEM((NSA_TILES, rows, 2 * NSA_HEAD_DIM), F32)],
        compiler_params=_cparams("parallel", "parallel", "parallel"),
    )(proj, cos, sin, gates, kc, vc, ks, vs, kw, vw)


def _even_layer(h, gain, w_in, conv_w, conv_b, dt_bias, a_log, d_skip, ssm_norm,
                dw_w, dw_b, ln_g, ln_b, w_out, i, bsz, seq):
    xbc_end = SSM_INNER + SSM_INNER + 2 * SSM_GROUPS * SSM_STATE
    dt_end = xbc_end + SSM_HEADS
    w_main = jnp.concatenate([w_in[:, :xbc_end], w_in[:, dt_end:]], axis=1).astype(BF16)
    w_dt = jnp.pad(w_in[:, xbc_end:dt_end], ((0, 0), (0, LANES - SSM_HEADS))).astype(BF16)
    proj = norm_matmul(h, gain, w_main)
    dt_raw = norm_matmul(h, gain, w_dt)
    y = ssd_mixer(proj, dt_raw, conv_w, conv_b, dt_bias, a_log, d_skip, ssm_norm, bsz, seq)
    u = conformer_mixer(proj, dw_w, dw_b, ln_g, ln_b, seq)
    return matmul_res([(y, 0, 0, SSM_INNER), (u, 0, SSM_INNER // CONF_WIDTH, CONF_WIDTH)], w_out, i, h)


def _odd_layer(h, gain, cos, sin, w_in, w_gates, sc_w, pe, w1, w2, w_out, i, bsz, seq):
    main = 3 * SC_WIDTH + NSA_Q + 6 * NSA_KV
    per_g = 3 * NSA_GROUP
    wg = w_gates.reshape(D_MODEL, NSA_KV_HEADS, per_g)
    wg = jnp.pad(wg, ((0, 0), (0, 0), (0, LANES - per_g))).reshape(D_MODEL, NSA_KV_HEADS * LANES)
    proj = norm_matmul(h, gain, w_in, layer=i, n=main)
    gates = norm_matmul(h, gain, wg.astype(BF16), tn=NSA_KV_HEADS * LANES)
    y_c = shortconv_mixer(proj, sc_w, seq)
    ks, vs, kw, vw = kv_prep(proj, cos, sin, seq)
    kc, vc = compress_kv(proj, cos, sin, pe, w1, w2, bsz, seq)
    y_d = nsa_attention(proj, gates, cos, sin, kc, vc, ks, vs, kw, vw, bsz, seq)
    return matmul_res([(y_c, 0, 0, SC_WIDTH), (y_d, 0, 1, SC_WIDTH), (y_d, 1, 2, SC_WIDTH)], w_out, i, h)


def _ffn(h, gain, w_up, conv_w, conv_b, w_down, layer, seq):
    a = ffn_up_gate(h, gain, w_up, layer, conv_w, conv_b, seq)
    return matmul_res([(a, 0, 0, D_FF)], w_down, layer, h, tn=TN_DOWN)


def kernel(x, positions, mix_norm, ffn_norm, final_norm, ab_w_in, ssm_conv_w, ssm_conv_b, ssm_dt_bias, ssm_a_log, ssm_d, ssm_norm, conf_dw_w, conf_dw_b, conf_ln_g, conf_ln_b, ab_w_out, cd_w_in, sc_conv_w, nsa_cmp_pe, nsa_cmp_w1, nsa_cmp_w2, cd_w_out, ffn_w_up, ffn_conv_w, ffn_conv_b, ffn_w_down):
    bsz, seq, d = x.shape
    depth = mix_norm.shape[0]
    h = x.reshape(bsz * seq, d)
    cos, sin = rope_tables(positions)
    ab_w_out_b, cd_w_in_b, cd_w_out_b = ab_w_out.astype(BF16), cd_w_in.astype(BF16), cd_w_out.astype(BF16)
    ffn_w_up_b, ffn_w_down_b = ffn_w_up.astype(BF16), ffn_w_down.astype(BF16)
    cmp_w1_b, cmp_w2_b = nsa_cmp_w1.astype(BF16), nsa_cmp_w2.astype(BF16)
    gate_cols = 3 * SC_WIDTH + NSA_Q + 6 * NSA_KV
    for layer in range(depth):
        i = layer // 2
        if layer % 2 == 0:
            h = _even_layer(h, mix_norm[layer], ab_w_in[i], ssm_conv_w[i], ssm_conv_b[i], ssm_dt_bias[i],
                            ssm_a_log[i], ssm_d[i], ssm_norm[i], conf_dw_w[i], conf_dw_b[i], conf_ln_g[i],
                            conf_ln_b[i], ab_w_out_b, i, bsz, seq)
        else:
            h = _odd_layer(h, mix_norm[layer], cos, sin, cd_w_in_b, cd_w_in[i][:, gate_cols:], sc_conv_w[i],
                           nsa_cmp_pe[i], cmp_w1_b[i], cmp_w2_b[i], cd_w_out_b, i, bsz, seq)
        h = _ffn(h, ffn_norm[layer], ffn_w_up_b, ffn_conv_w[layer], ffn_conv_b[layer], ffn_w_down_b, layer, seq)
    return rmsnorm_rows(h, final_norm).reshape(bsz, seq, d)
```

```python
import functools

import jax
import jax.numpy as jnp
from jax import lax
from jax.experimental import pallas as pl
from jax.experimental.pallas import tpu as pltpu

F32 = jnp.float32
BF16 = jnp.bfloat16

D_MODEL = 2048
NORM_EPS = 1e-6
SSM_INNER = 2048
SSM_HEADDIM = 64
SSM_HEADS = 32
SSM_GROUPS = 4
SSM_STATE = 128
SSM_CONV_WIDTH = 4
SSM_CHUNK = 128
CONF_WIDTH = 1024
CONF_CONV_WIDTH = 31
SC_WIDTH = 1024
SC_CONV_WIDTH = 3
NSA_HEAD_DIM = 128
NSA_HEADS = 16
NSA_KV_HEADS = 4
NSA_GROUP = 4
CMP_BLOCK = 32
CMP_STRIDE = 16
SLC_BLOCK = 64
N_SELECT = 16
N_LOCAL = 2
WINDOW = 512
ROPE_THETA = 10000.0
NSA_Q = 2048
NSA_KV = 512
D_FF = 5632
FFN_CONV_WIDTH = 3
NEG_BIG = -1e30
FORCE = 1e9

V7X_VMEM_BYTES = 64 * 1024 * 1024
VMEM_LIMIT = V7X_VMEM_BYTES - 8 * 1024 * 1024
LANES = 128
SUBLANES = 8

TM_PROJ = 1024
TN_PROJ = 1024
TM_OUT = 512
TN_OUT = 2048
TN_DOWN = 1024
TN_FFN = 512
FFN_HALO = 16
FFN_CHUNKS = 2
TM_CONF = 256
CONF_HALO = 32
TM_SC = 512
TM_ROPE = 1024
TQ = 128
TK = 1024
NSA_TILES = 2


def _cparams(*sem):
    return pltpu.CompilerParams(dimension_semantics=sem, vmem_limit_bytes=VMEM_LIMIT)


def _silu(x):
    return x * jax.nn.sigmoid(x)


def _softplus(x):
    return jnp.maximum(x, 0.0) + jnp.log1p(jnp.exp(-jnp.abs(x)))


def _split3(x):
    hi = x.astype(BF16)
    r1 = x - hi.astype(F32)
    mid = r1.astype(BF16)
    lo = (r1 - mid.astype(F32)).astype(BF16)
    return hi, mid, lo


def _dot(a, b):
    return jnp.dot(a, b, preferred_element_type=F32)


def _dot_nt(a, b):
    return lax.dot_general(a, b, (((1,), (1,)), ((), ())), preferred_element_type=F32)


def _dot_f32_rhs01(x, e01):
    return _dot(jnp.concatenate(_split3(x), axis=1), jnp.concatenate([e01] * 3, axis=0))


def _dot_f32_lhs01(e01, x):
    return _dot(jnp.concatenate([e01] * 3, axis=1), jnp.concatenate(_split3(x), axis=0))


def _dwconv(ext_ref, x, halo, first, w_ref, width):
    rows = x.shape[0]
    hrows = halo.shape[0]
    ext_ref[0:hrows, :] = jnp.where(first, 0.0, halo)
    ext_ref[hrows:hrows + rows, :] = x
    acc = None
    for k in range(width):
        off = hrows - (width - 1) + k
        term = w_ref[k:k + 1, :] * ext_ref[off:off + rows, :]
        acc = term if acc is None else acc + term
    return acc


def _norm_matmul_kernel(h_ref, g_ref, w_ref, ws_ref, o_ref, os_ref, xn_ref):
    @pl.when(pl.program_id(1) == 0)
    def _():
        x = h_ref[...]
        ms = jnp.mean(x * x, axis=-1, keepdims=True)
        xn_ref[...] = (x * lax.rsqrt(ms + NORM_EPS) * g_ref[...]).astype(BF16)
        os_ref[...] = _dot(xn_ref[...], ws_ref[...])

    o_ref[...] = _dot(xn_ref[...], w_ref[...]).astype(o_ref.dtype)


def norm_matmul(h, gain, w, w_side, *, layer=None, n=None, tm=TM_PROJ, tn=TN_PROJ):
    t, k = h.shape
    ns = w_side.shape[1]
    if layer is None:
        n = w.shape[1]
        w_spec = pl.BlockSpec((k, tn), lambda i, j: (0, j))
    else:
        w_spec = pl.BlockSpec((None, k, tn), lambda i, j: (layer, 0, j))
    return pl.pallas_call(
        _norm_matmul_kernel,
        grid=(t // tm, n // tn),
        in_specs=[pl.BlockSpec((tm, k), lambda i, j: (i, 0)),
                  pl.BlockSpec((1, k), lambda i, j: (0, 0)),
                  w_spec,
                  pl.BlockSpec((k, ns), lambda i, j: (0, 0))],
        out_specs=[pl.BlockSpec((tm, tn), lambda i, j: (i, j)),
                   pl.BlockSpec((tm, ns), lambda i, j: (i, 0))],
        out_shape=[jax.ShapeDtypeStruct((t, n), F32), jax.ShapeDtypeStruct((t, ns), F32)],
        scratch_shapes=[pltpu.VMEM((tm, k), BF16)],
        compiler_params=_cparams("parallel", "arbitrary"),
    )(h, gain.reshape(1, k), w, w_side)


def _matmul_res_kernel(*refs, n_in):
    a_refs, w_refs = refs[:n_in], refs[n_in:2 * n_in]
    r_ref, o_ref = refs[2 * n_in], refs[2 * n_in + 1]
    acc = _dot(a_refs[0][...], w_refs[0][...])
    for a_ref, w_ref in zip(a_refs[1:], w_refs[1:]):
        acc = acc + _dot(a_ref[...], w_ref[...])
    o_ref[...] = r_ref[...] + acc


def matmul_res(terms, w, layer, res, *, tm=TM_OUT, tn=TN_OUT):
    t, n = res.shape
    in_specs = ([pl.BlockSpec((tm, kw), lambda j, i, ca=ca: (i, ca)) for _, ca, _, kw in terms]
                + [pl.BlockSpec((None, kw, tn), lambda j, i, rw=rw: (layer, rw, j)) for _, _, rw, kw in terms]
                + [pl.BlockSpec((tm, tn), lambda j, i: (i, j))])
    return pl.pallas_call(
        functools.partial(_matmul_res_kernel, n_in=len(terms)),
        grid=(n // tn, t // tm),
        in_specs=in_specs,
        out_specs=pl.BlockSpec((tm, tn), lambda j, i: (i, j)),
        out_shape=jax.ShapeDtypeStruct((t, n), F32),
        compiler_params=_cparams("parallel", "parallel"),
    )(*[a for a, _, _, _ in terms], *([w] * len(terms)), res)


def _rmsnorm_kernel(h_ref, g_ref, o_ref):
    x = h_ref[...]
    ms = jnp.mean(x * x, axis=-1, keepdims=True)
    o_ref[...] = x * lax.rsqrt(ms + NORM_EPS) * g_ref[...]


def rmsnorm_rows(h, gain, *, tm=512):
    t, k = h.shape
    return pl.pallas_call(
        _rmsnorm_kernel,
        grid=(t // tm,),
        in_specs=[pl.BlockSpec((tm, k), lambda i: (i, 0)), pl.BlockSpec((1, k), lambda i: (0, 0))],
        out_specs=pl.BlockSpec((tm, k), lambda i: (i, 0)),
        out_shape=jax.ShapeDtypeStruct((t, k), F32),
        compiler_params=_cparams("parallel"),
    )(h, gain.reshape(1, k))


def _ffn_up_kernel(h_ref, hh_ref, gain_ref, wg_ref, wv_ref, cwg_ref, cwv_ref, bg_ref, bv_ref, o_ref,
                   xn_ref, *, tiles_per_seq):
    tm, tn = o_ref.shape

    @pl.when(pl.program_id(1) == 0)
    def _():
        first = (pl.program_id(0) % tiles_per_seq) == 0

        def norm(x):
            ms = jnp.mean(x * x, axis=-1, keepdims=True)
            return x * lax.rsqrt(ms + NORM_EPS) * gain_ref[...]

        xn_ref[0:FFN_HALO, :] = jnp.where(first, 0.0, norm(hh_ref[...])).astype(BF16)
        xn_ref[FFN_HALO:, :] = norm(h_ref[...]).astype(BF16)

    def conv(u, cw_ref, b_ref, cs):
        acc = cw_ref[0:1, cs] * u
        for k in range(1, FFN_CONV_WIDTH):
            acc = pltpu.roll(acc, 1, axis=0) + cw_ref[k:k + 1, cs] * u
        return acc[FFN_HALO:FFN_HALO + tm, :] + b_ref[:, cs]

    cw = tn // FFN_CHUNKS
    for c in range(FFN_CHUNKS):
        cs = slice(c * cw, (c + 1) * cw)
        ug = _dot(xn_ref[...], wg_ref[:, cs])
        uv = _dot(xn_ref[...], wv_ref[:, cs])
        o_ref[:, cs] = (_silu(conv(ug, cwg_ref, bg_ref, cs)) * conv(uv, cwv_ref, bv_ref, cs)).astype(o_ref.dtype)


def ffn_up_gate(h, gain, w_up, layer, conv_w, conv_b, seq, *, tm=TM_PROJ, tn=TN_FFN):
    t, k = h.shape
    nf = D_FF // tn
    hb = tm // FFN_HALO
    return pl.pallas_call(
        functools.partial(_ffn_up_kernel, tiles_per_seq=seq // tm),
        grid=(t // tm, nf),
        in_specs=[pl.BlockSpec((tm, k), lambda i, j: (i, 0)),
                  pl.BlockSpec((FFN_HALO, k), lambda i, j: (jnp.maximum(i * hb - 1, 0), 0)),
                  pl.BlockSpec((1, k), lambda i, j: (0, 0)),
                  pl.BlockSpec((None, k, tn), lambda i, j: (layer, 0, j)),
                  pl.BlockSpec((None, k, tn), lambda i, j: (layer, 0, j + nf)),
                  pl.BlockSpec((FFN_CONV_WIDTH, tn), lambda i, j: (0, j)),
                  pl.BlockSpec((FFN_CONV_WIDTH, tn), lambda i, j: (0, j + nf)),
                  pl.BlockSpec((1, tn), lambda i, j: (0, j)),
                  pl.BlockSpec((1, tn), lambda i, j: (0, j + nf))],
        out_specs=pl.BlockSpec((tm, tn), lambda i, j: (i, j)),
        out_shape=jax.ShapeDtypeStruct((t, D_FF), BF16),
        scratch_shapes=[pltpu.VMEM((tm + FFN_HALO, k), BF16)],
        compiler_params=_cparams("parallel", "arbitrary"),
    )(h, h, gain.reshape(1, k), w_up, w_up, conv_w, conv_w, conv_b.reshape(1, -1), conv_b.reshape(1, -1))


def _ssd_kernel(z_ref, xs_ref, xsh_ref, bc_ref, bch_ref, dt_ref, cwx_ref, cbx_ref, cwb_ref, cbb_ref,
                dtb_ref, alog_ref, dsk_ref, gn_ref, y_ref, extx_ref, extb_ref, state_ref):
    L = SSM_CHUNK
    GW = SSM_INNER // SSM_GROUPS
    first = pl.program_id(1) == 0

    @pl.when(first)
    def _():
        state_ref[...] = jnp.zeros_like(state_ref)

    xs = _silu(_dwconv(extx_ref, xs_ref[...], xsh_ref[...], first, cwx_ref, SSM_CONV_WIDTH) + cbx_ref[...])
    bc = _silu(_dwconv(extb_ref, bc_ref[...], bch_ref[...], first, cwb_ref, SSM_CONV_WIDTH) + cbb_ref[...])

    dt = _softplus(dt_ref[...] + dtb_ref[...])
    a = -jnp.exp(alog_ref[...])
    la = dt * a
    row = lax.broadcasted_iota(jnp.int32, (L, L), 0)
    col = lax.broadcasted_iota(jnp.int32, (L, L), 1)
    causal = col <= row
    cum = _dot_f32_lhs01(causal.astype(BF16), la)
    cum_t = cum.T
    clast = cum[L - 1:L, :]

    e_head = (lax.broadcasted_iota(jnp.int32, (LANES, SSM_INNER), 1) // SSM_HEADDIM
              == lax.broadcasted_iota(jnp.int32, (LANES, SSM_INNER), 0)).astype(BF16)
    x = xs * _dot_f32_rhs01(dt, e_head)
    ecum_x = _dot_f32_rhs01(jnp.exp(cum), e_head)
    dte_x = _dot_f32_rhs01(jnp.exp(clast - cum), e_head)
    cdec_x = _dot_f32_rhs01(jnp.broadcast_to(jnp.exp(clast), (SUBLANES, LANES)), e_head)[0:1, :]
    xb = x.astype(BF16)
    xdte = (x * dte_x).astype(BF16)
    lo_half = lax.broadcasted_iota(jnp.int32, (L, LANES), 1) < SSM_HEADDIM

    for g in range(SSM_GROUPS):
        bg = bc[:, g * SSM_STATE:(g + 1) * SSM_STATE]
        cg = bc[:, (SSM_GROUPS + g) * SSM_STATE:(SSM_GROUPS + g + 1) * SSM_STATE].astype(BF16)
        cb = _dot_nt(cg, bg.astype(BF16))
        hg = state_ref[g]
        y_off = _dot(cg, hg.astype(BF16)) * ecum_x[:, g * GW:(g + 1) * GW]
        pieces = []
        for pr in range(GW // LANES):
            h0 = g * (GW // SSM_HEADDIM) + 2 * pr
            xp = xb[:, h0 * SSM_HEADDIM:h0 * SSM_HEADDIM + LANES]
            ypair = None
            for s in range(2):
                h = h0 + s
                diff = cum[:, h:h + 1] - cum_t[h:h + 1, :]
                decay = jnp.exp(jnp.where(causal, diff, -jnp.inf))
                m = (cb * decay).astype(BF16)
                keep = lo_half if s == 0 else jnp.logical_not(lo_half)
                yh = _dot(m, jnp.where(keep, xp, jnp.zeros_like(xp)))
                ypair = yh if ypair is None else ypair + yh
            pieces.append(ypair)
        y_diag = jnp.concatenate(pieces, axis=-1)
        st = _dot(bg.T.astype(BF16), xdte[:, g * GW:(g + 1) * GW])
        state_ref[g] = hg * cdec_x[:, g * GW:(g + 1) * GW] + st

        yg = y_diag + y_off + dsk_ref[:, g * GW:(g + 1) * GW] * xs[:, g * GW:(g + 1) * GW]
        yg = yg * _silu(z_ref[:, g * GW:(g + 1) * GW])
        ms = jnp.mean(yg * yg, axis=-1, keepdims=True)
        yg = yg * lax.rsqrt(ms + NORM_EPS) * gn_ref[:, g * GW:(g + 1) * GW]
        y_ref[:, g * GW:(g + 1) * GW] = yg.astype(y_ref.dtype)


def ssd_mixer(proj, dt_raw, conv_w, conv_b, dt_bias, a_log, d_skip, ssm_norm, bsz, seq):
    L = SSM_CHUNK
    nc = seq // L
    hb = L // SUBLANES
    pad = lambda v: jnp.pad(v.reshape(1, -1), ((0, 0), (0, LANES - v.shape[-1])))
    rowblk = lambda b, c: b * nc + c
    halo_row = lambda b, c: jnp.maximum((b * nc + c) * hb - 1, 0)
    const = lambda b, c: (0, 0)
    bcw = 2 * SSM_GROUPS * SSM_STATE
    return pl.pallas_call(
        _ssd_kernel,
        grid=(bsz, nc),
        in_specs=[pl.BlockSpec((L, SSM_INNER), lambda b, c: (rowblk(b, c), 0)),
                  pl.BlockSpec((L, SSM_INNER), lambda b, c: (rowblk(b, c), 1)),
                  pl.BlockSpec((SUBLANES, SSM_INNER), lambda b, c: (halo_row(b, c), 1)),
                  pl.BlockSpec((L, bcw), lambda b, c: (rowblk(b, c), 4)),
                  pl.BlockSpec((SUBLANES, bcw), lambda b, c: (halo_row(b, c), 4)),
                  pl.BlockSpec((L, LANES), lambda b, c: (rowblk(b, c), 0)),
                  pl.BlockSpec((SSM_CONV_WIDTH, SSM_INNER), const),
                  pl.BlockSpec((1, SSM_INNER), const),
                  pl.BlockSpec((SSM_CONV_WIDTH, bcw), const),
                  pl.BlockSpec((1, bcw), const),
                  pl.BlockSpec((1, LANES), const),
                  pl.BlockSpec((1, LANES), const),
                  pl.BlockSpec((1, SSM_INNER), const),
                  pl.BlockSpec((1, SSM_INNER), const)],
        out_specs=pl.BlockSpec((L, SSM_INNER), lambda b, c: (rowblk(b, c), 0)),
        out_shape=jax.ShapeDtypeStruct((bsz * seq, SSM_INNER), BF16),
        scratch_shapes=[pltpu.VMEM((L + SUBLANES, SSM_INNER), F32),
                        pltpu.VMEM((L + SUBLANES, bcw), F32),
                        pltpu.VMEM((SSM_GROUPS, SSM_STATE, SSM_INNER // SSM_GROUPS), F32)],
        compiler_params=_cparams("parallel", "arbitrary"),
    )(proj, proj, proj, proj, proj, dt_raw,
      conv_w[:, :SSM_INNER], conv_b[:SSM_INNER].reshape(1, -1),
      conv_w[:, SSM_INNER:], conv_b[SSM_INNER:].reshape(1, -1),
      pad(dt_bias), pad(a_log), jnp.repeat(d_skip, SSM_HEADDIM).reshape(1, -1), ssm_norm.reshape(1, -1))


def _conformer_kernel(ua_ref, ug_ref, uah_ref, ugh_ref, w_ref, b_ref, lg_ref, lb_ref, o_ref, ext_ref,
                      shift_ref, conv_ref, *, tiles_per_seq):
    first = (pl.program_id(0) % tiles_per_seq) == 0
    tm = ua_ref.shape[0]
    ext_ref[0:CONF_HALO, :] = jnp.where(first, 0.0, uah_ref[...] * jax.nn.sigmoid(ugh_ref[...]))
    ext_ref[CONF_HALO:, :] = ua_ref[...] * jax.nn.sigmoid(ug_ref[...])
    span = tm + CONF_HALO - SUBLANES
    for s in range(1, SUBLANES):
        shift_ref[s - 1] = ext_ref[s:s + span, :]
    for c0 in range(0, CONF_WIDTH, LANES):
        cs = slice(c0, c0 + LANES)
        acc = jnp.broadcast_to(b_ref[:, cs], (tm, LANES))
        for k in range(CONF_CONV_WIDTH):
            q, s = divmod(CONF_HALO - (CONF_CONV_WIDTH - 1) + k, SUBLANES)
            rows = slice(q * SUBLANES, q * SUBLANES + tm)
            tap = ext_ref[rows, cs] if s == 0 else shift_ref[s - 1, rows, cs]
            acc = acc + w_ref[k:k + 1, cs] * tap
        conv_ref[:, cs] = acc
    c = conv_ref[...]
    mu = jnp.mean(c, axis=-1, keepdims=True)
    d = c - mu
    var = jnp.mean(d * d, axis=-1, keepdims=True)
    y = d * lax.rsqrt(var + NORM_EPS) * lg_ref[...] + lb_ref[...]
    o_ref[...] = _silu(y).astype(o_ref.dtype)


def conformer_mixer(proj, dw_w, dw_b, ln_g, ln_b, seq, *, tm=TM_CONF):
    t = proj.shape[0]
    hb = tm // CONF_HALO
    halo_row = lambda i: jnp.maximum(i * hb - 1, 0)
    const = lambda i: (0, 0)
    return pl.pallas_call(
        functools.partial(_conformer_kernel, tiles_per_seq=seq // tm),
        grid=(t // tm,),
        in_specs=[pl.BlockSpec((tm, CONF_WIDTH), lambda i: (i, 5)),
                  pl.BlockSpec((tm, CONF_WIDTH), lambda i: (i, 6)),
                  pl.BlockSpec((CONF_HALO, CONF_WIDTH), lambda i: (halo_row(i), 5)),
                  pl.BlockSpec((CONF_HALO, CONF_WIDTH), lambda i: (halo_row(i), 6)),
                  pl.BlockSpec((CONF_CONV_WIDTH, CONF_WIDTH), const),
                  pl.BlockSpec((1, CONF_WIDTH), const),
                  pl.BlockSpec((1, CONF_WIDTH), const),
                  pl.BlockSpec((1, CONF_WIDTH), const)],
        out_specs=pl.BlockSpec((tm, CONF_WIDTH), lambda i: (i, 0)),
        out_shape=jax.ShapeDtypeStruct((t, CONF_WIDTH), BF16),
        scratch_shapes=[pltpu.VMEM((tm + CONF_HALO, CONF_WIDTH), F32),
                        pltpu.VMEM((SUBLANES - 1, tm + CONF_HALO - SUBLANES, CONF_WIDTH), F32),
                        pltpu.VMEM((tm, CONF_WIDTH), F32)],
        compiler_params=_cparams("parallel"),
    )(proj, proj, proj, proj, dw_w, dw_b.reshape(1, -1), ln_g.reshape(1, -1), ln_b.reshape(1, -1))


def _shortconv_kernel(b_ref, c_ref, h_ref, ch_ref, hh_ref, w_ref, o_ref, ext_ref, *, tiles_per_seq):
    first = (pl.program_id(0) % tiles_per_seq) == 0
    conv = _dwconv(ext_ref, c_ref[...] * h_ref[...], ch_ref[...] * hh_ref[...], first, w_ref, SC_CONV_WIDTH)
    o_ref[...] = (b_ref[...] * conv).astype(o_ref.dtype)


def shortconv_mixer(proj, conv_w, seq, *, tm=TM_SC):
    t = proj.shape[0]
    hb = tm // SUBLANES
    halo_row = lambda i: jnp.maximum(i * hb - 1, 0)
    return pl.pallas_call(
        functools.partial(_shortconv_kernel, tiles_per_seq=seq // tm),
        grid=(t // tm,),
        in_specs=[pl.BlockSpec((tm, SC_WIDTH), lambda i: (i, 0)),
                  pl.BlockSpec((tm, SC_WIDTH), lambda i: (i, 1)),
                  pl.BlockSpec((tm, SC_WIDTH), lambda i: (i, 2)),
                  pl.BlockSpec((SUBLANES, SC_WIDTH), lambda i: (halo_row(i), 1)),
                  pl.BlockSpec((SUBLANES, SC_WIDTH), lambda i: (halo_row(i), 2)),
                  pl.BlockSpec((SC_CONV_WIDTH, SC_WIDTH), lambda i: (0, 0))],
        out_specs=pl.BlockSpec((tm, SC_WIDTH), lambda i: (i, 0)),
        out_shape=jax.ShapeDtypeStruct((t, SC_WIDTH), BF16),
        scratch_shapes=[pltpu.VMEM((tm + SUBLANES, SC_WIDTH), F32)],
        compiler_params=_cparams("parallel"),
    )(proj, proj, proj, proj, proj, conv_w)


def _rope_table_kernel(pos_ref, inv_ref, cos_ref, sin_ref):
    ang = pos_ref[...].astype(F32) * inv_ref[...]
    sign = jnp.where(lax.broadcasted_iota(jnp.int32, ang.shape, 1) < NSA_HEAD_DIM // 2, -1.0, 1.0)
    cos_ref[...] = jnp.cos(ang)
    sin_ref[...] = jnp.sin(ang) * sign


def rope_tables(positions, *, tm=TM_ROPE):
    t = positions.size
    half = NSA_HEAD_DIM // 2
    inv = ROPE_THETA ** (-jnp.arange(half, dtype=F32) / half)
    inv = jnp.concatenate([inv, inv]).reshape(1, NSA_HEAD_DIM)
    tab = jax.ShapeDtypeStruct((t, NSA_HEAD_DIM), F32)
    return pl.pallas_call(
        _rope_table_kernel,
        grid=(t // tm,),
        in_specs=[pl.BlockSpec((tm, 1), lambda i: (i, 0)), pl.BlockSpec((1, NSA_HEAD_DIM), lambda i: (0, 0))],
        out_specs=[pl.BlockSpec((tm, NSA_HEAD_DIM), lambda i: (i, 0))] * 2,
        out_shape=[tab, tab],
        compiler_params=_cparams("parallel"),
    )(positions.reshape(t, 1), inv)


def _rope(x, cos, sin_signed):
    return x * cos + pltpu.roll(x, NSA_HEAD_DIM // 2, axis=1) * sin_signed


def _kv_prep_kernel(ks_ref, vs_ref, kw_ref, vw_ref, cos_ref, sin_ref, kso_ref, vso_ref, kwo_ref, vwo_ref,
                    *, tiles_per_seq):
    D = NSA_HEAD_DIM
    tm = ks_ref.shape[0]
    cos, sin = cos_ref[...], sin_ref[...]
    row = (pl.program_id(0) % tiles_per_seq) * tm + lax.broadcasted_iota(jnp.int32, (tm, D), 0)
    lane = lax.broadcasted_iota(jnp.int32, (tm, D), 1)
    onehot = jnp.where(row // SLC_BLOCK == lane, 1.0, 0.0).astype(BF16)
    ones_col = jnp.where(lane == 0, 1.0, 0.0).astype(BF16)
    for g in range(NSA_KV_HEADS):
        sl = slice(g * D, (g + 1) * D)
        kso_ref[:, 2 * g * D:(2 * g + 1) * D] = _rope(ks_ref[:, sl], cos, sin).astype(BF16)
        kso_ref[:, (2 * g + 1) * D:(2 * g + 2) * D] = onehot
        kwo_ref[:, sl] = _rope(kw_ref[:, sl], cos, sin).astype(BF16)
        vso_ref[:, 2 * g * D:(2 * g + 1) * D] = vs_ref[:, sl].astype(BF16)
        vso_ref[:, (2 * g + 1) * D:(2 * g + 2) * D] = ones_col
        vwo_ref[:, 2 * g * D:(2 * g + 1) * D] = vw_ref[:, sl].astype(BF16)
        vwo_ref[:, (2 * g + 1) * D:(2 * g + 2) * D] = ones_col


def kv_prep(proj, cos, sin, seq, *, tm=512):
    t = proj.shape[0]
    kv = jax.ShapeDtypeStruct((t, NSA_KV), BF16)
    kv_aug = jax.ShapeDtypeStruct((t, 2 * NSA_KV), BF16)
    spec = pl.BlockSpec((tm, NSA_KV), lambda i: (i, 0))
    spec_aug = pl.BlockSpec((tm, 2 * NSA_KV), lambda i: (i, 0))
    return pl.pallas_call(
        functools.partial(_kv_prep_kernel, tiles_per_seq=seq // tm),
        grid=(t // tm,),
        in_specs=[pl.BlockSpec((tm, NSA_KV), lambda i: (i, 12)),
                  pl.BlockSpec((tm, NSA_KV), lambda i: (i, 13)),
                  pl.BlockSpec((tm, NSA_KV), lambda i: (i, 14)),
                  pl.BlockSpec((tm, NSA_KV), lambda i: (i, 15)),
                  pl.BlockSpec((tm, NSA_HEAD_DIM), lambda i: (i, 0)),
                  pl.BlockSpec((tm, NSA_HEAD_DIM), lambda i: (i, 0))],
        out_specs=[spec_aug, spec_aug, spec, spec_aug],
        out_shape=[kv_aug, kv_aug, kv, kv_aug],
        compiler_params=_cparams("parallel"),
    )(proj, proj, proj, proj, cos, sin)


def _compress_kernel(kt_ref, vt_ref, cos_ref, sin_ref, pe_ref, w1_ref, w2_ref, kc_ref, vc_ref):
    nb = kc_ref.shape[0]
    half = CMP_BLOCK // 2

    def compress(tok_ref, which):
        acc_lo = jnp.zeros((nb, NSA_HEAD_DIM), F32)
        acc_hi = jnp.zeros((nb, NSA_HEAD_DIM), F32)
        for l in range(half):
            r = tok_ref[pl.ds(l, nb, stride=CMP_STRIDE), :]
            lo = (r + pe_ref[which, l:l + 1, :]).astype(BF16)
            hi = (r + pe_ref[which, half + l:half + l + 1, :]).astype(BF16)
            acc_lo = acc_lo + _dot(lo, w1_ref[which, l * NSA_HEAD_DIM:(l + 1) * NSA_HEAD_DIM, :])
            acc_hi = acc_hi + _dot(hi, w1_ref[which, (half + l) * NSA_HEAD_DIM:(half + l + 1) * NSA_HEAD_DIM, :])
        pre = acc_lo + pltpu.roll(acc_hi, nb - 1, axis=0)
        return _dot(_silu(pre).astype(BF16), w2_ref[which])

    cos_e = pltpu.roll(cos_ref[pl.ds(CMP_STRIDE - 1, nb, stride=CMP_STRIDE), :], nb - 1, axis=0)
    sin_e = pltpu.roll(sin_ref[pl.ds(CMP_STRIDE - 1, nb, stride=CMP_STRIDE), :], nb - 1, axis=0)
    kc_ref[...] = _rope(compress(kt_ref, 0), cos_e, sin_e).astype(BF16)
    vc_ref[...] = compress(vt_ref, 1).astype(BF16)


def compress_kv(proj, cos, sin, pe, w1, w2, bsz, seq):
    nb = seq // CMP_STRIDE
    out = jax.ShapeDtypeStruct((bsz, NSA_KV_HEADS, nb, NSA_HEAD_DIM), BF16)
    full = lambda *shape: pl.BlockSpec(shape, lambda b, g: (0,) * len(shape))
    return pl.pallas_call(
        _compress_kernel,
        grid=(bsz, NSA_KV_HEADS),
        in_specs=[pl.BlockSpec((seq, NSA_HEAD_DIM), lambda b, g: (b, 40 + g)),
                  pl.BlockSpec((seq, NSA_HEAD_DIM), lambda b, g: (b, 44 + g)),
                  pl.BlockSpec((seq, NSA_HEAD_DIM), lambda b, g: (b, 0)),
                  pl.BlockSpec((seq, NSA_HEAD_DIM), lambda b, g: (b, 0)),
                  full(2, CMP_BLOCK, NSA_HEAD_DIM),
                  full(2, CMP_BLOCK * NSA_HEAD_DIM, NSA_HEAD_DIM),
                  full(2, NSA_HEAD_DIM, NSA_HEAD_DIM)],
        out_specs=[pl.BlockSpec((None, None, nb, NSA_HEAD_DIM), lambda b, g: (b, g, 0, 0))] * 2,
        out_shape=[out, out],
        compiler_params=_cparams("parallel", "parallel"),
    )(proj, proj, cos, sin, pe, w1, w2)


def _nsa_kernel(q_ref, cos_ref, sin_ref, gate_ref, kc_ref, vc_ref, ks_ref, vs_ref, kw_ref, vw_ref,
                o_ref, qa_ref, impt_ref, sc_ref, mrun_ref, acc_ref):
    G = NSA_GROUP
    D = NSA_HEAD_DIM
    R = G * TQ
    base = pl.program_id(2) * (NSA_TILES * TQ)
    scale = D ** -0.5
    n_cmp = kc_ref.shape[0]
    n_slc = ks_ref.shape[0] // SLC_BLOCK
    wk = WINDOW + TQ
    j = lax.broadcasted_iota(jnp.int32, (1, LANES), 1)
    ends = lax.broadcasted_iota(jnp.int32, (1, n_cmp), 1) * CMP_STRIDE + (CMP_BLOCK - 1)
    n_i = lax.broadcasted_iota(jnp.int32, (n_cmp, LANES), 0)
    j_i = lax.broadcasted_iota(jnp.int32, (n_cmp, LANES), 1)
    overlap = ((n_i * CMP_STRIDE < (j_i + 1) * SLC_BLOCK)
               & (n_i * CMP_STRIDE + CMP_BLOCK - 1 >= j_i * SLC_BLOCK)).astype(BF16)
    rel = lax.broadcasted_iota(jnp.int32, (TQ, wk), 1) - lax.broadcasted_iota(jnp.int32, (TQ, wk), 0)
    lower = (lax.broadcasted_iota(jnp.int32, (TQ, TQ), 1) <= lax.broadcasted_iota(jnp.int32, (TQ, TQ), 0))
    own_bias = jnp.where(lower, 0.0, NEG_BIG)
    sub = lax.broadcasted_iota(jnp.int32, (SUBLANES, TQ), 0)

    o_cmp, o_win, s_own = [], [], []
    for u in range(NSA_TILES):
        rows = slice(u * TQ, (u + 1) * TQ)
        q0 = base + u * TQ
        cos, sin = cos_ref[rows, :], sin_ref[rows, :]
        for e in range(G):
            xq = _rope(q_ref[rows, e * D:(e + 1) * D], cos, sin)
            qa_ref[u, e * TQ:(e + 1) * TQ, 0:D] = (xq * scale).astype(BF16)
        qs = qa_ref[u, :, 0:D]
        t_col = q0 + lax.broadcasted_iota(jnp.int32, (TQ, 1), 0)

        cvalid = ends <= t_col
        cbias = jnp.where(cvalid, 0.0, NEG_BIG)
        c01 = jnp.where(cvalid, 1.0, 0.0)
        s = _dot_nt(qs, kc_ref[...]).reshape(G, TQ, n_cmp) + cbias[None]
        m = jnp.max(s, axis=-1, keepdims=True)
        p = jnp.exp(s - m) * c01[None]
        l = jnp.sum(p, axis=-1, keepdims=True)
        p = p * (1.0 / jnp.maximum(l, 1e-30))
        o_cmp.append(_dot(p.reshape(R, n_cmp).astype(BF16), vc_ref[...]))

        start = pl.multiple_of(jnp.maximum(q0 - WINDOW, 0), TQ)
        d = q0 - start
        wbias = jnp.where((rel <= d) & (rel > d - WINDOW), 0.0, NEG_BIG)
        sw = _dot_nt(qs, kw_ref[pl.ds(start, wk), :]).reshape(G, TQ, wk) + wbias[None]
        mw = jnp.max(sw, axis=-1, keepdims=True)
        pw = jnp.exp(sw - mw).reshape(R, wk)
        accw = _dot(pw.astype(BF16), vw_ref[pl.ds(start, wk), :])
        o_win.append(accw[:, 0:D] * (1.0 / accw[:, D:D + 1]))

        psum = p[0]
        for e in range(1, G):
            psum = psum + p[e]
        imp = _dot_f32_rhs01(psum, overlap)
        cur = t_col // SLC_BLOCK
        forced = (j == 0) | ((j <= cur) & (j > cur - N_LOCAL))
        valid = j * SLC_BLOCK <= t_col
        imp = jnp.where(forced, FORCE, jnp.where(valid, imp, -FORCE))
        impt_ref[u] = imp.T
        nrb = n_slc // SUBLANES
        vals = [impt_ref[u, rb * SUBLANES:(rb + 1) * SUBLANES, :] for rb in range(nrb)]
        cnt = [jnp.zeros((SUBLANES, TQ), F32) for _ in range(nrb)]
        for k in range(n_slc):
            vk = jnp.broadcast_to(impt_ref[u, k:k + 1, :], (SUBLANES, TQ))
            for rb in range(nrb):
                if rb * SUBLANES > k:
                    beats = vk >= vals[rb]
                elif rb * SUBLANES + SUBLANES - 1 < k:
                    beats = vk > vals[rb]
                else:
                    beats = (vk > vals[rb]) | ((vk == vals[rb]) & (sub > k - rb * SUBLANES))
                cnt[rb] = cnt[rb] + jnp.where(beats, 1.0, 0.0)
        sel_t = jnp.concatenate([jnp.where(c < N_SELECT, 1.0, 0.0) for c in cnt]
                                + [jnp.zeros((LANES - n_slc, TQ), F32)], axis=0)
        sel = sel_t.T
        sbias = jnp.where((sel > 0.5) & (j * SLC_BLOCK < q0), 0.0, NEG_BIG).astype(BF16)
        for e in range(G):
            qa_ref[u, e * TQ:(e + 1) * TQ, D:2 * D] = sbias

        s_own.append((_dot_nt(qs, ks_ref[pl.ds(q0, TQ), 0:D]).reshape(G, TQ, TQ)
                      + own_bias[None]).reshape(R, TQ))
        mrun_ref[u] = s_own[u]

    n_kt = (base + (NSA_TILES - 1) * TQ + TK - 1) // TK

    def score_body(kt, carry):
        k0 = pl.multiple_of(kt * TK, TK)
        for u in range(NSA_TILES):
            s = _dot_nt(qa_ref[u], ks_ref[pl.ds(k0, TK), :])
            sc_ref[u, kt] = s
            t = s[:, 0:LANES]
            for c in range(1, TK // LANES):
                t = jnp.maximum(t, s[:, c * LANES:(c + 1) * LANES])
            mrun_ref[u] = jnp.maximum(mrun_ref[u], t)
        return carry

    lax.fori_loop(0, n_kt, score_body, 0)
    m_row = []
    for u in range(NSA_TILES):
        m_row.append(jnp.max(mrun_ref[u], axis=-1, keepdims=True))
        acc_ref[u] = _dot(jnp.exp(s_own[u] - m_row[u]).astype(BF16),
                          vs_ref[pl.ds(base + u * TQ, TQ), :])

    def pv_body(kt, carry):
        k0 = pl.multiple_of(kt * TK, TK)
        for u in range(NSA_TILES):
            p = jnp.exp(sc_ref[u, kt] - m_row[u]).astype(BF16)
            acc_ref[u] += _dot(p, vs_ref[pl.ds(k0, TK), :])
        return carry

    lax.fori_loop(0, n_kt, pv_body, 0)

    for u in range(NSA_TILES):
        rows = slice(u * TQ, (u + 1) * TQ)
        acc = acc_ref[u]
        o_slc = acc[:, 0:D] * (1.0 / acc[:, D:D + 1])
        gate = jax.nn.sigmoid(gate_ref[rows, :])
        for e in range(G):
            hr = slice(e * TQ, (e + 1) * TQ)
            o = (gate[:, 3 * e:3 * e + 1] * o_cmp[u][hr] + gate[:, 3 * e + 1:3 * e + 2] * o_slc[hr]
                 + gate[:, 3 * e + 2:3 * e + 3] * o_win[u][hr])
            o_ref[rows, e * D:(e + 1) * D] = o.astype(o_ref.dtype)


def nsa_attention(proj, gates, cos, sin, kc, vc, ks, vs, kw, vw, bsz, seq):
    tq = NSA_TILES * TQ
    nq = seq // tq
    qw = NSA_GROUP * NSA_HEAD_DIM
    rows = NSA_GROUP * TQ
    row = lambda b, g, i: b * nq + i
    kvspec = pl.BlockSpec((seq, NSA_HEAD_DIM), lambda b, g, i: (b, g))
    augspec = pl.BlockSpec((seq, 2 * NSA_HEAD_DIM), lambda b, g, i: (b, g))
    cspec = pl.BlockSpec((None, None, seq // CMP_STRIDE, NSA_HEAD_DIM), lambda b, g, i: (b, g, 0, 0))
    return pl.pallas_call(
        _nsa_kernel,
        grid=(bsz, NSA_KV_HEADS, nq),
        in_specs=[pl.BlockSpec((tq, qw), lambda b, g, i: (row(b, g, i), 6 + g)),
                  pl.BlockSpec((tq, NSA_HEAD_DIM), lambda b, g, i: (row(b, g, i), 0)),
                  pl.BlockSpec((tq, NSA_HEAD_DIM), lambda b, g, i: (row(b, g, i), 0)),
                  pl.BlockSpec((tq, LANES), lambda b, g, i: (row(b, g, i), g)),
                  cspec, cspec, augspec, augspec, kvspec, augspec],
        out_specs=pl.BlockSpec((tq, qw), lambda b, g, i: (row(b, g, i), g)),
        out_shape=jax.ShapeDtypeStruct((bsz * seq, NSA_Q), BF16),
        scratch_shapes=[pltpu.VMEM((NSA_TILES, rows, 2 * NSA_HEAD_DIM), BF16),
                        pltpu.VMEM((NSA_TILES, LANES, TQ), F32),
                        pltpu.VMEM((NSA_TILES, seq // TK, rows, TK), F32),
                        pltpu.VMEM((NSA_TILES, rows, LANES), F32),
                        pltpu.VMEM((NSA_TILES, rows, 2 * NSA_HEAD_DIM), F32)],
        compiler_params=_cparams("parallel", "parallel", "parallel"),
    )(proj, cos, sin, gates, kc, vc, ks, vs, kw, vw)


def _even_layer(h, gain, w_in, conv_w, conv_b, dt_bias, a_log, d_skip, ssm_norm,
                dw_w, dw_b, ln_g, ln_b, w_out, i, bsz, seq):
    xbc_end = SSM_INNER + SSM_INNER + 2 * SSM_GROUPS * SSM_STATE
    dt_end = xbc_end + SSM_HEADS
    w_main = jnp.concatenate([w_in[:, :xbc_end], w_in[:, dt_end:]], axis=1).astype(BF16)
    w_dt = jnp.pad(w_in[:, xbc_end:dt_end], ((0, 0), (0, LANES - SSM_HEADS))).astype(BF16)
    proj, dt_raw = norm_matmul(h, gain, w_main, w_dt)
    y = ssd_mixer(proj, dt_raw, conv_w, conv_b, dt_bias, a_log, d_skip, ssm_norm, bsz, seq)
    u = conformer_mixer(proj, dw_w, dw_b, ln_g, ln_b, seq)
    return matmul_res([(y, 0, 0, SSM_INNER), (u, 0, SSM_INNER // CONF_WIDTH, CONF_WIDTH)], w_out, i, h)


def _odd_layer(h, gain, cos, sin, w_in, w_gates, sc_w, pe, w1, w2, w_out, i, bsz, seq):
    main = 3 * SC_WIDTH + NSA_Q + 6 * NSA_KV
    per_g = 3 * NSA_GROUP
    wg = w_gates.reshape(D_MODEL, NSA_KV_HEADS, per_g)
    wg = jnp.pad(wg, ((0, 0), (0, 0), (0, LANES - per_g))).reshape(D_MODEL, NSA_KV_HEADS * LANES)
    proj, gates = norm_matmul(h, gain, w_in, wg.astype(BF16), layer=i, n=main)
    y_c = shortconv_mixer(proj, sc_w, seq)
    ks, vs, kw, vw = kv_prep(proj, cos, sin, seq)
    kc, vc = compress_kv(proj, cos, sin, pe, w1, w2, bsz, seq)
    y_d = nsa_attention(proj, gates, cos, sin, kc, vc, ks, vs, kw, vw, bsz, seq)
    return matmul_res([(y_c, 0, 0, SC_WIDTH), (y_d, 0, 1, SC_WIDTH), (y_d, 1, 2, SC_WIDTH)], w_out, i, h)


def _ffn(h, gain, w_up, conv_w, conv_b, w_down, layer, seq):
    a = ffn_up_gate(h, gain, w_up, layer, conv_w, conv_b, seq)
    return matmul_res([(a, 0, 0, D_FF)], w_down, layer, h, tn=TN_DOWN)


def kernel(x, positions, mix_norm, ffn_norm, final_norm, ab_w_in, ssm_conv_w, ssm_conv_b, ssm_dt_bias, ssm_a_log, ssm_d, ssm_norm, conf_dw_w, conf_dw_b, conf_ln_g, conf_ln_b, ab_w_out, cd_w_in, sc_conv_w, nsa_cmp_pe, nsa_cmp_w1, nsa_cmp_w2, cd_w_out, ffn_w_up, ffn_conv_w, ffn_conv_b, ffn_w_down):
    bsz, seq, d = x.shape
    depth = mix_norm.shape[0]
    h = x.reshape(bsz * seq, d)
    cos, sin = rope_tables(positions)
    gate_cols = 3 * SC_WIDTH + NSA_Q + 6 * NSA_KV
    ab_w_out_b, cd_w_out_b = ab_w_out.astype(BF16), cd_w_out.astype(BF16)
    cd_w_in_b = cd_w_in[:, :, :gate_cols].astype(BF16)
    ffn_w_up_b, ffn_w_down_b = ffn_w_up.astype(BF16), ffn_w_down.astype(BF16)
    cmp_w1_b, cmp_w2_b = nsa_cmp_w1.astype(BF16), nsa_cmp_w2.astype(BF16)
    for layer in range(depth):
        i = layer // 2
        if layer % 2 == 0:
            h = _even_layer(h, mix_norm[layer], ab_w_in[i], ssm_conv_w[i], ssm_conv_b[i], ssm_dt_bias[i],
                            ssm_a_log[i], ssm_d[i], ssm_norm[i], conf_dw_w[i], conf_dw_b[i], conf_ln_g[i],
                            conf_ln_b[i], ab_w_out_b, i, bsz, seq)
        else:
            h = _odd_layer(h, mix_norm[layer], cos, sin, cd_w_in_b, cd_w_in[i][:, gate_cols:], sc_conv_w[i],
                           nsa_cmp_pe[i], cmp_w1_b[i], cmp_w2_b[i], cd_w_out_b, i, bsz, seq)
        h = _ffn(h, ffn_norm[layer], ffn_w_up_b, ffn_conv_w[layer], ffn_conv_b[layer], ffn_w_down_b, layer, seq)
    return rmsnorm_rows(h, final_norm).reshape(bsz, seq, d)
```

```python
import functools

import jax
import jax.numpy as jnp
from jax import lax
from jax.experimental import pallas as pl
from jax.experimental.pallas import tpu as pltpu

F32 = jnp.float32
BF16 = jnp.bfloat16

D_MODEL = 2048
NORM_EPS = 1e-6
SSM_INNER = 2048
SSM_HEADDIM = 64
SSM_HEADS = 32
SSM_GROUPS = 4
SSM_STATE = 128
SSM_CONV_WIDTH = 4
SSM_CHUNK = 128
CONF_WIDTH = 1024
CONF_CONV_WIDTH = 31
SC_WIDTH = 1024
SC_CONV_WIDTH = 3
NSA_HEAD_DIM = 128
NSA_HEADS = 16
NSA_KV_HEADS = 4
NSA_GROUP = 4
CMP_BLOCK = 32
CMP_STRIDE = 16
SLC_BLOCK = 64
N_SELECT = 16
N_LOCAL = 2
WINDOW = 512
ROPE_THETA = 10000.0
NSA_Q = 2048
NSA_KV = 512
D_FF = 5632
FFN_CONV_WIDTH = 3
NEG_BIG = -1e30
FORCE = 1e9

V7X_VMEM_BYTES = 64 * 1024 * 1024
VMEM_LIMIT = V7X_VMEM_BYTES - 8 * 1024 * 1024
LANES = 128
SUBLANES = 8

TM_PROJ = 1024
TN_PROJ = 1024
TM_OUT = 512
TN_OUT = 2048
TN_DOWN = 1024
TN_FFN = 512
FFN_HALO = 16
FFN_CHUNKS = 2
TM_CONF = 256
CONF_HALO = 32
TM_SC = 512
TM_ROPE = 1024
TQ = 128
TK = 1024
NSA_TILES = 2


def _cparams(*sem):
    return pltpu.CompilerParams(dimension_semantics=sem, vmem_limit_bytes=VMEM_LIMIT)


def _silu(x):
    return x * jax.nn.sigmoid(x)


def _softplus(x):
    return jnp.maximum(x, 0.0) + jnp.log1p(jnp.exp(-jnp.abs(x)))


def _split3(x):
    hi = x.astype(BF16)
    r1 = x - hi.astype(F32)
    mid = r1.astype(BF16)
    lo = (r1 - mid.astype(F32)).astype(BF16)
    return hi, mid, lo


def _dot(a, b):
    return jnp.dot(a, b, preferred_element_type=F32)


def _dot_nt(a, b):
    return lax.dot_general(a, b, (((1,), (1,)), ((), ())), preferred_element_type=F32)


def _dot_f32_rhs01(x, e01):
    return _dot(jnp.concatenate(_split3(x), axis=1), jnp.concatenate([e01] * 3, axis=0))


def _dot_f32_lhs01(e01, x):
    return _dot(jnp.concatenate([e01] * 3, axis=1), jnp.concatenate(_split3(x), axis=0))


def _dwconv(ext_ref, x, halo, first, w_ref, width):
    rows = x.shape[0]
    hrows = halo.shape[0]
    ext_ref[0:hrows, :] = jnp.where(first, 0.0, halo)
    ext_ref[hrows:hrows + rows, :] = x
    acc = None
    for k in range(width):
        off = hrows - (width - 1) + k
        term = w_ref[k:k + 1, :] * ext_ref[off:off + rows, :]
        acc = term if acc is None else acc + term
    return acc


def _norm_matmul_kernel(h_ref, g_ref, w_ref, ws_ref, o_ref, os_ref, xn_ref):
    @pl.when(pl.program_id(1) == 0)
    def _():
        x = h_ref[...]
        ms = jnp.mean(x * x, axis=-1, keepdims=True)
        xn_ref[...] = (x * lax.rsqrt(ms + NORM_EPS) * g_ref[...]).astype(BF16)
        os_ref[...] = _dot(xn_ref[...], ws_ref[...])

    o_ref[...] = _dot(xn_ref[...], w_ref[...]).astype(o_ref.dtype)


def norm_matmul(h, gain, w, w_side, *, layer=None, n=None, tm=TM_PROJ, tn=TN_PROJ):
    t, k = h.shape
    ns = w_side.shape[1]
    if layer is None:
        n = w.shape[1]
        w_spec = pl.BlockSpec((k, tn), lambda i, j: (0, j))
    else:
        w_spec = pl.BlockSpec((None, k, tn), lambda i, j: (layer, 0, j))
    return pl.pallas_call(
        _norm_matmul_kernel,
        grid=(t // tm, n // tn),
        in_specs=[pl.BlockSpec((tm, k), lambda i, j: (i, 0)),
                  pl.BlockSpec((1, k), lambda i, j: (0, 0)),
                  w_spec,
                  pl.BlockSpec((k, ns), lambda i, j: (0, 0))],
        out_specs=[pl.BlockSpec((tm, tn), lambda i, j: (i, j)),
                   pl.BlockSpec((tm, ns), lambda i, j: (i, 0))],
        out_shape=[jax.ShapeDtypeStruct((t, n), F32), jax.ShapeDtypeStruct((t, ns), F32)],
        scratch_shapes=[pltpu.VMEM((tm, k), BF16)],
        compiler_params=_cparams("parallel", "arbitrary"),
    )(h, gain.reshape(1, k), w, w_side)


def _matmul_res_kernel(*refs, n_in):
    a_refs, w_refs = refs[:n_in], refs[n_in:2 * n_in]
    r_ref, o_ref = refs[2 * n_in], refs[2 * n_in + 1]
    acc = _dot(a_refs[0][...], w_refs[0][...])
    for a_ref, w_ref in zip(a_refs[1:], w_refs[1:]):
        acc = acc + _dot(a_ref[...], w_ref[...])
    o_ref[...] = r_ref[...] + acc


def matmul_res(terms, w, layer, res, *, tm=TM_OUT, tn=TN_OUT):
    t, n = res.shape
    in_specs = ([pl.BlockSpec((tm, kw), lambda j, i, ca=ca: (i, ca)) for _, ca, _, kw in terms]
                + [pl.BlockSpec((None, kw, tn), lambda j, i, rw=rw: (layer, rw, j)) for _, _, rw, kw in terms]
                + [pl.BlockSpec((tm, tn), lambda j, i: (i, j))])
    return pl.pallas_call(
        functools.partial(_matmul_res_kernel, n_in=len(terms)),
        grid=(n // tn, t // tm),
        in_specs=in_specs,
        out_specs=pl.BlockSpec((tm, tn), lambda j, i: (i, j)),
        out_shape=jax.ShapeDtypeStruct((t, n), F32),
        compiler_params=_cparams("parallel", "parallel"),
    )(*[a for a, _, _, _ in terms], *([w] * len(terms)), res)


def _rmsnorm_kernel(h_ref, g_ref, o_ref):
    x = h_ref[...]
    ms = jnp.mean(x * x, axis=-1, keepdims=True)
    o_ref[...] = x * lax.rsqrt(ms + NORM_EPS) * g_ref[...]


def rmsnorm_rows(h, gain, *, tm=512):
    t, k = h.shape
    return pl.pallas_call(
        _rmsnorm_kernel,
        grid=(t // tm,),
        in_specs=[pl.BlockSpec((tm, k), lambda i: (i, 0)), pl.BlockSpec((1, k), lambda i: (0, 0))],
        out_specs=pl.BlockSpec((tm, k), lambda i: (i, 0)),
        out_shape=jax.ShapeDtypeStruct((t, k), F32),
        compiler_params=_cparams("parallel"),
    )(h, gain.reshape(1, k))


def _ffn_up_kernel(h_ref, hh_ref, gain_ref, wg_ref, wv_ref, cwg_ref, cwv_ref, bg_ref, bv_ref, o_ref,
                   xn_ref, *, tiles_per_seq):
    tm, tn = o_ref.shape

    @pl.when(pl.program_id(1) == 0)
    def _():
        first = (pl.program_id(0) % tiles_per_seq) == 0

        def norm(x):
            ms = jnp.mean(x * x, axis=-1, keepdims=True)
            return x * lax.rsqrt(ms + NORM_EPS) * gain_ref[...]

        xn_ref[0:FFN_HALO, :] = jnp.where(first, 0.0, norm(hh_ref[...])).astype(BF16)
        xn_ref[FFN_HALO:, :] = norm(h_ref[...]).astype(BF16)

    def conv(u, cw_ref, b_ref, cs):
        acc = cw_ref[0:1, cs] * u
        for k in range(1, FFN_CONV_WIDTH):
            acc = pltpu.roll(acc, 1, axis=0) + cw_ref[k:k + 1, cs] * u
        return acc[FFN_HALO:FFN_HALO + tm, :] + b_ref[:, cs]

    cw = tn // FFN_CHUNKS
    for c in range(FFN_CHUNKS):
        cs = slice(c * cw, (c + 1) * cw)
        ug = _dot(xn_ref[...], wg_ref[:, cs])
        uv = _dot(xn_ref[...], wv_ref[:, cs])
        o_ref[:, cs] = (_silu(conv(ug, cwg_ref, bg_ref, cs)) * conv(uv, cwv_ref, bv_ref, cs)).astype(o_ref.dtype)


def ffn_up_gate(h, gain, w_up, layer, conv_w, conv_b, seq, *, tm=TM_PROJ, tn=TN_FFN):
    t, k = h.shape
    nf = D_FF // tn
    hb = tm // FFN_HALO
    return pl.pallas_call(
        functools.partial(_ffn_up_kernel, tiles_per_seq=seq // tm),
        grid=(t // tm, nf),
        in_specs=[pl.BlockSpec((tm, k), lambda i, j: (i, 0)),
                  pl.BlockSpec((FFN_HALO, k), lambda i, j: (jnp.maximum(i * hb - 1, 0), 0)),
                  pl.BlockSpec((1, k), lambda i, j: (0, 0)),
                  pl.BlockSpec((None, k, tn), lambda i, j: (layer, 0, j)),
                  pl.BlockSpec((None, k, tn), lambda i, j: (layer, 0, j + nf)),
                  pl.BlockSpec((FFN_CONV_WIDTH, tn), lambda i, j: (0, j)),
                  pl.BlockSpec((FFN_CONV_WIDTH, tn), lambda i, j: (0, j + nf)),
                  pl.BlockSpec((1, tn), lambda i, j: (0, j)),
                  pl.BlockSpec((1, tn), lambda i, j: (0, j + nf))],
        out_specs=pl.BlockSpec((tm, tn), lambda i, j: (i, j)),
        out_shape=jax.ShapeDtypeStruct((t, D_FF), BF16),
        scratch_shapes=[pltpu.VMEM((tm + FFN_HALO, k), BF16)],
        compiler_params=_cparams("parallel", "arbitrary"),
    )(h, h, gain.reshape(1, k), w_up, w_up, conv_w, conv_w, conv_b.reshape(1, -1), conv_b.reshape(1, -1))


def _ssd_kernel(z_ref, xs_ref, xsh_ref, bc_ref, bch_ref, dt_ref, cwx_ref, cbx_ref, cwb_ref, cbb_ref,
                dtb_ref, alog_ref, dsk_ref, gn_ref, y_ref, extx_ref, extb_ref, state_ref):
    L = SSM_CHUNK
    GW = SSM_INNER // SSM_GROUPS
    first = pl.program_id(1) == 0

    @pl.when(first)
    def _():
        state_ref[...] = jnp.zeros_like(state_ref)

    xs = _silu(_dwconv(extx_ref, xs_ref[...], xsh_ref[...], first, cwx_ref, SSM_CONV_WIDTH) + cbx_ref[...])
    bc = _silu(_dwconv(extb_ref, bc_ref[...], bch_ref[...], first, cwb_ref, SSM_CONV_WIDTH) + cbb_ref[...])

    dt = _softplus(dt_ref[...] + dtb_ref[...])
    a = -jnp.exp(alog_ref[...])
    la = dt * a
    row = lax.broadcasted_iota(jnp.int32, (L, L), 0)
    col = lax.broadcasted_iota(jnp.int32, (L, L), 1)
    causal = col <= row
    cum = _dot_f32_lhs01(causal.astype(BF16), la)
    cum_t = cum.T
    clast = cum[L - 1:L, :]

    e_head = (lax.broadcasted_iota(jnp.int32, (LANES, SSM_INNER), 1) // SSM_HEADDIM
              == lax.broadcasted_iota(jnp.int32, (LANES, SSM_INNER), 0)).astype(BF16)
    x = xs * _dot_f32_rhs01(dt, e_head)
    ecum_x = _dot_f32_rhs01(jnp.exp(cum), e_head)
    dte_x = _dot_f32_rhs01(jnp.exp(clast - cum), e_head)
    cdec_x = _dot_f32_rhs01(jnp.broadcast_to(jnp.exp(clast), (SUBLANES, LANES)), e_head)[0:1, :]
    xb = x.astype(BF16)
    xdte = (x * dte_x).astype(BF16)
    lo_half = lax.broadcasted_iota(jnp.int32, (L, LANES), 1) < SSM_HEADDIM

    for g in range(SSM_GROUPS):
        bg = bc[:, g * SSM_STATE:(g + 1) * SSM_STATE]
        cg = bc[:, (SSM_GROUPS + g) * SSM_STATE:(SSM_GROUPS + g + 1) * SSM_STATE].astype(BF16)
        cb = _dot_nt(cg, bg.astype(BF16))
        hg = state_ref[g]
        y_off = _dot(cg, hg.astype(BF16)) * ecum_x[:, g * GW:(g + 1) * GW]
        pieces = []
        for pr in range(GW // LANES):
            h0 = g * (GW // SSM_HEADDIM) + 2 * pr
            xp = xb[:, h0 * SSM_HEADDIM:h0 * SSM_HEADDIM + LANES]
            ypair = None
            for s in range(2):
                h = h0 + s
                diff = cum[:, h:h + 1] - cum_t[h:h + 1, :]
                decay = jnp.exp(jnp.where(causal, diff, -jnp.inf))
                m = (cb * decay).astype(BF16)
                keep = lo_half if s == 0 else jnp.logical_not(lo_half)
                yh = _dot(m, jnp.where(keep, xp, jnp.zeros_like(xp)))
                ypair = yh if ypair is None else ypair + yh
            pieces.append(ypair)
        y_diag = jnp.concatenate(pieces, axis=-1)
        st = _dot(bg.T.astype(BF16), xdte[:, g * GW:(g + 1) * GW])
        state_ref[g] = hg * cdec_x[:, g * GW:(g + 1) * GW] + st

        yg = y_diag + y_off + dsk_ref[:, g * GW:(g + 1) * GW] * xs[:, g * GW:(g + 1) * GW]
        yg = yg * _silu(z_ref[:, g * GW:(g + 1) * GW])
        ms = jnp.mean(yg * yg, axis=-1, keepdims=True)
        yg = yg * lax.rsqrt(ms + NORM_EPS) * gn_ref[:, g * GW:(g + 1) * GW]
        y_ref[:, g * GW:(g + 1) * GW] = yg.astype(y_ref.dtype)


def ssd_mixer(proj, dt_raw, conv_w, conv_b, dt_bias, a_log, d_skip, ssm_norm, bsz, seq):
    L = SSM_CHUNK
    nc = seq // L
    hb = L // SUBLANES
    pad = lambda v: jnp.pad(v.reshape(1, -1), ((0, 0), (0, LANES - v.shape[-1])))
    rowblk = lambda b, c: b * nc + c
    halo_row = lambda b, c: jnp.maximum((b * nc + c) * hb - 1, 0)
    const = lambda b, c: (0, 0)
    bcw = 2 * SSM_GROUPS * SSM_STATE
    return pl.pallas_call(
        _ssd_kernel,
        grid=(bsz, nc),
        in_specs=[pl.BlockSpec((L, SSM_INNER), lambda b, c: (rowblk(b, c), 0)),
                  pl.BlockSpec((L, SSM_INNER), lambda b, c: (rowblk(b, c), 1)),
                  pl.BlockSpec((SUBLANES, SSM_INNER), lambda b, c: (halo_row(b, c), 1)),
                  pl.BlockSpec((L, bcw), lambda b, c: (rowblk(b, c), 4)),
                  pl.BlockSpec((SUBLANES, bcw), lambda b, c: (halo_row(b, c), 4)),
                  pl.BlockSpec((L, LANES), lambda b, c: (rowblk(b, c), 0)),
                  pl.BlockSpec((SSM_CONV_WIDTH, SSM_INNER), const),
                  pl.BlockSpec((1, SSM_INNER), const),
                  pl.BlockSpec((SSM_CONV_WIDTH, bcw), const),
                  pl.BlockSpec((1, bcw), const),
                  pl.BlockSpec((1, LANES), const),
                  pl.BlockSpec((1, LANES), const),
                  pl.BlockSpec((1, SSM_INNER), const),
                  pl.BlockSpec((1, SSM_INNER), const)],
        out_specs=pl.BlockSpec((L, SSM_INNER), lambda b, c: (rowblk(b, c), 0)),
        out_shape=jax.ShapeDtypeStruct((bsz * seq, SSM_INNER), BF16),
        scratch_shapes=[pltpu.VMEM((L + SUBLANES, SSM_INNER), F32),
                        pltpu.VMEM((L + SUBLANES, bcw), F32),
                        pltpu.VMEM((SSM_GROUPS, SSM_STATE, SSM_INNER // SSM_GROUPS), F32)],
        compiler_params=_cparams("parallel", "arbitrary"),
    )(proj, proj, proj, proj, proj, dt_raw,
      conv_w[:, :SSM_INNER], conv_b[:SSM_INNER].reshape(1, -1),
      conv_w[:, SSM_INNER:], conv_b[SSM_INNER:].reshape(1, -1),
      pad(dt_bias), pad(a_log), jnp.repeat(d_skip, SSM_HEADDIM).reshape(1, -1), ssm_norm.reshape(1, -1))


def _conformer_kernel(ua_ref, ug_ref, uah_ref, ugh_ref, w_ref, b_ref, lg_ref, lb_ref, o_ref, ext_ref,
                      shift_ref, conv_ref, *, tiles_per_seq):
    first = (pl.program_id(0) % tiles_per_seq) == 0
    tm = ua_ref.shape[0]
    ext_ref[0:CONF_HALO, :] = jnp.where(first, 0.0, uah_ref[...] * jax.nn.sigmoid(ugh_ref[...]))
    ext_ref[CONF_HALO:, :] = ua_ref[...] * jax.nn.sigmoid(ug_ref[...])
    span = tm + CONF_HALO - SUBLANES
    for s in range(1, SUBLANES):
        shift_ref[s - 1] = ext_ref[s:s + span, :]
    for c0 in range(0, CONF_WIDTH, LANES):
        cs = slice(c0, c0 + LANES)
        acc = jnp.broadcast_to(b_ref[:, cs], (tm, LANES))
        for k in range(CONF_CONV_WIDTH):
            q, s = divmod(CONF_HALO - (CONF_CONV_WIDTH - 1) + k, SUBLANES)
            rows = slice(q * SUBLANES, q * SUBLANES + tm)
            tap = ext_ref[rows, cs] if s == 0 else shift_ref[s - 1, rows, cs]
            acc = acc + w_ref[k:k + 1, cs] * tap
        conv_ref[:, cs] = acc
    c = conv_ref[...]
    mu = jnp.mean(c, axis=-1, keepdims=True)
    d = c - mu
    var = jnp.mean(d * d, axis=-1, keepdims=True)
    y = d * lax.rsqrt(var + NORM_EPS) * lg_ref[...] + lb_ref[...]
    o_ref[...] = _silu(y).astype(o_ref.dtype)


def conformer_mixer(proj, dw_w, dw_b, ln_g, ln_b, seq, *, tm=TM_CONF):
    t = proj.shape[0]
    hb = tm // CONF_HALO
    halo_row = lambda i: jnp.maximum(i * hb - 1, 0)
    const = lambda i: (0, 0)
    return pl.pallas_call(
        functools.partial(_conformer_kernel, tiles_per_seq=seq // tm),
        grid=(t // tm,),
        in_specs=[pl.BlockSpec((tm, CONF_WIDTH), lambda i: (i, 5)),
                  pl.BlockSpec((tm, CONF_WIDTH), lambda i: (i, 6)),
                  pl.BlockSpec((CONF_HALO, CONF_WIDTH), lambda i: (halo_row(i), 5)),
                  pl.BlockSpec((CONF_HALO, CONF_WIDTH), lambda i: (halo_row(i), 6)),
                  pl.BlockSpec((CONF_CONV_WIDTH, CONF_WIDTH), const),
                  pl.BlockSpec((1, CONF_WIDTH), const),
                  pl.BlockSpec((1, CONF_WIDTH), const),
                  pl.BlockSpec((1, CONF_WIDTH), const)],
        out_specs=pl.BlockSpec((tm, CONF_WIDTH), lambda i: (i, 0)),
        out_shape=jax.ShapeDtypeStruct((t, CONF_WIDTH), BF16),
        scratch_shapes=[pltpu.VMEM((tm + CONF_HALO, CONF_WIDTH), F32),
                        pltpu.VMEM((SUBLANES - 1, tm + CONF_HALO - SUBLANES, CONF_WIDTH), F32),
                        pltpu.VMEM((tm, CONF_WIDTH), F32)],
        compiler_params=_cparams("parallel"),
    )(proj, proj, proj, proj, dw_w, dw_b.reshape(1, -1), ln_g.reshape(1, -1), ln_b.reshape(1, -1))


def _shortconv_kernel(b_ref, c_ref, h_ref, ch_ref, hh_ref, w_ref, o_ref, ext_ref, *, tiles_per_seq):
    first = (pl.program_id(0) % tiles_per_seq) == 0
    conv = _dwconv(ext_ref, c_ref[...] * h_ref[...], ch_ref[...] * hh_ref[...], first, w_ref, SC_CONV_WIDTH)
    o_ref[...] = (b_ref[...] * conv).astype(o_ref.dtype)


def shortconv_mixer(proj, conv_w, seq, *, tm=TM_SC):
    t = proj.shape[0]
    hb = tm // SUBLANES
    halo_row = lambda i: jnp.maximum(i * hb - 1, 0)
    return pl.pallas_call(
        functools.partial(_shortconv_kernel, tiles_per_seq=seq // tm),
        grid=(t // tm,),
        in_specs=[pl.BlockSpec((tm, SC_WIDTH), lambda i: (i, 0)),
                  pl.BlockSpec((tm, SC_WIDTH), lambda i: (i, 1)),
                  pl.BlockSpec((tm, SC_WIDTH), lambda i: (i, 2)),
                  pl.BlockSpec((SUBLANES, SC_WIDTH), lambda i: (halo_row(i), 1)),
                  pl.BlockSpec((SUBLANES, SC_WIDTH), lambda i: (halo_row(i), 2)),
                  pl.BlockSpec((SC_CONV_WIDTH, SC_WIDTH), lambda i: (0, 0))],
        out_specs=pl.BlockSpec((tm, SC_WIDTH), lambda i: (i, 0)),
        out_shape=jax.ShapeDtypeStruct((t, SC_WIDTH), BF16),
        scratch_shapes=[pltpu.VMEM((tm + SUBLANES, SC_WIDTH), F32)],
        compiler_params=_cparams("parallel"),
    )(proj, proj, proj, proj, proj, conv_w)


def _rope_table_kernel(pos_ref, inv_ref, cos_ref, sin_ref):
    ang = pos_ref[...].astype(F32) * inv_ref[...]
    sign = jnp.where(lax.broadcasted_iota(jnp.int32, ang.shape, 1) < NSA_HEAD_DIM // 2, -1.0, 1.0)
    cos_ref[...] = jnp.cos(ang)
    sin_ref[...] = jnp.sin(ang) * sign


def rope_tables(positions, *, tm=TM_ROPE):
    t = positions.size
    half = NSA_HEAD_DIM // 2
    inv = ROPE_THETA ** (-jnp.arange(half, dtype=F32) / half)
    inv = jnp.concatenate([inv, inv]).reshape(1, NSA_HEAD_DIM)
    tab = jax.ShapeDtypeStruct((t, NSA_HEAD_DIM), F32)
    return pl.pallas_call(
        _rope_table_kernel,
        grid=(t // tm,),
        in_specs=[pl.BlockSpec((tm, 1), lambda i: (i, 0)), pl.BlockSpec((1, NSA_HEAD_DIM), lambda i: (0, 0))],
        out_specs=[pl.BlockSpec((tm, NSA_HEAD_DIM), lambda i: (i, 0))] * 2,
        out_shape=[tab, tab],
        compiler_params=_cparams("parallel"),
    )(positions.reshape(t, 1), inv)


def _rope(x, cos, sin_signed):
    return x * cos + pltpu.roll(x, NSA_HEAD_DIM // 2, axis=1) * sin_signed


def _kv_prep_kernel(ks_ref, vs_ref, kw_ref, vw_ref, cos_ref, sin_ref, kso_ref, vso_ref, kwo_ref, vwo_ref,
                    *, tiles_per_seq):
    D = NSA_HEAD_DIM
    tm = ks_ref.shape[0]
    cos, sin = cos_ref[...], sin_ref[...]
    row = (pl.program_id(0) % tiles_per_seq) * tm + lax.broadcasted_iota(jnp.int32, (tm, D), 0)
    lane = lax.broadcasted_iota(jnp.int32, (tm, D), 1)
    onehot = jnp.where(row // SLC_BLOCK == lane, 1.0, 0.0).astype(BF16)
    ones_col = jnp.where(lane == 0, 1.0, 0.0).astype(BF16)
    for g in range(NSA_KV_HEADS):
        sl = slice(g * D, (g + 1) * D)
        kso_ref[:, 2 * g * D:(2 * g + 1) * D] = _rope(ks_ref[:, sl], cos, sin).astype(BF16)
        kso_ref[:, (2 * g + 1) * D:(2 * g + 2) * D] = onehot
        kwo_ref[:, sl] = _rope(kw_ref[:, sl], cos, sin).astype(BF16)
        vso_ref[:, 2 * g * D:(2 * g + 1) * D] = vs_ref[:, sl].astype(BF16)
        vso_ref[:, (2 * g + 1) * D:(2 * g + 2) * D] = ones_col
        vwo_ref[:, 2 * g * D:(2 * g + 1) * D] = vw_ref[:, sl].astype(BF16)
        vwo_ref[:, (2 * g + 1) * D:(2 * g + 2) * D] = ones_col


def kv_prep(proj, cos, sin, seq, *, tm=512):
    t = proj.shape[0]
    kv = jax.ShapeDtypeStruct((t, NSA_KV), BF16)
    kv_aug = jax.ShapeDtypeStruct((t, 2 * NSA_KV), BF16)
    spec = pl.BlockSpec((tm, NSA_KV), lambda i: (i, 0))
    spec_aug = pl.BlockSpec((tm, 2 * NSA_KV), lambda i: (i, 0))
    return pl.pallas_call(
        functools.partial(_kv_prep_kernel, tiles_per_seq=seq // tm),
        grid=(t // tm,),
        in_specs=[pl.BlockSpec((tm, NSA_KV), lambda i: (i, 12)),
                  pl.BlockSpec((tm, NSA_KV), lambda i: (i, 13)),
                  pl.BlockSpec((tm, NSA_KV), lambda i: (i, 14)),
                  pl.BlockSpec((tm, NSA_KV), lambda i: (i, 15)),
                  pl.BlockSpec((tm, NSA_HEAD_DIM), lambda i: (i, 0)),
                  pl.BlockSpec((tm, NSA_HEAD_DIM), lambda i: (i, 0))],
        out_specs=[spec_aug, spec_aug, spec, spec_aug],
        out_shape=[kv_aug, kv_aug, kv, kv_aug],
        compiler_params=_cparams("parallel"),
    )(proj, proj, proj, proj, cos, sin)


def _compress_kernel(kt_ref, vt_ref, cos_ref, sin_ref, pe_ref, w1_ref, w2_ref, kc_ref, vc_ref):
    nb = kc_ref.shape[0]
    half = CMP_BLOCK // 2

    def compress(tok_ref, which):
        acc_lo = jnp.zeros((nb, NSA_HEAD_DIM), F32)
        acc_hi = jnp.zeros((nb, NSA_HEAD_DIM), F32)
        for l in range(half):
            r = tok_ref[pl.ds(l, nb, stride=CMP_STRIDE), :]
            lo = (r + pe_ref[which, l:l + 1, :]).astype(BF16)
            hi = (r + pe_ref[which, half + l:half + l + 1, :]).astype(BF16)
            acc_lo = acc_lo + _dot(lo, w1_ref[which, l * NSA_HEAD_DIM:(l + 1) * NSA_HEAD_DIM, :])
            acc_hi = acc_hi + _dot(hi, w1_ref[which, (half + l) * NSA_HEAD_DIM:(half + l + 1) * NSA_HEAD_DIM, :])
        pre = acc_lo + pltpu.roll(acc_hi, nb - 1, axis=0)
        return _dot(_silu(pre).astype(BF16), w2_ref[which])

    cos_e = pltpu.roll(cos_ref[pl.ds(CMP_STRIDE - 1, nb, stride=CMP_STRIDE), :], nb - 1, axis=0)
    sin_e = pltpu.roll(sin_ref[pl.ds(CMP_STRIDE - 1, nb, stride=CMP_STRIDE), :], nb - 1, axis=0)
    kc_ref[...] = _rope(compress(kt_ref, 0), cos_e, sin_e).astype(BF16)
    vc_ref[...] = compress(vt_ref, 1).astype(BF16)


def compress_kv(proj, cos, sin, pe, w1, w2, bsz, seq):
    nb = seq // CMP_STRIDE
    out = jax.ShapeDtypeStruct((bsz, NSA_KV_HEADS, nb, NSA_HEAD_DIM), BF16)
    full = lambda *shape: pl.BlockSpec(shape, lambda b, g: (0,) * len(shape))
    return pl.pallas_call(
        _compress_kernel,
        grid=(bsz, NSA_KV_HEADS),
        in_specs=[pl.BlockSpec((seq, NSA_HEAD_DIM), lambda b, g: (b, 40 + g)),
                  pl.BlockSpec((seq, NSA_HEAD_DIM), lambda b, g: (b, 44 + g)),
                  pl.BlockSpec((seq, NSA_HEAD_DIM), lambda b, g: (b, 0)),
                  pl.BlockSpec((seq, NSA_HEAD_DIM), lambda b, g: (b, 0)),
                  full(2, CMP_BLOCK, NSA_HEAD_DIM),
                  full(2, CMP_BLOCK * NSA_HEAD_DIM, NSA_HEAD_DIM),
                  full(2, NSA_HEAD_DIM, NSA_HEAD_DIM)],
        out_specs=[pl.BlockSpec((None, None, nb, NSA_HEAD_DIM), lambda b, g: (b, g, 0, 0))] * 2,
        out_shape=[out, out],
        compiler_params=_cparams("parallel", "parallel"),
    )(proj, proj, cos, sin, pe, w1, w2)


def _nsa_kernel(q_ref, cos_ref, sin_ref, gate_ref, kc_ref, vc_ref, ks_ref, vs_ref, kw_ref, vw_ref,
                o_ref, qa_ref, impt_ref, sc_ref, mrun_ref, acc_ref):
    G = NSA_GROUP
    D = NSA_HEAD_DIM
    R = G * TQ
    base = pl.program_id(2) * (NSA_TILES * TQ)
    scale = D ** -0.5
    n_cmp = kc_ref.shape[0]
    n_slc = ks_ref.shape[0] // SLC_BLOCK
    wk = WINDOW + TQ
    j = lax.broadcasted_iota(jnp.int32, (1, LANES), 1)
    ends = lax.broadcasted_iota(jnp.int32, (1, n_cmp), 1) * CMP_STRIDE + (CMP_BLOCK - 1)
    n_i = lax.broadcasted_iota(jnp.int32, (n_cmp, LANES), 0)
    j_i = lax.broadcasted_iota(jnp.int32, (n_cmp, LANES), 1)
    overlap = ((n_i * CMP_STRIDE < (j_i + 1) * SLC_BLOCK)
               & (n_i * CMP_STRIDE + CMP_BLOCK - 1 >= j_i * SLC_BLOCK)).astype(BF16)
    rel = lax.broadcasted_iota(jnp.int32, (TQ, wk), 1) - lax.broadcasted_iota(jnp.int32, (TQ, wk), 0)
    lower = (lax.broadcasted_iota(jnp.int32, (TQ, TQ), 1) <= lax.broadcasted_iota(jnp.int32, (TQ, TQ), 0))
    own_bias = jnp.where(lower, 0.0, NEG_BIG)
    sub = lax.broadcasted_iota(jnp.int32, (SUBLANES, TQ), 0)

    o_cmp, o_win, s_own = [], [], []
    for u in range(NSA_TILES):
        rows = slice(u * TQ, (u + 1) * TQ)
        q0 = base + u * TQ
        cos, sin = cos_ref[rows, :], sin_ref[rows, :]
        for e in range(G):
            xq = _rope(q_ref[rows, e * D:(e + 1) * D], cos, sin)
            qa_ref[u, e * TQ:(e + 1) * TQ, 0:D] = (xq * scale).astype(BF16)
        qs = qa_ref[u, :, 0:D]
        t_col = q0 + lax.broadcasted_iota(jnp.int32, (TQ, 1), 0)

        cvalid = ends <= t_col
        cbias = jnp.where(cvalid, 0.0, NEG_BIG)
        c01 = jnp.where(cvalid, 1.0, 0.0)
        s = _dot_nt(qs, kc_ref[...]).reshape(G, TQ, n_cmp) + cbias[None]
        m = jnp.max(s, axis=-1, keepdims=True)
        p = jnp.exp(s - m) * c01[None]
        l = jnp.sum(p, axis=-1, keepdims=True)
        p = p * (1.0 / jnp.maximum(l, 1e-30))
        o_cmp.append(_dot(p.reshape(R, n_cmp).astype(BF16), vc_ref[...]))

        start = pl.multiple_of(jnp.maximum(q0 - WINDOW, 0), TQ)
        d = q0 - start
        wbias = jnp.where((rel <= d) & (rel > d - WINDOW), 0.0, NEG_BIG)
        sw = _dot_nt(qs, kw_ref[pl.ds(start, wk), :]).reshape(G, TQ, wk) + wbias[None]
        mw = jnp.max(sw, axis=-1, keepdims=True)
        pw = jnp.exp(sw - mw).reshape(R, wk)
        accw = _dot(pw.astype(BF16), vw_ref[pl.ds(start, wk), :])
        o_win.append(accw[:, 0:D] * (1.0 / accw[:, D:D + 1]))

        psum = p[0]
        for e in range(1, G):
            psum = psum + p[e]
        imp = _dot_f32_rhs01(psum, overlap)
        cur = t_col // SLC_BLOCK
        forced = (j == 0) | ((j <= cur) & (j > cur - N_LOCAL))
        valid = j * SLC_BLOCK <= t_col
        imp = jnp.where(forced, FORCE, jnp.where(valid, imp, -FORCE))
        impt_ref[u] = imp.T
        nrb = n_slc // SUBLANES
        vals = [impt_ref[u, rb * SUBLANES:(rb + 1) * SUBLANES, :] for rb in range(nrb)]
        cnt = [jnp.zeros((SUBLANES, TQ), F32) for _ in range(nrb)]
        for k in range(n_slc):
            vk = jnp.broadcast_to(impt_ref[u, k:k + 1, :], (SUBLANES, TQ))
            for rb in range(nrb):
                if rb * SUBLANES > k:
                    beats = vk >= vals[rb]
                elif rb * SUBLANES + SUBLANES - 1 < k:
                    beats = vk > vals[rb]
                else:
                    beats = (vk > vals[rb]) | ((vk == vals[rb]) & (sub > k - rb * SUBLANES))
                cnt[rb] = cnt[rb] + jnp.where(beats, 1.0, 0.0)
        sel_t = jnp.concatenate([jnp.where(c < N_SELECT, 1.0, 0.0) for c in cnt]
                                + [jnp.zeros((LANES - n_slc, TQ), F32)], axis=0)
        sel = sel_t.T
        sbias = jnp.where((sel > 0.5) & (j * SLC_BLOCK < q0), 0.0, NEG_BIG).astype(BF16)
        for e in range(G):
            qa_ref[u, e * TQ:(e + 1) * TQ, D:2 * D] = sbias

        s_own.append((_dot_nt(qs, ks_ref[pl.ds(q0, TQ), 0:D]).reshape(G, TQ, TQ)
                      + own_bias[None]).reshape(R, TQ))
        mrun_ref[u] = s_own[u]

    n_kt = (base + (NSA_TILES - 1) * TQ + TK - 1) // TK

    def score_body(kt, carry):
        k0 = pl.multiple_of(kt * TK, TK)
        for u in range(NSA_TILES):
            s = _dot_nt(qa_ref[u], ks_ref[pl.ds(k0, TK), :])
            sc_ref[u, kt] = s
            t = s[:, 0:LANES]
            for c in range(1, TK // LANES):
                t = jnp.maximum(t, s[:, c * LANES:(c + 1) * LANES])
            mrun_ref[u] = jnp.maximum(mrun_ref[u], t)
        return carry

    lax.fori_loop(0, n_kt, score_body, 0)
    m_row = []
    for u in range(NSA_TILES):
        m_row.append(jnp.max(mrun_ref[u], axis=-1, keepdims=True))
        acc_ref[u] = _dot(jnp.exp(s_own[u] - m_row[u]).astype(BF16),
                          vs_ref[pl.ds(base + u * TQ, TQ), :])

    def pv_body(kt, carry):
        k0 = pl.multiple_of(kt * TK, TK)
        for u in range(NSA_TILES):
            p = jnp.exp(sc_ref[u, kt] - m_row[u]).astype(BF16)
            acc_ref[u] += _dot(p, vs_ref[pl.ds(k0, TK), :])
        return carry

    lax.fori_loop(0, n_kt, pv_body, 0)

    for u in range(NSA_TILES):
        rows = slice(u * TQ, (u + 1) * TQ)
        acc = acc_ref[u]
        o_slc = acc[:, 0:D] * (1.0 / acc[:, D:D + 1])
        gate = jax.nn.sigmoid(gate_ref[rows, :])
        for e in range(G):
            hr = slice(e * TQ, (e + 1) * TQ)
            o = (gate[:, 3 * e:3 * e + 1] * o_cmp[u][hr] + gate[:, 3 * e + 1:3 * e + 2] * o_slc[hr]
                 + gate[:, 3 * e + 2:3 * e + 3] * o_win[u][hr])
            o_ref[rows, e * D:(e + 1) * D] = o.astype(o_ref.dtype)


def nsa_attention(proj, gates, cos, sin, kc, vc, ks, vs, kw, vw, bsz, seq):
    tq = NSA_TILES * TQ
    nq = seq // tq
    qw = NSA_GROUP * NSA_HEAD_DIM
    rows = NSA_GROUP * TQ
    row = lambda b, g, i: b * nq + i
    kvspec = pl.BlockSpec((seq, NSA_HEAD_DIM), lambda b, g, i: (b, g))
    augspec = pl.BlockSpec((seq, 2 * NSA_HEAD_DIM), lambda b, g, i: (b, g))
    cspec = pl.BlockSpec((None, None, seq // CMP_STRIDE, NSA_HEAD_DIM), lambda b, g, i: (b, g, 0, 0))
    return pl.pallas_call(
        _nsa_kernel,
        grid=(bsz, NSA_KV_HEADS, nq),
        in_specs=[pl.BlockSpec((tq, qw), lambda b, g, i: (row(b, g, i), 6 + g)),
                  pl.BlockSpec((tq, NSA_HEAD_DIM), lambda b, g, i: (row(b, g, i), 0)),
                  pl.BlockSpec((tq, NSA_HEAD_DIM), lambda b, g, i: (row(b, g, i), 0)),
                  pl.BlockSpec((tq, LANES), lambda b, g, i: (row(b, g, i), g)),
                  cspec, cspec, augspec, augspec, kvspec, augspec],
        out_specs=pl.BlockSpec((tq, qw), lambda b, g, i: (row(b, g, i), g)),
        out_shape=jax.ShapeDtypeStruct((bsz * seq, NSA_Q), BF16),
        scratch_shapes=[pltpu.VMEM((NSA_TILES, rows, 2 * NSA_HEAD_DIM), BF16),
                        pltpu.VMEM((NSA_TILES, LANES, TQ), F32),
                        pltpu.VMEM((NSA_TILES, seq // TK, rows, TK), F32),
                        pltpu.VMEM((NSA_TILES, rows, LANES), F32),
                        pltpu.VMEM((NSA_TILES, rows, 2 * NSA_HEAD_DIM), F32)],
        compiler_params=_cparams("parallel", "parallel", "parallel"),
    )(proj, cos, sin, gates, kc, vc, ks, vs, kw, vw)


def _even_layer(h, gain, w_in, conv_w, conv_b, dt_bias, a_log, d_skip, ssm_norm,
                dw_w, dw_b, ln_g, ln_b, w_out, i, bsz, seq):
    xbc_end = SSM_INNER + SSM_INNER + 2 * SSM_GROUPS * SSM_STATE
    dt_end = xbc_end + SSM_HEADS
    w_main = jnp.concatenate([w_in[:, :xbc_end], w_in[:, dt_end:]], axis=1).astype(BF16)
    w_dt = jnp.pad(w_in[:, xbc_end:dt_end], ((0, 0), (0, LANES - SSM_HEADS))).astype(BF16)
    proj, dt_raw = norm_matmul(h, gain, w_main, w_dt)
    y = ssd_mixer(proj, dt_raw, conv_w, conv_b, dt_bias, a_log, d_skip, ssm_norm, bsz, seq)
    u = conformer_mixer(proj, dw_w, dw_b, ln_g, ln_b, seq)
    return matmul_res([(y, 0, 0, SSM_INNER), (u, 0, SSM_INNER // CONF_WIDTH, CONF_WIDTH)], w_out, i, h)


def _odd_layer(h, gain, cos, sin, w_in, w_gates, sc_w, pe, w1, w2, w_out, i, bsz, seq):
    main = 3 * SC_WIDTH + NSA_Q + 6 * NSA_KV
    per_g = 3 * NSA_GROUP
    wg = w_gates.reshape(D_MODEL, NSA_KV_HEADS, per_g)
    wg = jnp.pad(wg, ((0, 0), (0, 0), (0, LANES - per_g))).reshape(D_MODEL, NSA_KV_HEADS * LANES)
    proj, gates = norm_matmul(h, gain, w_in, wg.astype(BF16), layer=i, n=main)
    y_c = shortconv_mixer(proj, sc_w, seq)
    ks, vs, kw, vw = kv_prep(proj, cos, sin, seq)
    kc, vc = compress_kv(proj, cos, sin, pe, w1, w2, bsz, seq)
    y_d = nsa_attention(proj, gates, cos, sin, kc, vc, ks, vs, kw, vw, bsz, seq)
    return matmul_res([(y_c, 0, 0, SC_WIDTH), (y_d, 0, 1, SC_WIDTH), (y_d, 1, 2, SC_WIDTH)], w_out, i, h)


def _ffn(h, gain, w_up, conv_w, conv_b, w_down, layer, seq):
    a = ffn_up_gate(h, gain, w_up, layer, conv_w, conv_b, seq)
    return matmul_res([(a, 0, 0, D_FF)], w_down, layer, h, tn=TN_DOWN)


def kernel(x, positions, mix_norm, ffn_norm, final_norm, ab_w_in, ssm_conv_w, ssm_conv_b, ssm_dt_bias, ssm_a_log, ssm_d, ssm_norm, conf_dw_w, conf_dw_b, conf_ln_g, conf_ln_b, ab_w_out, cd_w_in, sc_conv_w, nsa_cmp_pe, nsa_cmp_w1, nsa_cmp_w2, cd_w_out, ffn_w_up, ffn_conv_w, ffn_conv_b, ffn_w_down):
    bsz, seq, d = x.shape
    depth = mix_norm.shape[0]
    h = x.reshape(bsz * seq, d)
    cos, sin = rope_tables(positions)
    gate_cols = 3 * SC_WIDTH + NSA_Q + 6 * NSA_KV
    ab_w_out_b, cd_w_in_b, cd_w_out_b = ab_w_out.astype(BF16), cd_w_in.astype(BF16), cd_w_out.astype(BF16)
    ffn_w_up_b, ffn_w_down_b = ffn_w_up.astype(BF16), ffn_w_down.astype(BF16)
    cmp_w1_b, cmp_w2_b = nsa_cmp_w1.astype(BF16), nsa_cmp_w2.astype(BF16)
    for layer in range(depth):
        i = layer // 2
        if layer % 2 == 0:
            h = _even_layer(h, mix_norm[layer], ab_w_in[i], ssm_conv_w[i], ssm_conv_b[i], ssm_dt_bias[i],
                            ssm_a_log[i], ssm_d[i], ssm_norm[i], conf_dw_w[i], conf_dw_b[i], conf_ln_g[i],
                            conf_ln_b[i], ab_w_out_b, i, bsz, seq)
        else:
            h = _odd_layer(h, mix_norm[layer], cos, sin, cd_w_in_b, cd_w_in[i][:, gate_cols:], sc_conv_w[i],
                           nsa_cmp_pe[i], cmp_w1_b[i], cmp_w2_b[i], cd_w_out_b, i, bsz, seq)
        h = _ffn(h, ffn_norm[layer], ffn_w_up_b, ffn_conv_w[layer], ffn_conv_b[layer], ffn_w_down_b, layer, seq)
    return rmsnorm_rows(h, final_norm).reshape(bsz, seq, d)
```
